```python
import jax, jax.numpy as jnp
from jax import lax
import numpy as np

D_MODEL = 2048
BATCH = 4
SEQ = 2048
DEPTH = 4
DEC_BATCH = 128
DEC_SEQ = 4
PAST_LEN = 16384
PAGE_SIZE = 128

H_A = 8
DK = 128
DV = 128
W_QK = H_A * DK
W_A = H_A * DV
W_B = D_MODEL - W_A
LRU_BLOCKS = 8
LRU_BW = W_B // LRU_BLOCKS
LRU_C = 8.0
CONV_W = 4
CHUNK = 64
D_FF = ((8 * D_MODEL // 3 + 127) // 128) * 128
HALF = 0.5
EPS = 1e-6
OFF_Q = 0
OFF_K = OFF_Q + W_QK
OFF_V = OFF_K + W_QK
OFF_G = OFF_V + W_A
OFF_A = OFF_G + W_A
OFF_B = OFF_A + H_A
OFF_X = OFF_B + H_A
OFF_Y = OFF_X + W_B
N_IN = OFF_Y + W_B
N_DCONV = 2 * W_QK + W_A

kernel_name = "hymba_deltanet_rglru_macaron_step"


def rmsnorm(x, g):
    xf = x.astype(jnp.float32)
    y = xf * lax.rsqrt(jnp.mean(xf * xf, axis=-1, keepdims=True) + EPS)
    return (y * g.astype(jnp.float32)).astype(x.dtype)


def l2norm(t):
    return t * lax.rsqrt(jnp.sum(t * t, axis=-1, keepdims=True) + EPS)


def modulate(h, shift, scale):
    return h * (1 + scale[:, None, :]) + shift[:, None, :]


def swiglu(h, w1, w3, w2):
    return (jax.nn.silu(h @ w1) * (h @ w3)) @ w2


def causal_conv(x, prev, w):
    xp = jnp.concatenate([prev.astype(x.dtype), x], axis=1)
    L = x.shape[1]
    y = xp[:, 0:L] * w[0]
    for j in range(1, CONV_W):
        y = y + xp[:, j:j + L] * w[j]
    return y, xp[:, -(CONV_W - 1):]


def gated_delta_rule(q, k, v, g, beta, s0):
    B, L, H, _ = q.shape
    C = min(CHUNK, L)
    n = -(-L // C)
    pad = n * C - L

    def blocks(t):
        t = jnp.pad(t, [(0, 0), (0, pad)] + [(0, 0)] * (t.ndim - 2))
        t = t.reshape((B, n, C) + t.shape[2:])
        return jnp.moveaxis(t, 3, 1)

    qb, kb, vb, gb, bb = blocks(q), blocks(k), blocks(v), blocks(g), blocks(beta)
    gc = jnp.cumsum(gb, axis=-1)
    causal = jnp.tril(jnp.ones((C, C), dtype=bool))
    strict = jnp.tril(jnp.ones((C, C), dtype=bool), -1)
    decay = jnp.exp(jnp.where(causal, gc[..., :, None] - gc[..., None, :], -jnp.inf))
    kk = jnp.einsum('bhncd,bhnmd->bhncm', kb, kb)
    m = jnp.where(strict, bb[..., None] * kk * decay, 0.0)
    tri = jnp.eye(C, dtype=q.dtype) + m
    rhs = jnp.concatenate([kb * (bb * jnp.exp(gc))[..., None], vb * bb[..., None]], axis=-1)
    sol = lax.linalg.triangular_solve(tri, rhs, left_side=True, lower=True, unit_diagonal=True)
    w_blk, u_blk = sol[..., :DK], sol[..., DK:]
    qk = jnp.einsum('bhncd,bhnmd->bhncm', qb, kb) * decay
    q_dec = qb * jnp.exp(gc)[..., None]
    k_dec = kb * jnp.exp(gc[..., -1:] - gc)[..., None]
    g_last = jnp.exp(gc[..., -1])

    def step(S, xs):
        w_c, u_c, qk_c, qd_c, kd_c, gl_c = xs
        u = u_c - jnp.einsum('bhcd,bhde->bhce', w_c, S)
        o = jnp.einsum('bhcd,bhde->bhce', qd_c, S) + jnp.einsum('bhcm,bhme->bhce', qk_c, u)
        S = S * gl_c[..., None, None] + jnp.einsum('bhcd,bhce->bhde', kd_c, u)
        return S, o

    xs = tuple(jnp.moveaxis(t, 2, 0) for t in (w_blk, u_blk, qk, q_dec, k_dec, g_last))
    s_new, o = lax.scan(step, s0, xs)
    o = jnp.transpose(o, (1, 0, 3, 2, 4)).reshape(B, n * C, H, DV)[:, :L]
    return o, s_new


def delta_mixer(proj, conv_prev, s0, conv_w, a_log, dt_bias, onorm_g):
    B, L, _ = proj.shape
    f32 = jnp.float32
    qkv, conv_new = causal_conv(proj[..., OFF_Q:OFF_G], conv_prev, conv_w)
    qkv = jax.nn.silu(qkv.astype(f32))
    q = l2norm(qkv[..., OFF_Q:OFF_K].reshape(B, L, H_A, DK)) * (DK ** -0.5)
    k = l2norm(qkv[..., OFF_K:OFF_V].reshape(B, L, H_A, DK))
    v = qkv[..., OFF_V:OFF_G].reshape(B, L, H_A, DV)
    g = -jnp.exp(a_log.astype(f32)) * jax.nn.softplus(proj[..., OFF_A:OFF_B].astype(f32) + dt_bias.astype(f32))
    beta = jax.nn.sigmoid(proj[..., OFF_B:OFF_X].astype(f32))
    o, s_new = gated_delta_rule(q, k, v, g, beta, s0.astype(f32))
    gate = jax.nn.silu(proj[..., OFF_G:OFF_A].astype(f32)).reshape(B, L, H_A, DV)
    o = rmsnorm(o, onorm_g) * gate
    return o.reshape(B, L, W_A).astype(proj.dtype), conv_new, s_new.astype(s0.dtype)


def lru_mixer(proj, conv_prev, h0, conv_w, conv_b, w_a, b_a, w_x, b_x, lam):
    f32 = jnp.float32
    xc, conv_new = causal_conv(proj[..., OFF_X:OFF_Y], conv_prev, conv_w)
    xc = xc + conv_b
    B, L, _ = xc.shape
    xb = xc.reshape(B, L, LRU_BLOCKS, LRU_BW)
    r = jax.nn.sigmoid(jnp.einsum('blni,nij->blnj', xb, w_a).reshape(B, L, W_B).astype(f32) + b_a.astype(f32))
    i = jax.nn.sigmoid(jnp.einsum('blni,nij->blnj', xb, w_x).reshape(B, L, W_B).astype(f32) + b_x.astype(f32))
    log_a = -LRU_C * r * jax.nn.softplus(-lam.astype(f32))
    a = jnp.exp(log_a)
    b = jnp.sqrt(-jnp.expm1(2.0 * log_a)) * (i * xc.astype(f32))
    b = b.at[:, 0].add(a[:, 0] * h0.astype(f32))

    def comb(left, right):
        a1, b1 = left
        a2, b2 = right
        return a1 * a2, a2 * b1 + b2

    _, h = lax.associative_scan(comb, (a, b), axis=1)
    out = h * jax.nn.gelu(proj[..., OFF_Y:N_IN].astype(f32))
    return out.astype(proj.dtype), conv_new, h[:, -1].astype(h0.dtype)


def trunk(x, c, s_delta, s_dconv, s_lru, s_lconv, w_in, w_out, norm_g, w_ada, b_ada,
          ffn_w1, ffn_w3, ffn_w2, dconv_w, d_alog, d_dtbias, d_onorm,
          lconv_w, lconv_b, lru_wa, lru_ba, lru_wx, lru_bx, lru_lam, final_g):
    B = x.shape[0]
    cs = jax.nn.silu(c)
    new_d, new_dc, new_l, new_lc = [], [], [], []
    for l in range(DEPTH):
        mod = (cs @ w_ada[l] + b_ada[l]).reshape(B, 3, 3, D_MODEL).astype(x.dtype)
        h = modulate(rmsnorm(x, norm_g[l, 0]), mod[:, 0, 0], mod[:, 0, 1])
        x = x + HALF * mod[:, 0, 2][:, None] * swiglu(h, ffn_w1[l, 0], ffn_w3[l, 0], ffn_w2[l, 0])
        h = modulate(rmsnorm(x, norm_g[l, 1]), mod[:, 1, 0], mod[:, 1, 1])
        proj = h @ w_in[l]
        oa, dc, sd = delta_mixer(proj, s_dconv[l], s_delta[l], dconv_w[l], d_alog[l], d_dtbias[l], d_onorm[l])
        ob, lc, sl = lru_mixer(proj, s_lconv[l], s_lru[l], lconv_w[l], lconv_b[l],
                               lru_wa[l], lru_ba[l], lru_wx[l], lru_bx[l], lru_lam[l])
        x = x + mod[:, 1, 2][:, None] * (jnp.concatenate([oa, ob], axis=-1) @ w_out[l])
        h = modulate(rmsnorm(x, norm_g[l, 2]), mod[:, 2, 0], mod[:, 2, 1])
        x = x + HALF * mod[:, 2, 2][:, None] * swiglu(h, ffn_w1[l, 1], ffn_w3[l, 1], ffn_w2[l, 1])
        new_d.append(sd)
        new_dc.append(dc)
        new_l.append(sl)
        new_lc.append(lc)
    y = rmsnorm(x, final_g)
    return y, jnp.stack(new_d), jnp.stack(new_dc), jnp.stack(new_l), jnp.stack(new_lc)


def setup_inputs(seed: int = 0) -> dict:
    key = jax.random.key(seed)
    ks = jax.random.split(key, 32)
    f32 = jnp.float32
    nrm = lambda k, shape, s: jax.random.normal(k, shape, f32) * s
    D = D_MODEL
    dt = jnp.exp(jax.random.uniform(ks[20], (DEPTH, H_A), f32, np.log(1e-3), np.log(1e-1)))
    a_pow = jax.random.uniform(ks[21], (DEPTH, W_B), f32, 0.9, 0.999) ** (1.0 / LRU_C)
    return {
        'x_prompt': nrm(ks[0], (BATCH, SEQ, D), 1.0),
        'x_sample': nrm(ks[1], (DEC_BATCH, DEC_SEQ, D), 1.0),
        'c_prompt': nrm(ks[2], (BATCH, D), 1.0),
        'c_sample': nrm(ks[3], (DEC_BATCH, D), 1.0),
        'state_delta': nrm(ks[4], (DEPTH, DEC_BATCH, H_A, DK, DV), DK ** -0.5),
        'state_delta_conv': nrm(ks[5], (DEPTH, DEC_BATCH, CONV_W - 1, N_DCONV), 1.0),
        'state_lru': nrm(ks[6], (DEPTH, DEC_BATCH, W_B), 0.5),
        'state_lru_conv': nrm(ks[7], (DEPTH, DEC_BATCH, CONV_W - 1, W_B), 1.0),
        'w_in': nrm(ks[8], (DEPTH, D, N_IN), D ** -0.5),
        'w_out': nrm(ks[9], (DEPTH, W_A + W_B, D), (W_A + W_B) ** -0.5),
        'norm_g': 1.0 + nrm(ks[10], (DEPTH, 3, D), 0.05),
        'w_ada': nrm(ks[11], (DEPTH, D, 9 * D), 0.5 * D ** -0.5),
        'b_ada': nrm(ks[12], (DEPTH, 9 * D), 0.02),
        'ffn_w1': nrm(ks[13], (DEPTH, 2, D, D_FF), D ** -0.5),
        'ffn_w3': nrm(ks[14], (DEPTH, 2, D, D_FF), D ** -0.5),
        'ffn_w2': nrm(ks[15], (DEPTH, 2, D_FF, D), D_FF ** -0.5),
        'dconv_w': nrm(ks[16], (DEPTH, CONV_W, N_DCONV), CONV_W ** -0.5),
        'd_alog': jnp.log(jax.random.uniform(ks[17], (DEPTH, H_A), f32, 1.0, 16.0)),
        'd_dtbias': dt + jnp.log(-jnp.expm1(-dt)),
        'd_onorm': 1.0 + nrm(ks[18], (DEPTH, DV), 0.05),
        'lconv_w': nrm(ks[19], (DEPTH, CONV_W, W_B), CONV_W ** -0.5),
        'lconv_b': nrm(ks[22], (DEPTH, W_B), 0.02),
        'lru_wa': nrm(ks[23], (DEPTH, LRU_BLOCKS, LRU_BW, LRU_BW), LRU_BW ** -0.5),
        'lru_ba': nrm(ks[24], (DEPTH, W_B), 0.02),
        'lru_wx': nrm(ks[25], (DEPTH, LRU_BLOCKS, LRU_BW, LRU_BW), LRU_BW ** -0.5),
        'lru_bx': nrm(ks[26], (DEPTH, W_B), 0.02),
        'lru_lam': jnp.log(a_pow / (1.0 - a_pow)),
        'final_g': 1.0 + nrm(ks[27], (D,), 0.05),
    }


def reference(x_prompt, x_sample, c_prompt, c_sample, state_delta, state_delta_conv, state_lru, state_lru_conv,
              w_in, w_out, norm_g, w_ada, b_ada, ffn_w1, ffn_w3, ffn_w2, dconv_w, d_alog, d_dtbias, d_onorm,
              lconv_w, lconv_b, lru_wa, lru_ba, lru_wx, lru_bx, lru_lam, final_g):
    weights = (w_in, w_out, norm_g, w_ada, b_ada, ffn_w1, ffn_w3, ffn_w2, dconv_w, d_alog, d_dtbias, d_onorm,
               lconv_w, lconv_b, lru_wa, lru_ba, lru_wx, lru_bx, lru_lam, final_g)
    bp = x_prompt.shape[0]
    z_delta = jnp.zeros((DEPTH, bp, H_A, DK, DV), state_delta.dtype)
    z_dconv = jnp.zeros((DEPTH, bp, CONV_W - 1, N_DCONV), x_prompt.dtype)
    z_lru = jnp.zeros((DEPTH, bp, W_B), state_lru.dtype)
    z_lconv = jnp.zeros((DEPTH, bp, CONV_W - 1, W_B), x_prompt.dtype)
    y_prompt, p_delta, p_dconv, p_lru, p_lconv = trunk(x_prompt, c_prompt, z_delta, z_dconv, z_lru, z_lconv, *weights)
    y_sample, s_delta, s_dconv, s_lru, s_lconv = trunk(x_sample, c_sample, state_delta, state_delta_conv,
                                                       state_lru, state_lru_conv, *weights)
    return (y_prompt, y_sample, p_delta, p_dconv, p_lru, p_lconv, s_delta, s_dconv, s_lru, s_lconv)
```

```python
import functools

import jax
import jax.numpy as jnp
from jax import lax
from jax.experimental import pallas as pl
from jax.experimental.pallas import tpu as pltpu

F32 = jnp.float32
BF16 = jnp.bfloat16

D_MODEL = 2048
DEPTH = 4
H_A = 8
DK = 128
DV = 128
W_QK = H_A * DK
W_A = H_A * DV
W_B = D_MODEL - W_A
LRU_BLOCKS = 8
LRU_BW = W_B // LRU_BLOCKS
LRU_C = 8.0
CONV_W = 4
D_FF = ((8 * D_MODEL // 3 + 127) // 128) * 128
HALF = 0.5
EPS = 1e-6
N_DCONV = 2 * W_QK + W_A
OFF_G = 2 * W_QK + W_A
OFF_A = OFF_G + W_A
OFF_X = OFF_A + 2 * H_A
N_IN = OFF_X + 2 * W_B

LANES = 128
SUBLANES = 8
MXU_DIM = 256
VMEM_LIMIT = 56 * 1024 * 1024

FF_PAD = ((D_FF + 2 * MXU_DIM - 1) // (2 * MXU_DIM)) * (2 * MXU_DIM)
COL_Q, COL_K, COL_V, COL_G = 0, W_QK, 2 * W_QK, 2 * W_QK + W_A
COL_X = COL_G + W_A
COL_Y = COL_X + W_B
COL_AB = COL_Y + W_B
N_PROJ = COL_AB + MXU_DIM

TM_ROWS = 512
TF_FFN = 512
TN_PROJ = 1280
TN_ADA = 1024
CHUNK = 128
HG_PROMPT = 2
TB_SAMPLE = 16

assert FF_PAD % TF_FFN == 0 and N_PROJ % TN_PROJ == 0 and (9 * D_MODEL) % TN_ADA == 0


def _cparams(*sem):
    return pltpu.CompilerParams(dimension_semantics=sem, vmem_limit_bytes=VMEM_LIMIT)


def _dot(a, b):
    return jnp.dot(a, b, preferred_element_type=F32)


def _dot_nt(a, b):
    return lax.dot_general(a, b, (((1,), (1,)), ((), ())), preferred_element_type=F32)


def _split(a):
    hi = a.astype(BF16)
    lo = (a - hi.astype(F32)).astype(BF16)
    return hi, lo


def _dot3(a, b):
    ah, al = _split(a)
    bh, bl = _split(b)
    return _dot(jnp.concatenate([ah, ah, al], axis=1), jnp.concatenate([bh, bl, bh], axis=0))


def _silu(x):
    return x * jax.nn.sigmoid(x)


def _expm1(x):
    u = jnp.exp(x)
    um1 = u - 1.0
    safe = um1 * x / jnp.log(jnp.where(um1 == 0.0, 2.0, u))
    return jnp.where(um1 == 0.0, x, jnp.where(x < -30.0, um1, safe))


def _norm_mod(x, g, shift, scale):
    var = jnp.mean(x * x, axis=-1, keepdims=True)
    y = x * lax.rsqrt(var + EPS) * g
    return y * (1.0 + scale) + shift


def _row_iota(shape, mod):
    return jnp.bitwise_and(lax.broadcasted_iota(jnp.int32, shape, 0), mod - 1)


def _conv_rows(x, pe, w, t):
    rows = x.shape[0]
    y = x * w[CONV_W - 1:CONV_W, :]
    for j in range(1, CONV_W):
        term = pltpu.roll(x, j, axis=0)
        if pe is None:
            term = jnp.where(t >= j, term, 0.0)
        else:
            back = CONV_W - 1 - j
            prev = pe if back == 0 else pltpu.roll(pe, rows - back, axis=0)
            term = jnp.where(t >= j, term, prev)
        y = y + term * w[CONV_W - 1 - j:CONV_W - j, :]
    return y


def _ada_body(c_ref, w_ref, b_ref, o_ref):
    cs = _silu(c_ref[...]).astype(BF16)
    o_ref[...] = _dot(cs, w_ref[...].astype(BF16)) + b_ref[...]


def _ada(c_all, w_ada, b_ada):
    rows = c_all.shape[0]
    n9 = 9 * D_MODEL
    return pl.pallas_call(
        _ada_body,
        grid=(DEPTH, n9 // TN_ADA),
        in_specs=[
            pl.BlockSpec((rows, D_MODEL), lambda l, n: (0, 0)),
            pl.BlockSpec((None, D_MODEL, TN_ADA), lambda l, n: (l, 0, n)),
            pl.BlockSpec((None, 1, TN_ADA), lambda l, n: (l, 0, n)),
        ],
        out_specs=pl.BlockSpec((None, rows, TN_ADA), lambda l, n: (l, 0, n)),
        out_shape=jax.ShapeDtypeStruct((DEPTH, rows, n9), F32),
        compiler_params=_cparams("parallel", "parallel"),
        name="ada",
    )(c_all, w_ada, b_ada.reshape(DEPTH, 1, n9))


class _Mod:
    def __init__(self, arr, per_token, seq_len):
        self.arr = arr
        self.per_token = per_token
        self.seq_len = seq_len

    def spec(self, layer, tm, k):
        if self.per_token:
            return pl.BlockSpec((None, tm, D_MODEL), lambda i, *_: (layer, i, k))
        seq_len = self.seq_len
        return pl.BlockSpec((None, None, 1, D_MODEL), lambda i, *_: (layer, (i * tm) // seq_len, 0, k))


def _ffn_body(x_ref, g_ref, sh_ref, sc_ref, gt_ref, w1_ref, w3_ref, w2_ref, *rest, final):
    if final:
        fg_ref, o_ref, h_scr, acc_scr = rest
    else:
        o_ref, h_scr, acc_scr = rest
    f = pl.program_id(1)

    @pl.when(f == 0)
    def _():
        h = _norm_mod(x_ref[...], g_ref[...], sh_ref[...], sc_ref[...])
        h_scr[...] = h.astype(BF16)
        acc_scr[...] = jnp.zeros_like(acc_scr)

    h = h_scr[...]
    a = _dot(h, w1_ref[...])
    b = _dot(h, w3_ref[...])
    acc_scr[...] += _dot((_silu(a) * b).astype(BF16), w2_ref[...])

    @pl.when(f == pl.num_programs(1) - 1)
    def _():
        y = x_ref[...] + HALF * gt_ref[...] * acc_scr[...]
        if final:
            var = jnp.mean(y * y, axis=-1, keepdims=True)
            y = y * lax.rsqrt(var + EPS) * fg_ref[...]
        o_ref[...] = y


def _ffn(x, mod, norm_g3, w1, w3, w2, layer, s, final_g=None):
    m = x.shape[0]
    tm = min(TM_ROWS, m)
    sub = 2 * s
    final = final_g is not None
    in_specs = [
        pl.BlockSpec((tm, D_MODEL), lambda i, f: (i, 0)),
        pl.BlockSpec((None, 1, D_MODEL), lambda i, f: (layer * 3 + sub, 0, 0)),
        mod.spec(layer, tm, 3 * sub), mod.spec(layer, tm, 3 * sub + 1), mod.spec(layer, tm, 3 * sub + 2),
        pl.BlockSpec((None, None, D_MODEL, TF_FFN), lambda i, f: (layer, s, 0, f)),
        pl.BlockSpec((None, None, D_MODEL, TF_FFN), lambda i, f: (layer, s, 0, f)),
        pl.BlockSpec((None, None, TF_FFN, D_MODEL), lambda i, f: (layer, s, f, 0)),
    ]
    args = [x, norm_g3, mod.arr, mod.arr, mod.arr, w1, w3, w2]
    if final:
        in_specs.append(pl.BlockSpec((1, D_MODEL), lambda i, f: (0, 0)))
        args.append(final_g.reshape(1, D_MODEL))
    return pl.pallas_call(
        functools.partial(_ffn_body, final=final),
        grid=(m // tm, FF_PAD // TF_FFN),
        in_specs=in_specs,
        out_specs=pl.BlockSpec((tm, D_MODEL), lambda i, f: (i, 0)),
        out_shape=jax.ShapeDtypeStruct((m, D_MODEL), F32),
        scratch_shapes=[pltpu.VMEM((tm, D_MODEL), BF16), pltpu.VMEM((tm, D_MODEL), F32)],
        compiler_params=_cparams("parallel", "arbitrary"),
        name="ffn",
    )(*args)


def _inproj_body(x_ref, g_ref, sh_ref, sc_ref, w_ref, o_ref, h_scr):
    @pl.when(pl.program_id(1) == 0)
    def _():
        h_scr[...] = _norm_mod(x_ref[...], g_ref[...], sh_ref[...], sc_ref[...]).astype(BF16)

    o_ref[...] = _dot(h_scr[...], w_ref[...])


def _inproj(x, mod, norm_g3, w_in, layer):
    m = x.shape[0]
    tm = min(TM_ROWS, m)
    return pl.pallas_call(
        _inproj_body,
        grid=(m // tm, N_PROJ // TN_PROJ),
        in_specs=[
            pl.BlockSpec((tm, D_MODEL), lambda i, n: (i, 0)),
            pl.BlockSpec((None, 1, D_MODEL), lambda i, n: (layer * 3 + 1, 0, 0)),
            mod.spec(layer, tm, 3), mod.spec(layer, tm, 4),
            pl.BlockSpec((None, D_MODEL, TN_PROJ), lambda i, n: (layer, 0, n)),
        ],
        out_specs=pl.BlockSpec((tm, TN_PROJ), lambda i, n: (i, n)),
        out_shape=jax.ShapeDtypeStruct((m, N_PROJ), F32),
        scratch_shapes=[pltpu.VMEM((tm, D_MODEL), BF16)],
        compiler_params=_cparams("parallel", "arbitrary"),
        name="inproj",
    )(x, norm_g3, mod.arr, mod.arr, w_in)


def _outproj_body(x_ref, oa_ref, ob_ref, gt_ref, wa_ref, wb_ref, o_ref):
    acc = _dot(oa_ref[...], wa_ref[...]) + _dot(ob_ref[...], wb_ref[...])
    o_ref[...] = x_ref[...] + gt_ref[...] * acc


def _outproj(x, oa, ob, mod, w_out, layer):
    m = x.shape[0]
    tm = min(TM_ROWS, m)
    return pl.pallas_call(
        _outproj_body,
        grid=(m // tm,),
        in_specs=[
            pl.BlockSpec((tm, D_MODEL), lambda i: (i, 0)),
            pl.BlockSpec((tm, W_A), lambda i: (i, 0)),
            pl.BlockSpec((tm, W_B), lambda i: (i, 0)),
            mod.spec(layer, tm, 5),
            pl.BlockSpec((None, W_A, D_MODEL), lambda i: (layer, 0, 0)),
            pl.BlockSpec((None, W_B, D_MODEL), lambda i: (layer, 1, 0)),
        ],
        out_specs=pl.BlockSpec((tm, D_MODEL), lambda i: (i, 0)),
        out_shape=jax.ShapeDtypeStruct((m, D_MODEL), F32),
        compiler_params=_cparams("parallel"),
        name="outproj",
    )(x, oa, ob, mod.arr, w_out, w_out)


def _lru_body(*refs, seq_len, has_state):
    if has_state:
        (x_ref, y_ref, pe_ref, h0_ref, cw_ref, cb_ref, wa_ref, wx_ref, ba_ref, bx_ref, lam_ref,
         ob_ref, hl_ref, a_scr, b_scr) = refs
    else:
        (x_ref, y_ref, cw_ref, cb_ref, wa_ref, wx_ref, ba_ref, bx_ref, lam_ref,
         ob_ref, hl_ref, a_scr, b_scr) = refs
    rows = x_ref.shape[0]
    t = _row_iota(x_ref.shape, seq_len)
    xc = _conv_rows(x_ref[...], pe_ref[...] if has_state else None, cw_ref[...], t) + cb_ref[...]
    xcb = xc.astype(BF16)
    r = jax.nn.sigmoid(_dot(xcb, wa_ref[...]) + ba_ref[...])
    i = jax.nn.sigmoid(_dot(xcb, wx_ref[...]) + bx_ref[...])
    log_a = -LRU_C * r * jax.nn.softplus(-lam_ref[...])
    a = jnp.exp(log_a)
    b = jnp.sqrt(-_expm1(2.0 * log_a)) * (i * xc)
    if has_state:
        b = b + a * h0_ref[...]
    group = min(seq_len, SUBLANES)
    tg = _row_iota(x_ref.shape, group)
    d = 1
    while d < group:
        keep = tg >= d
        b = jnp.where(keep, a * pltpu.roll(b, d, axis=0) + b, b)
        a = jnp.where(keep, a * pltpu.roll(a, d, axis=0), a)
        d *= 2
    if seq_len > SUBLANES:
        a_scr[...] = a
        b_scr[...] = b

        def step(k, carry):
            r0 = pl.multiple_of(k * SUBLANES, SUBLANES)
            hk = b_scr[pl.ds(r0, SUBLANES), :] + a_scr[pl.ds(r0, SUBLANES), :] * carry
            b_scr[pl.ds(r0, SUBLANES), :] = hk
            return hk[SUBLANES - 1:SUBLANES, :]

        lax.fori_loop(0, rows // SUBLANES, step, jnp.zeros((1, LRU_BW), F32), unroll=8)
        h = b_scr[...]
    else:
        h = b
    ob_ref[...] = (h * jax.nn.gelu(y_ref[...])).astype(BF16)
    hl_ref[...] = h[rows - hl_ref.shape[0]:, :]


def _lru(proj, seq_len, layer, lconv_w, lconv_b, wa, wx, ba, bx, lam, pe=None, h0e=None):
    m = proj.shape[0]
    has_state = pe is not None
    rows = seq_len if not has_state else m
    nseq = m // rows
    hl_rows = rows if has_state else SUBLANES
    bx0, by0 = COL_X // LRU_BW, COL_Y // LRU_BW
    col = lambda b, n: (b, n)
    vec = pl.BlockSpec((None, 1, LRU_BW), lambda b, n: (layer, 0, n))
    blk = pl.BlockSpec((None, None, LRU_BW, LRU_BW), lambda b, n: (layer, n, 0, 0))
    in_specs = [pl.BlockSpec((rows, LRU_BW), lambda b, n: (b, bx0 + n)),
                pl.BlockSpec((rows, LRU_BW), lambda b, n: (b, by0 + n))]
    args = [proj, proj]
    if has_state:
        in_specs += [pl.BlockSpec((rows, LRU_BW), col), pl.BlockSpec((rows, LRU_BW), col)]
        args += [pe, h0e]
    in_specs += [pl.BlockSpec((None, CONV_W, LRU_BW), lambda b, n: (layer, 0, n)), vec, blk, blk, vec, vec, vec]
    args += [lconv_w, lconv_b, wa, wx, ba, bx, lam]
    return pl.pallas_call(
        functools.partial(_lru_body, seq_len=seq_len, has_state=has_state),
        grid=(nseq, LRU_BLOCKS),
        in_specs=in_specs,
        out_specs=[pl.BlockSpec((rows, LRU_BW), col), pl.BlockSpec((hl_rows, LRU_BW), col)],
        out_shape=[jax.ShapeDtypeStruct((m, W_B), BF16), jax.ShapeDtypeStruct((nseq * hl_rows, W_B), F32)],
        scratch_shapes=[pltpu.VMEM((rows, LRU_BW), F32), pltpu.VMEM((rows, LRU_BW), F32)],
        compiler_params=_cparams("parallel", "parallel"),
        name="lru",
    )(*args)


def _qkv_prep(q_pre, k_pre, v_pre, pe, cw_q, cw_k, cw_v, t, heads):
    pq, pk, pv = (None, None, None) if pe is None else pe
    q = _silu(_conv_rows(q_pre, pq, cw_q, t))
    k = _silu(_conv_rows(k_pre, pk, cw_k, t))
    v = _silu(_conv_rows(v_pre, pv, cw_v, t))
    qs, ks = [], []
    for h in range(heads):
        sl = slice(h * DK, (h + 1) * DK)
        qh, kh = q[:, sl], k[:, sl]
        qs.append(qh * (lax.rsqrt(jnp.sum(qh * qh, axis=-1, keepdims=True) + EPS) * DK ** -0.5))
        ks.append(kh * lax.rsqrt(jnp.sum(kh * kh, axis=-1, keepdims=True) + EPS))
    return jnp.concatenate(qs, axis=1), jnp.concatenate(ks, axis=1), v


def _gate_norm(o, onorm, gate_pre):
    var = jnp.mean(o * o, axis=-1, keepdims=True)
    return o * lax.rsqrt(var + EPS) * onorm * _silu(gate_pre)


def _seg_cumsum(g, tg, group):
    d = 1
    while d < group:
        g = jnp.where(tg >= d, g + pltpu.roll(g, d, axis=0), g)
        d *= 2
    return g


def _delta_prompt_body(q_ref, k_ref, v_ref, gt_ref, ab_ref, cwq_ref, cwk_ref, cwv_ref, alog_ref, dtb_ref,
                       on_ref, oa_ref, s_ref, qn_scr, kn_scr, vv_scr, gc_scr, bt_scr, s_scr):
    rows = q_ref.shape[0]
    hg = q_ref.shape[1] // DK
    t = lax.broadcasted_iota(jnp.int32, q_ref.shape, 0)
    qn, kn, vv = _qkv_prep(q_ref[...], k_ref[...], v_ref[...], None,
                           cwq_ref[...], cwk_ref[...], cwv_ref[...], t, hg)
    qn_scr[...] = qn
    kn_scr[...] = kn
    vv_scr[...] = vv
    ab = ab_ref[...]
    g = -jnp.exp(alog_ref[...]) * jax.nn.softplus(ab + dtb_ref[...])
    gc_scr[...] = _seg_cumsum(g, _row_iota(ab.shape, CHUNK), CHUNK)
    bt_scr[...] = jax.nn.sigmoid(ab)
    s_scr[...] = jnp.zeros_like(s_scr)
    head0 = pl.program_id(1) * hg
    onorm = on_ref[...]

    def chunk(c, carry):
        r0 = pl.multiple_of(c * CHUNK, CHUNK)
        rs = pl.ds(r0, CHUNK)
        gcc = gc_scr[rs, :]
        btc = bt_scr[rs, :]
        lane = lax.broadcasted_iota(jnp.int32, (CHUNK, LANES), 1)
        ri = lax.broadcasted_iota(jnp.int32, (CHUNK, CHUNK), 0)
        ci = lax.broadcasted_iota(jnp.int32, (CHUNK, CHUNK), 1)
        causal = ri >= ci
        strict = ri > ci
        eye = jnp.where(ri == ci, 1.0, 0.0).astype(F32)
        for hh in range(hg):
            sl = slice(hh * DK, (hh + 1) * DK)
            gcol = jnp.sum(jnp.where(lane == head0 + hh, gcc, 0.0), axis=1, keepdims=True)
            bcol = jnp.sum(jnp.where(lane == head0 + hh + H_A, btc, 0.0), axis=1, keepdims=True)
            gmat = jnp.broadcast_to(gcol, (CHUNK, CHUNK))
            decay = jnp.where(causal, jnp.exp(jnp.where(causal, gmat - gmat.T, 0.0)), 0.0)
            q = qn_scr[rs, sl]
            k = kn_scr[rs, sl]
            v = vv_scr[rs, sl]
            kq = _dot_nt(jnp.concatenate([k, q], axis=0).astype(BF16), k.astype(BF16))
            m = jnp.where(strict, bcol * kq[:CHUNK] * decay, 0.0)
            tinv = eye - jnp.where(lax.shift_right_logical(ri, 1) == lax.shift_right_logical(ci, 1), m, 0.0)
            lg = 1
            while (1 << lg) < CHUNK:
                same_big = lax.shift_right_logical(ri, lg + 1) == lax.shift_right_logical(ci, lg + 1)
                same_small = lax.shift_right_logical(ri, lg) == lax.shift_right_logical(ci, lg)
                off = jnp.where(same_big & jnp.logical_not(same_small), m, 0.0)
                tinv = tinv - _dot3(_dot3(tinv, off), tinv)
                lg += 1
            eg = jnp.exp(gcol)
            sol = _dot3(tinv, jnp.concatenate([k * (bcol * eg), v * bcol], axis=1))
            w, uc = sol[:, :DK], sol[:, DK:]
            qkd = kq[CHUNK:] * decay
            qd = q * eg
            glast = jnp.broadcast_to(gcol[CHUNK - 1:CHUNK, :], (CHUNK, 1))
            kd = k * jnp.exp(glast - gcol)
            s_old = s_scr[hh]
            p = _dot(jnp.concatenate([w, qd], axis=0).astype(BF16), s_old.astype(BF16))
            u = (uc - p[:CHUNK]).astype(BF16)
            o = p[CHUNK:] + _dot(qkd.astype(BF16), u)
            sdec = jnp.broadcast_to(eg, (CHUNK, DV))[CHUNK - 1:CHUNK, :]
            s_scr[hh] = s_old * sdec + _dot(kd.T.astype(BF16), u)
            oa_ref[rs, sl] = _gate_norm(o, onorm, gt_ref[rs, sl]).astype(BF16)
        return carry

    lax.fori_loop(0, rows // CHUNK, chunk, 0)
    s_ref[...] = s_scr[...]


def _delta_prompt(proj, batch, seq_len, layer, dconv_w, alog_row, dtb_row, onorm):
    m = proj.shape[0]
    hg = HG_PROMPT
    wcols = hg * DK
    nq, nk, nv, ng = (c // wcols for c in (COL_Q, COL_K, COL_V, COL_G))
    colspec = lambda off: pl.BlockSpec((seq_len, wcols), lambda b, j: (b, off + j))
    cwspec = lambda off: pl.BlockSpec((None, CONV_W, wcols), lambda b, j: (layer, 0, off + j))
    row = pl.BlockSpec((None, 1, LANES), lambda b, j: (layer, 0, 0))
    return pl.pallas_call(
        _delta_prompt_body,
        grid=(batch, H_A // hg),
        in_specs=[colspec(nq), colspec(nk), colspec(nv), colspec(ng),
                  pl.BlockSpec((seq_len, LANES), lambda b, j: (b, COL_AB // LANES)),
                  cwspec(nq), cwspec(nk), cwspec(nv), row, row, row],
        out_specs=[pl.BlockSpec((seq_len, wcols), lambda b, j: (b, j)),
                   pl.BlockSpec((None, hg, DK, DV), lambda b, j: (b, j, 0, 0))],
        out_shape=[jax.ShapeDtypeStruct((m, W_A), BF16), jax.ShapeDtypeStruct((batch, H_A, DK, DV), F32)],
        scratch_shapes=[pltpu.VMEM((seq_len, wcols), F32)] * 3
        + [pltpu.VMEM((seq_len, LANES), F32)] * 2 + [pltpu.VMEM((hg, DK, DV), F32)],
        compiler_params=_cparams("parallel", "parallel"),
        name="delta_prompt",
    )(proj, proj, proj, proj, proj, dconv_w, dconv_w, dconv_w, alog_row, dtb_row, onorm)


def _delta_sample_body(q_ref, k_ref, v_ref, gt_ref, ab_ref, pe_ref, cw_ref, alog_ref, dtb_ref, on_ref, s0_ref,
                       oa_ref, s_ref, qn_scr, kn_scr, vv_scr, gc_scr, bt_scr, *, seq_len):
    rows = q_ref.shape[0]
    t = _row_iota(q_ref.shape, seq_len)
    pe = pe_ref[...]
    cw = cw_ref[...]
    qn, kn, vv = _qkv_prep(q_ref[...], k_ref[...], v_ref[...],
                           (pe[:, COL_Q:COL_K], pe[:, COL_K:COL_V], pe[:, COL_V:COL_G]),
                           cw[:, COL_Q:COL_K], cw[:, COL_K:COL_V], cw[:, COL_V:COL_G], t, H_A)
    qn_scr[...] = qn
    kn_scr[...] = kn
    vv_scr[...] = vv
    ab = ab_ref[...]
    g = -jnp.exp(alog_ref[...]) * jax.nn.softplus(ab + dtb_ref[...])
    gc_scr[...] = _seg_cumsum(g, _row_iota(ab.shape, seq_len), seq_len)
    bt_scr[...] = jax.nn.sigmoid(ab)
    onorm = on_ref[...]
    per_tile = SUBLANES // seq_len

    def tile(p, carry):
        r0 = pl.multiple_of(p * SUBLANES, SUBLANES)
        rs = pl.ds(r0, SUBLANES)
        gcc = gc_scr[rs, :]
        btc = bt_scr[rs, :]
        tt = _row_iota((SUBLANES, LANES), seq_len)
        tt2 = _row_iota((SUBLANES, DK + DV), seq_len)
        ri = lax.broadcasted_iota(jnp.int32, (SUBLANES, LANES), 0)
        ri2 = _row_iota((2 * SUBLANES, LANES), SUBLANES)
        lane = lax.broadcasted_iota(jnp.int32, (LANES, LANES), 1)
        zpad = jnp.zeros((LANES - SUBLANES, DK), F32)
        for h in range(H_A):
            sl = slice(h * DK, (h + 1) * DK)
            gc = jnp.broadcast_to(gcc[:, h:h + 1], (SUBLANES, LANES))
            beta = jnp.broadcast_to(btc[:, H_A + h:H_A + h + 1], (SUBLANES, LANES))
            q = qn_scr[rs, sl]
            k = kn_scr[rs, sl]
            v = vv_scr[rs, sl]
            eg = jnp.exp(gc)
            decs, ms = [None], [None]
            for d in range(1, seq_len):
                ok = tt >= d
                dec = jnp.where(ok, jnp.exp(jnp.where(ok, gc - pltpu.roll(gc, d, axis=0), 0.0)), 0.0)
                kk = jnp.sum(k * pltpu.roll(k, d, axis=0), axis=-1, keepdims=True)
                decs.append(dec)
                md = beta * kk * dec
                ms.append(jnp.concatenate([md, md], axis=1))
            rhs = jnp.concatenate([k * (beta * eg), v * beta], axis=1)
            sol = rhs
            for step in range(1, seq_len):
                acc = ms[1] * pltpu.roll(sol, 1, axis=0)
                for d in range(2, seq_len):
                    acc = acc + ms[d] * pltpu.roll(sol, d, axis=0)
                sol = jnp.where(tt2 == step, rhs - acc, sol)
            w, uc = sol[:, :DK], sol[:, DK:]
            qd = q * eg
            lhs = jnp.concatenate([w, qd], axis=0).astype(BF16)
            s_old = [s0_ref[p * per_tile + bb, h] for bb in range(per_tile)]
            pp = _dot(lhs, s_old[0].astype(BF16))
            for bb in range(1, per_tile):
                pp = jnp.where(ri2 >= bb * seq_len, _dot(lhs, s_old[bb].astype(BF16)), pp)
            u = uc - pp[:SUBLANES]
            o = pp[SUBLANES:] + jnp.sum(q * k, axis=-1, keepdims=True) * u
            for d in range(1, seq_len):
                qk = jnp.sum(q * pltpu.roll(k, d, axis=0), axis=-1, keepdims=True)
                o = o + (qk * decs[d]) * pltpu.roll(u, d, axis=0)
            oa_ref[rs, sl] = _gate_norm(o, onorm, gt_ref[rs, sl]).astype(BF16)
            glast = jnp.broadcast_to(gc[seq_len - 1:seq_len, :], (SUBLANES, LANES))
            for bb in range(1, per_tile):
                last = bb * seq_len + seq_len - 1
                glast = jnp.where(ri >= bb * seq_len, jnp.broadcast_to(gc[last:last + 1, :], (SUBLANES, LANES)), glast)
            kd = k * jnp.exp(glast - gc)
            kdt = jnp.concatenate([kd, zpad], axis=0).T
            upad = jnp.concatenate([u, zpad], axis=0).astype(BF16)
            for bb in range(per_tile):
                last = bb * seq_len + seq_len - 1
                cols = (lane >= bb * seq_len) & (lane < (bb + 1) * seq_len)
                kd_b = jnp.where(cols, kdt, 0.0).astype(BF16)
                s_ref[p * per_tile + bb, h] = s_old[bb] * eg[last:last + 1, :] + _dot(kd_b, upad)
        return carry

    lax.fori_loop(0, rows // SUBLANES, tile, 0)


def _delta_sample(proj, seq_len, layer, pe, dconv_w, alog_row, dtb_row, onorm, s0):
    m = proj.shape[0]
    tb = TB_SAMPLE
    rows = tb * seq_len
    colspec = lambda off: pl.BlockSpec((rows, W_A), lambda i: (i, off // W_A))
    row = pl.BlockSpec((None, 1, LANES), lambda i: (layer, 0, 0))
    sspec = pl.BlockSpec((None, tb, H_A, DK, DV), lambda i: (layer, i, 0, 0, 0))
    return pl.pallas_call(
        functools.partial(_delta_sample_body, seq_len=seq_len),
        grid=(m // rows,),
        in_specs=[colspec(COL_Q), colspec(COL_K), colspec(COL_V), colspec(COL_G),
                  pl.BlockSpec((rows, LANES), lambda i: (i, COL_AB // LANES)),
                  pl.BlockSpec((rows, N_DCONV), lambda i: (i, 0)),
                  pl.BlockSpec((None, CONV_W, N_DCONV), lambda i: (layer, 0, 0)),
                  row, row, row, sspec],
        out_specs=[pl.BlockSpec((rows, W_A), lambda i: (i, 0)),
                   pl.BlockSpec((tb, H_A, DK, DV), lambda i: (i, 0, 0, 0))],
        out_shape=[jax.ShapeDtypeStruct((m, W_A), BF16),
                   jax.ShapeDtypeStruct((m // seq_len, H_A, DK, DV), F32)],
        scratch_shapes=[pltpu.VMEM((rows, W_A), F32)] * 3 + [pltpu.VMEM((rows, LANES), F32)] * 2,
        compiler_params=_cparams("parallel"),
        name="delta_sample",
    )(proj, proj, proj, proj, proj, pe, dconv_w, alog_row, dtb_row, onorm, s0)


def _pad_rows(state, seq_len):
    b, r, c = state.shape
    return jnp.pad(state, ((0, 0), (0, seq_len - r), (0, 0))).reshape(b * seq_len, c)


def _trunk(x, mod, batch, seq_len, wts, states):
    (w_in, w_out, norm_g3, w1, w3, w2, dconv_w, alog_row, dtb_row, onorm, lconv_w, lconv_b,
     wa, wx, ba, bx, lam, final_g) = wts
    new_d, new_dc, new_l, new_lc = [], [], [], []
    for layer in range(DEPTH):
        x = _ffn(x, mod, norm_g3, w1, w3, w2, layer, 0)
        proj = _inproj(x, mod, norm_g3, w_in, layer)
        proj3 = proj.reshape(batch, seq_len, N_PROJ)
        if states is None:
            oa, sd = _delta_prompt(proj, batch, seq_len, layer, dconv_w, alog_row, dtb_row, onorm)
            ob, hl = _lru(proj, seq_len, layer, lconv_w, lconv_b, wa, wx, ba, bx, lam)
            sl = hl.reshape(batch, SUBLANES, W_B)[:, SUBLANES - 1]
        else:
            s_delta, s_dconv, s_lru, s_lconv = states
            oa, sd = _delta_sample(proj, seq_len, layer, _pad_rows(s_dconv[layer], seq_len),
                                   dconv_w, alog_row, dtb_row, onorm, s_delta)
            ob, hl = _lru(proj, seq_len, layer, lconv_w, lconv_b, wa, wx, ba, bx, lam,
                          pe=_pad_rows(s_lconv[layer], seq_len),
                          h0e=_pad_rows(s_lru[layer][:, None, :], seq_len))
            sl = hl.reshape(batch, seq_len, W_B)[:, seq_len - 1]
        x = _outproj(x, oa, ob, mod, w_out, layer)
        x = _ffn(x, mod, norm_g3, w1, w3, w2, layer, 1, final_g if layer == DEPTH - 1 else None)
        new_d.append(sd)
        new_dc.append(proj3[:, seq_len - (CONV_W - 1):, COL_Q:COL_G])
        new_l.append(sl)
        new_lc.append(proj3[:, seq_len - (CONV_W - 1):, COL_X:COL_Y])
    return x, jnp.stack(new_d), jnp.stack(new_dc), jnp.stack(new_l), jnp.stack(new_lc)


def kernel(x_prompt, x_sample, c_prompt, c_sample, state_delta, state_delta_conv, state_lru, state_lru_conv, w_in, w_out, norm_g, w_ada, b_ada, ffn_w1, ffn_w3, ffn_w2, dconv_w, d_alog, d_dtbias, d_onorm, lconv_w, lconv_b, lru_wa, lru_ba, lru_wx, lru_bx, lru_lam, final_g):
    bp, lp, _ = x_prompt.shape
    bs, ls, _ = x_sample.shape
    assert lp % CHUNK == 0 and SUBLANES % ls == 0 and bp <= SUBLANES

    ffpad = FF_PAD - D_FF
    w1 = jnp.pad(ffn_w1.astype(BF16), ((0, 0), (0, 0), (0, 0), (0, ffpad)))
    w3 = jnp.pad(ffn_w3.astype(BF16), ((0, 0), (0, 0), (0, 0), (0, ffpad)))
    w2 = jnp.pad(ffn_w2.astype(BF16), ((0, 0), (0, 0), (0, ffpad), (0, 0)))
    w_in_r = jnp.concatenate(
        [w_in[..., :OFF_A], w_in[..., OFF_X:], w_in[..., OFF_A:OFF_X],
         jnp.zeros((DEPTH, D_MODEL, N_PROJ - N_IN), w_in.dtype)], axis=-1).astype(BF16)
    w_out_b = w_out.astype(BF16)
    wa = lru_wa.astype(BF16)
    wx = lru_wx.astype(BF16)
    norm_g3 = norm_g.reshape(DEPTH * 3, 1, D_MODEL)
    lane_row = lambda v: jnp.pad(v, ((0, 0), (0, LANES - v.shape[1]))).reshape(DEPTH, 1, LANES)
    vec = lambda v: v.reshape(DEPTH, 1, W_B)
    wts = (w_in_r, w_out_b, norm_g3, w1, w3, w2, dconv_w, lane_row(d_alog), lane_row(d_dtbias),
           d_onorm.reshape(DEPTH, 1, DV), lconv_w, vec(lconv_b), wa, wx, vec(lru_ba), vec(lru_bx),
           vec(lru_lam), final_g)

    c_all = jnp.concatenate([c_prompt, jnp.zeros((SUBLANES - bp, D_MODEL), F32), c_sample], axis=0)
    mod_all = _ada(c_all, w_ada, b_ada)
    mod_p = _Mod(mod_all[:, :bp].reshape(DEPTH, bp, 1, 9 * D_MODEL), False, lp)
    mod_s = _Mod(jnp.repeat(mod_all[:, SUBLANES:], ls, axis=1), True, ls)

    yp, pd, pdc, pl_, plc = _trunk(x_prompt.reshape(bp * lp, D_MODEL), mod_p, bp, lp, wts, None)
    ys, sd, sdc, sl_, slc = _trunk(x_sample.reshape(bs * ls, D_MODEL), mod_s, bs, ls, wts,
                                   (state_delta, state_delta_conv, state_lru, state_lru_conv))
    return (yp.reshape(bp, lp, D_MODEL), ys.reshape(bs, ls, D_MODEL), pd, pdc, pl_, plc, sd, sdc, sl_, slc)
```

```python
import functools

import jax
import jax.numpy as jnp
from jax import lax
from jax.experimental import pallas as pl
from jax.experimental.pallas import tpu as pltpu

F32 = jnp.float32
BF16 = jnp.bfloat16

D_MODEL = 2048
DEPTH = 4
H_A = 8
DK = 128
DV = 128
W_QK = H_A * DK
W_A = H_A * DV
W_B = D_MODEL - W_A
LRU_BLOCKS = 8
LRU_BW = W_B // LRU_BLOCKS
LRU_C = 8.0
CONV_W = 4
D_FF = ((8 * D_MODEL // 3 + 127) // 128) * 128
HALF = 0.5
EPS = 1e-6
N_DCONV = 2 * W_QK + W_A
OFF_G = 2 * W_QK + W_A
OFF_A = OFF_G + W_A
OFF_X = OFF_A + 2 * H_A
N_IN = OFF_X + 2 * W_B

LANES = 128
SUBLANES = 8
MXU_DIM = 256
VMEM_LIMIT = 56 * 1024 * 1024

COL_Q, COL_K, COL_V, COL_G = 0, W_QK, 2 * W_QK, 2 * W_QK + W_A
COL_X = COL_G + W_A
COL_Y = COL_X + W_B
COL_AB = COL_Y + W_B
N_PROJ = COL_AB + MXU_DIM

TM_ROWS = 512
TM_PROJ = 1024
TF_FFN = 512
FF_STEPS = -(-D_FF // TF_FFN)
TN_PROJ = 1280
TN_ADA = 1024
CHUNK = 128
HG_PROMPT = 2
FACTOR_CHUNKS = 4
TB_SAMPLE = 16

assert N_PROJ % TN_PROJ == 0 and (9 * D_MODEL) % TN_ADA == 0 and D_FF % LANES == 0


def _cparams(*sem):
    return pltpu.CompilerParams(dimension_semantics=sem, vmem_limit_bytes=VMEM_LIMIT)


def _dot(a, b):
    return jnp.dot(a, b, preferred_element_type=F32)


def _dot_nt(a, b):
    return lax.dot_general(a, b, (((1,), (1,)), ((), ())), preferred_element_type=F32)


def _silu(x):
    return x * jax.nn.sigmoid(x)


def _expm1(x):
    u = jnp.exp(x)
    um1 = u - 1.0
    safe = um1 * x / jnp.log(jnp.where(um1 == 0.0, 2.0, u))
    return jnp.where(um1 == 0.0, x, jnp.where(x < -30.0, um1, safe))


def _norm_mod(x, g, shift, scale):
    var = jnp.mean(x * x, axis=-1, keepdims=True)
    y = x * lax.rsqrt(var + EPS) * g
    return y * (1.0 + scale) + shift


def _row_iota(shape, mod):
    return jnp.bitwise_and(lax.broadcasted_iota(jnp.int32, shape, 0), mod - 1)


def _conv_rows(x, pe, w, t):
    rows = x.shape[0]
    y = x * w[CONV_W - 1:CONV_W, :]
    for j in range(1, CONV_W):
        term = pltpu.roll(x, j, axis=0)
        if pe is None:
            term = jnp.where(t >= j, term, 0.0)
        else:
            back = CONV_W - 1 - j
            prev = pe if back == 0 else pltpu.roll(pe, rows - back, axis=0)
            term = jnp.where(t >= j, term, prev)
        y = y + term * w[CONV_W - 1 - j:CONV_W - j, :]
    return y


def _ada_body(c_ref, w_ref, b_ref, o_ref):
    cs = _silu(c_ref[...]).astype(BF16)
    o_ref[...] = _dot(cs, w_ref[...].astype(BF16)) + b_ref[...]


def _ada(c_all, w_ada, b_ada):
    rows = c_all.shape[0]
    n9 = 9 * D_MODEL
    return pl.pallas_call(
        _ada_body,
        grid=(DEPTH, n9 // TN_ADA),
        in_specs=[
            pl.BlockSpec((rows, D_MODEL), lambda l, n: (0, 0)),
            pl.BlockSpec((None, D_MODEL, TN_ADA), lambda l, n: (l, 0, n)),
            pl.BlockSpec((None, 1, TN_ADA), lambda l, n: (l, 0, n)),
        ],
        out_specs=pl.BlockSpec((None, rows, TN_ADA), lambda l, n: (l, 0, n)),
        out_shape=jax.ShapeDtypeStruct((DEPTH, rows, n9), F32),
        compiler_params=_cparams("parallel", "parallel"),
        name="ada",
    )(c_all, w_ada, b_ada.reshape(DEPTH, 1, n9))


class _Mod:
    def __init__(self, arr, per_token, seq_len):
        self.arr = arr
        self.per_token = per_token
        self.seq_len = seq_len

    def spec(self, layer, tm, k):
        if self.per_token:
            return pl.BlockSpec((None, tm, D_MODEL), lambda i, *_: (layer, i, k))
        seq_len = self.seq_len
        return pl.BlockSpec((None, None, 1, D_MODEL), lambda i, *_: (layer, (i * tm) // seq_len, 0, k))


def _ffn_body(x_ref, g_ref, sh_ref, sc_ref, gt_ref, w1_ref, w3_ref, w2_ref, *rest, final):
    if final:
        fg_ref, o_ref, h_scr, acc_scr = rest
    else:
        o_ref, h_scr, acc_scr = rest
    f = pl.program_id(1)

    @pl.when(f == 0)
    def _():
        h = _norm_mod(x_ref[...], g_ref[...], sh_ref[...], sc_ref[...])
        h_scr[...] = h.astype(BF16)
        acc_scr[...] = jnp.zeros_like(acc_scr)

    def accumulate(width):
        h = h_scr[...]
        a = _dot(h, w1_ref[:, :width])
        b = _dot(h, w3_ref[:, :width])
        acc_scr[...] += _dot((_silu(a) * b).astype(BF16), w2_ref[:width, :])

    last = pl.num_programs(1) - 1
    tail = D_FF - (FF_STEPS - 1) * TF_FFN
    if tail == TF_FFN:
        accumulate(TF_FFN)
    else:
        pl.when(f < last)(lambda: accumulate(TF_FFN))
        pl.when(f == last)(lambda: accumulate(tail))

    @pl.when(f == last)
    def _():
        y = x_ref[...] + HALF * gt_ref[...] * acc_scr[...]
        if final:
            var = jnp.mean(y * y, axis=-1, keepdims=True)
            y = y * lax.rsqrt(var + EPS) * fg_ref[...]
        o_ref[...] = y


def _ffn(x, mod, norm_g3, w1, w3, w2, layer, s, final_g=None):
    m = x.shape[0]
    tm = min(TM_ROWS, m)
    sub = 2 * s
    final = final_g is not None
    in_specs = [
        pl.BlockSpec((tm, D_MODEL), lambda i, f: (i, 0)),
        pl.BlockSpec((None, 1, D_MODEL), lambda i, f: (layer * 3 + sub, 0, 0)),
        mod.spec(layer, tm, 3 * sub), mod.spec(layer, tm, 3 * sub + 1), mod.spec(layer, tm, 3 * sub + 2),
        pl.BlockSpec((None, None, D_MODEL, TF_FFN), lambda i, f: (layer, s, 0, f)),
        pl.BlockSpec((None, None, D_MODEL, TF_FFN), lambda i, f: (layer, s, 0, f)),
        pl.BlockSpec((None, None, TF_FFN, D_MODEL), lambda i, f: (layer, s, f, 0)),
    ]
    args = [x, norm_g3, mod.arr, mod.arr, mod.arr, w1, w3, w2]
    if final:
        in_specs.append(pl.BlockSpec((1, D_MODEL), lambda i, f: (0, 0)))
        args.append(final_g.reshape(1, D_MODEL))
    return pl.pallas_call(
        functools.partial(_ffn_body, final=final),
        grid=(m // tm, FF_STEPS),
        in_specs=in_specs,
        out_specs=pl.BlockSpec((tm, D_MODEL), lambda i, f: (i, 0)),
        out_shape=jax.ShapeDtypeStruct((m, D_MODEL), F32),
        scratch_shapes=[pltpu.VMEM((tm, D_MODEL), BF16), pltpu.VMEM((tm, D_MODEL), F32)],
        compiler_params=_cparams("parallel", "arbitrary"),
        name="ffn",
    )(*args)


def _inproj_body(x_ref, g_ref, sh_ref, sc_ref, w_ref, o_ref, h_scr):
    @pl.when(pl.program_id(1) == 0)
    def _():
        h_scr[...] = _norm_mod(x_ref[...], g_ref[...], sh_ref[...], sc_ref[...]).astype(BF16)

    o_ref[...] = _dot(h_scr[...], w_ref[...])


def _inproj(x, mod, norm_g3, w_in, layer):
    m = x.shape[0]
    tm = min(TM_PROJ, m)
    return pl.pallas_call(
        _inproj_body,
        grid=(m // tm, N_PROJ // TN_PROJ),
        in_specs=[
            pl.BlockSpec((tm, D_MODEL), lambda i, n: (i, 0)),
            pl.BlockSpec((None, 1, D_MODEL), lambda i, n: (layer * 3 + 1, 0, 0)),
            mod.spec(layer, tm, 3), mod.spec(layer, tm, 4),
            pl.BlockSpec((None, D_MODEL, TN_PROJ), lambda i, n: (layer, 0, n)),
        ],
        out_specs=pl.BlockSpec((tm, TN_PROJ), lambda i, n: (i, n)),
        out_shape=jax.ShapeDtypeStruct((m, N_PROJ), F32),
        scratch_shapes=[pltpu.VMEM((tm, D_MODEL), BF16)],
        compiler_params=_cparams("parallel", "arbitrary"),
        name="inproj",
    )(x, norm_g3, mod.arr, mod.arr, w_in)


def _outproj_body(x_ref, oa_ref, ob_ref, gt_ref, wa_ref, wb_ref, o_ref):
    acc = _dot(oa_ref[...], wa_ref[...]) + _dot(ob_ref[...], wb_ref[...])
    o_ref[...] = x_ref[...] + gt_ref[...] * acc


def _outproj(x, oa, ob, mod, w_out, layer):
    m = x.shape[0]
    tm = min(TM_ROWS, m)
    return pl.pallas_call(
        _outproj_body,
        grid=(m // tm,),
        in_specs=[
            pl.BlockSpec((tm, D_MODEL), lambda i: (i, 0)),
            pl.BlockSpec((tm, W_A), lambda i: (i, 0)),
            pl.BlockSpec((tm, W_B), lambda i: (i, 0)),
            mod.spec(layer, tm, 5),
            pl.BlockSpec((None, W_A, D_MODEL), lambda i: (layer, 0, 0)),
            pl.BlockSpec((None, W_B, D_MODEL), lambda i: (layer, 1, 0)),
        ],
        out_specs=pl.BlockSpec((tm, D_MODEL), lambda i: (i, 0)),
        out_shape=jax.ShapeDtypeStruct((m, D_MODEL), F32),
        compiler_params=_cparams("parallel"),
        name="outproj",
    )(x, oa, ob, mod.arr, w_out, w_out)


def _lru_body(*refs, seq_len, has_state):
    if has_state:
        (x_ref, y_ref, pe_ref, h0_ref, cw_ref, cb_ref, wa_ref, wx_ref, ba_ref, bx_ref, lam_ref,
         ob_ref, hl_ref, a_scr, b_scr) = refs
    else:
        (x_ref, y_ref, cw_ref, cb_ref, wa_ref, wx_ref, ba_ref, bx_ref, lam_ref,
         ob_ref, hl_ref, a_scr, b_scr) = refs
    rows = x_ref.shape[0]
    t = _row_iota(x_ref.shape, seq_len)
    xc = _conv_rows(x_ref[...], pe_ref[...] if has_state else None, cw_ref[...], t) + cb_ref[...]
    xcb = xc.astype(BF16)
    r = jax.nn.sigmoid(_dot(xcb, wa_ref[...]) + ba_ref[...])
    i = jax.nn.sigmoid(_dot(xcb, wx_ref[...]) + bx_ref[...])
    log_a = -LRU_C * r * jax.nn.softplus(-lam_ref[...])
    a = jnp.exp(log_a)
    b = jnp.sqrt(-_expm1(2.0 * log_a)) * (i * xc)
    if has_state:
        b = b + a * h0_ref[...]
    group = min(seq_len, SUBLANES)
    tg = _row_iota(x_ref.shape, group)
    d = 1
    while d < group:
        keep = tg >= d
        b = jnp.where(keep, a * pltpu.roll(b, d, axis=0) + b, b)
        a = jnp.where(keep, a * pltpu.roll(a, d, axis=0), a)
        d *= 2
    if seq_len > SUBLANES:
        a_scr[...] = a
        b_scr[...] = b

        def step(k, carry):
            r0 = pl.multiple_of(k * SUBLANES, SUBLANES)
            hk = b_scr[pl.ds(r0, SUBLANES), :] + a_scr[pl.ds(r0, SUBLANES), :] * carry
            b_scr[pl.ds(r0, SUBLANES), :] = hk
            return hk[SUBLANES - 1:SUBLANES, :]

        lax.fori_loop(0, rows // SUBLANES, step, jnp.zeros((1, LRU_BW), F32), unroll=8)
        h = b_scr[...]
    else:
        h = b
    ob_ref[...] = (h * jax.nn.gelu(y_ref[...])).astype(BF16)
    hl_ref[...] = h[rows - hl_ref.shape[0]:, :]


def _lru(proj, seq_len, layer, lconv_w, lconv_b, wa, wx, ba, bx, lam, pe=None, h0e=None):
    m = proj.shape[0]
    has_state = pe is not None
    rows = seq_len if not has_state else m
    nseq = m // rows
    hl_rows = rows if has_state else SUBLANES
    bx0, by0 = COL_X // LRU_BW, COL_Y // LRU_BW
    col = lambda b, n: (b, n)
    vec = pl.BlockSpec((None, 1, LRU_BW), lambda b, n: (layer, 0, n))
    blk = pl.BlockSpec((None, None, LRU_BW, LRU_BW), lambda b, n: (layer, n, 0, 0))
    in_specs = [pl.BlockSpec((rows, LRU_BW), lambda b, n: (b, bx0 + n)),
                pl.BlockSpec((rows, LRU_BW), lambda b, n: (b, by0 + n))]
    args = [proj, proj]
    if has_state:
        in_specs += [pl.BlockSpec((rows, LRU_BW), col), pl.BlockSpec((rows, LRU_BW), col)]
        args += [pe, h0e]
    in_specs += [pl.BlockSpec((None, CONV_W, LRU_BW), lambda b, n: (layer, 0, n)), vec, blk, blk, vec, vec, vec]
    args += [lconv_w, lconv_b, wa, wx, ba, bx, lam]
    return pl.pallas_call(
        functools.partial(_lru_body, seq_len=seq_len, has_state=has_state),
        grid=(nseq, LRU_BLOCKS),
        in_specs=in_specs,
        out_specs=[pl.BlockSpec((rows, LRU_BW), col), pl.BlockSpec((hl_rows, LRU_BW), col)],
        out_shape=[jax.ShapeDtypeStruct((m, W_B), BF16), jax.ShapeDtypeStruct((nseq * hl_rows, W_B), F32)],
        scratch_shapes=[pltpu.VMEM((rows, LRU_BW), F32), pltpu.VMEM((rows, LRU_BW), F32)],
        compiler_params=_cparams("parallel", "parallel"),
        name="lru",
    )(*args)


def _qkv_prep(q_pre, k_pre, v_pre, pe, cw_q, cw_k, cw_v, t, heads):
    pq, pk, pv = (None, None, None) if pe is None else pe
    q = _silu(_conv_rows(q_pre, pq, cw_q, t))
    k = _silu(_conv_rows(k_pre, pk, cw_k, t))
    v = _silu(_conv_rows(v_pre, pv, cw_v, t))
    qs, ks = [], []
    for h in range(heads):
        sl = slice(h * DK, (h + 1) * DK)
        qh, kh = q[:, sl], k[:, sl]
        qs.append(qh * (lax.rsqrt(jnp.sum(qh * qh, axis=-1, keepdims=True) + EPS) * DK ** -0.5))
        ks.append(kh * lax.rsqrt(jnp.sum(kh * kh, axis=-1, keepdims=True) + EPS))
    return jnp.concatenate(qs, axis=1), jnp.concatenate(ks, axis=1), v


def _gate_norm(o, onorm, gate_pre):
    var = jnp.mean(o * o, axis=-1, keepdims=True)
    return o * lax.rsqrt(var + EPS) * onorm * _silu(gate_pre)


def _seg_cumsum(g, tg, group):
    d = 1
    while d < group:
        g = jnp.where(tg >= d, g + pltpu.roll(g, d, axis=0), g)
        d *= 2
    return g


def _delta_prompt_body(q_ref, k_ref, v_ref, gt_ref, ab_ref, cwq_ref, cwk_ref, cwv_ref, alog_ref, dtb_ref,
                       on_ref, oa_ref, s_ref, qn_scr, kn_scr, vv_scr, gc_scr, bt_scr, s_scr,
                       wq_scr, uc_scr, qk_scr, kdt_scr, sd_scr):
    rows = q_ref.shape[0]
    hg = q_ref.shape[1] // DK
    t = lax.broadcasted_iota(jnp.int32, q_ref.shape, 0)
    qn, kn, vv = _qkv_prep(q_ref[...], k_ref[...], v_ref[...], None,
                           cwq_ref[...], cwk_ref[...], cwv_ref[...], t, hg)
    qn_scr[...] = qn
    kn_scr[...] = kn
    vv_scr[...] = vv
    ab = ab_ref[...]
    g = -jnp.exp(alog_ref[...]) * jax.nn.softplus(ab + dtb_ref[...])
    gc_scr[...] = _seg_cumsum(g, _row_iota(ab.shape, CHUNK), CHUNK)
    bt_scr[...] = jax.nn.sigmoid(ab)
    s_scr[...] = jnp.zeros_like(s_scr)
    head0 = pl.program_id(1) * hg
    onorm = on_ref[...]

    def factor(step, carry):
        lane = lax.broadcasted_iota(jnp.int32, (CHUNK, LANES), 1)
        ri = lax.broadcasted_iota(jnp.int32, (CHUNK, CHUNK), 0)
        ci = lax.broadcasted_iota(jnp.int32, (CHUNK, CHUNK), 1)
        causal = ri >= ci
        strict = ri > ci
        blk = jnp.bitwise_xor(ri, ci)
        chains = []
        for cc in range(FACTOR_CHUNKS):
            c = step * FACTOR_CHUNKS + cc
            rs = pl.ds(pl.multiple_of(c * CHUNK, CHUNK), CHUNK)
            gcc = gc_scr[rs, :]
            btc = bt_scr[rs, :]
            for hh in range(hg):
                sl = slice(hh * DK, (hh + 1) * DK)
                gcol = jnp.sum(jnp.where(lane == head0 + hh, gcc, 0.0), axis=1, keepdims=True)
                bcol = jnp.sum(jnp.where(lane == head0 + hh + H_A, btc, 0.0), axis=1, keepdims=True)
                gmat = jnp.broadcast_to(gcol, (CHUNK, CHUNK))
                decay = jnp.where(causal, jnp.exp(jnp.where(causal, gmat - gmat.T, 0.0)), 0.0)
                q = qn_scr[rs, sl]
                k = kn_scr[rs, sl]
                kq = _dot_nt(jnp.concatenate([k, q], axis=0).astype(BF16), k.astype(BF16))
                m = jnp.where(strict, bcol * kq[:CHUNK] * decay, 0.0)
                eg = jnp.exp(gcol)
                glast = jnp.broadcast_to(gcol[CHUNK - 1:CHUNK, :], (CHUNK, 1))
                wq_scr[hh, c, CHUNK:, :] = (q * eg).astype(BF16)
                qk_scr[hh, rs, :] = (kq[CHUNK:] * decay).astype(BF16)
                kdt_scr[hh, rs, :] = (k * jnp.exp(glast - gcol)).T.astype(BF16)
                sd_scr[hh, pl.ds(pl.multiple_of(c * SUBLANES, SUBLANES), SUBLANES), :] = jnp.broadcast_to(
                    jnp.broadcast_to(eg, (CHUNK, DV))[CHUNK - 1:CHUNK, :], (SUBLANES, DV))
                chains.append((hh, c, rs, sl, m, bcol, eg))
        es = [-jnp.where(lax.shift_right_logical(blk, 1) == 0, ch[4], 0.0) for ch in chains]
        lg = 1
        while (1 << lg) < CHUNK:
            cross = lax.shift_right_logical(blk, lg) == 1
            offs = [jnp.where(cross, ch[4], 0.0) for ch in chains]
            ebs = [e.astype(BF16) for e in es]
            xs = [off + _dot(eb, off.astype(BF16)) for off, eb in zip(offs, ebs)]
            es = [e - (x + _dot(x.astype(BF16), eb)) for e, x, eb in zip(es, xs, ebs)]
            lg += 1
        for (hh, c, rs, sl, m, bcol, eg), e in zip(chains, es):
            rhs = jnp.concatenate([kn_scr[rs, sl] * (bcol * eg), vv_scr[rs, sl] * bcol], axis=1)
            sol = rhs + _dot(e.astype(BF16), rhs.astype(BF16))
            wq_scr[hh, c, :CHUNK, :] = sol[:, :DK].astype(BF16)
            uc_scr[hh, rs, :] = sol[:, DK:]
        return carry

    lax.fori_loop(0, rows // (CHUNK * FACTOR_CHUNKS), factor, 0)

    def recur(c, carry):
        rs = pl.ds(pl.multiple_of(c * CHUNK, CHUNK), CHUNK)
        s_old = [s_scr[hh] for hh in range(hg)]
        ps = [_dot(wq_scr[hh, c], s_old[hh].astype(BF16)) for hh in range(hg)]
        us = [(uc_scr[hh, rs, :] - ps[hh][:CHUNK]).astype(BF16) for hh in range(hg)]
        os_ = [ps[hh][CHUNK:] + _dot(qk_scr[hh, rs, :], us[hh]) for hh in range(hg)]
        for hh in range(hg):
            sdec = sd_scr[hh, pl.ds(pl.multiple_of(c * SUBLANES, SUBLANES), SUBLANES), :]
            s_scr[hh] = s_old[hh] * sdec[:1, :] + _dot(kdt_scr[hh, rs, :], us[hh])
        for hh in range(hg):
            sl = slice(hh * DK, (hh + 1) * DK)
            oa_ref[rs, sl] = _gate_norm(os_[hh], onorm, gt_ref[rs, sl]).astype(BF16)
        return carry

    lax.fori_loop(0, rows // CHUNK, recur, 0)
    s_ref[...] = s_scr[...]


def _delta_prompt(proj, batch, seq_len, layer, dconv_w, alog_row, dtb_row, onorm):
    m = proj.shape[0]
    hg = HG_PROMPT
    wcols = hg * DK
    nchunks = seq_len // CHUNK
    nq, nk, nv, ng = (c // wcols for c in (COL_Q, COL_K, COL_V, COL_G))
    colspec = lambda off: pl.BlockSpec((seq_len, wcols), lambda b, j: (b, off + j))
    cwspec = lambda off: pl.BlockSpec((None, CONV_W, wcols), lambda b, j: (layer, 0, off + j))
    row = pl.BlockSpec((None, 1, LANES), lambda b, j: (layer, 0, 0))
    return pl.pallas_call(
        _delta_prompt_body,
        grid=(batch, H_A // hg),
        in_specs=[colspec(nq), colspec(nk), colspec(nv), colspec(ng),
                  pl.BlockSpec((seq_len, LANES), lambda b, j: (b, COL_AB // LANES)),
                  cwspec(nq), cwspec(nk), cwspec(nv), row, row, row],
        out_specs=[pl.BlockSpec((seq_len, wcols), lambda b, j: (b, j)),
                   pl.BlockSpec((None, hg, DK, DV), lambda b, j: (b, j, 0, 0))],
        out_shape=[jax.ShapeDtypeStruct((m, W_A), BF16), jax.ShapeDtypeStruct((batch, H_A, DK, DV), F32)],
        scratch_shapes=[pltpu.VMEM((seq_len, wcols), F32)] * 3
        + [pltpu.VMEM((seq_len, LANES), F32)] * 2 + [pltpu.VMEM((hg, DK, DV), F32)]
        + [pltpu.VMEM((hg, nchunks, 2 * CHUNK, DK), BF16), pltpu.VMEM((hg, seq_len, DV), F32),
           pltpu.VMEM((hg, seq_len, CHUNK), BF16), pltpu.VMEM((hg, seq_len, CHUNK), BF16),
           pltpu.VMEM((hg, nchunks * SUBLANES, DV), F32)],
        compiler_params=_cparams("parallel", "parallel"),
        name="delta_prompt",
    )(proj, proj, proj, proj, proj, dconv_w, dconv_w, dconv_w, alog_row, dtb_row, onorm)


def _delta_sample_body(q_ref, k_ref, v_ref, gt_ref, ab_ref, pe_ref, cw_ref, alog_ref, dtb_ref, on_ref, s0_ref,
                       *rest, seq_len):
    oa_ref, s_ref, qn_scr, kn_scr, vv_scr, gc_scr, bt_scr = rest[-7:]
    rows = q_ref.shape[0]
    t = _row_iota(q_ref.shape, seq_len)
    pe = pe_ref[...]
    cw = cw_ref[...]
    qn, kn, vv = _qkv_prep(q_ref[...], k_ref[...], v_ref[...],
                           (pe[:, COL_Q:COL_K], pe[:, COL_K:COL_V], pe[:, COL_V:COL_G]),
                           cw[:, COL_Q:COL_K], cw[:, COL_K:COL_V], cw[:, COL_V:COL_G], t, H_A)
    qn_scr[...] = qn
    kn_scr[...] = kn
    vv_scr[...] = vv
    ab = ab_ref[...]
    g = -jnp.exp(alog_ref[...]) * jax.nn.softplus(ab + dtb_ref[...])
    gc_scr[...] = _seg_cumsum(g, _row_iota(ab.shape, seq_len), seq_len)
    bt_scr[...] = jax.nn.sigmoid(ab)
    onorm = on_ref[...]
    per_tile = SUBLANES // seq_len

    def tile(p, carry):
        r0 = pl.multiple_of(p * SUBLANES, SUBLANES)
        rs = pl.ds(r0, SUBLANES)
        gcc = gc_scr[rs, :]
        btc = bt_scr[rs, :]
        tt = _row_iota((SUBLANES, LANES), seq_len)
        tt2 = _row_iota((SUBLANES, DK + DV), seq_len)
        ri = lax.broadcasted_iota(jnp.int32, (SUBLANES, LANES), 0)
        ri2 = _row_iota((2 * SUBLANES, LANES), SUBLANES)
        lane = lax.broadcasted_iota(jnp.int32, (LANES, LANES), 1)
        zpad = jnp.zeros((LANES - SUBLANES, DK), F32)
        for h in range(H_A):
            sl = slice(h * DK, (h + 1) * DK)
            gc = jnp.broadcast_to(gcc[:, h:h + 1], (SUBLANES, LANES))
            beta = jnp.broadcast_to(btc[:, H_A + h:H_A + h + 1], (SUBLANES, LANES))
            q = qn_scr[rs, sl]
            k = kn_scr[rs, sl]
            v = vv_scr[rs, sl]
            eg = jnp.exp(gc)
            decs, ms = [None], [None]
            for d in range(1, seq_len):
                ok = tt >= d
                dec = jnp.where(ok, jnp.exp(jnp.where(ok, gc - pltpu.roll(gc, d, axis=0), 0.0)), 0.0)
                kk = jnp.sum(k * pltpu.roll(k, d, axis=0), axis=-1, keepdims=True)
                decs.append(dec)
                md = beta * kk * dec
                ms.append(jnp.concatenate([md, md], axis=1))
            rhs = jnp.concatenate([k * (beta * eg), v * beta], axis=1)
            sol = rhs
            for step in range(1, seq_len):
                acc = ms[1] * pltpu.roll(sol, 1, axis=0)
                for d in range(2, seq_len):
                    acc = acc + ms[d] * pltpu.roll(sol, d, axis=0)
                sol = jnp.where(tt2 == step, rhs - acc, sol)
            w, uc = sol[:, :DK], sol[:, DK:]
            qd = q * eg
            lhs = jnp.concatenate([w, qd], axis=0).astype(BF16)
            s_old = [s0_ref[p * per_tile + bb, h] for bb in range(per_tile)]
            pp = _dot(lhs, s_old[0].astype(BF16))
            for bb in range(1, per_tile):
                pp = jnp.where(ri2 >= bb * seq_len, _dot(lhs, s_old[bb].astype(BF16)), pp)
            u = uc - pp[:SUBLANES]
            o = pp[SUBLANES:] + jnp.sum(q * k, axis=-1, keepdims=True) * u
            for d in range(1, seq_len):
                qk = jnp.sum(q * pltpu.roll(k, d, axis=0), axis=-1, keepdims=True)
                o = o + (qk * decs[d]) * pltpu.roll(u, d, axis=0)
            oa_ref[rs, sl] = _gate_norm(o, onorm, gt_ref[rs, sl]).astype(BF16)
            glast = jnp.broadcast_to(gc[seq_len - 1:seq_len, :], (SUBLANES, LANES))
            for bb in range(1, per_tile):
                last = bb * seq_len + seq_len - 1
                glast = jnp.where(ri >= bb * seq_len, jnp.broadcast_to(gc[last:last + 1, :], (SUBLANES, LANES)), glast)
            kd = k * jnp.exp(glast - gc)
            kdt = jnp.concatenate([kd, zpad], axis=0).T
            upad = jnp.concatenate([u, zpad], axis=0).astype(BF16)
            for bb in range(per_tile):
                last = bb * seq_len + seq_len - 1
                cols = (lane >= bb * seq_len) & (lane < (bb + 1) * seq_len)
                kd_b = jnp.where(cols, kdt, 0.0).astype(BF16)
                s_ref[p * per_tile + bb, h] = s_old[bb] * eg[last:last + 1, :] + _dot(kd_b, upad)
        return carry

    lax.fori_loop(0, rows // SUBLANES, tile, 0)


def _delta_sample(proj, seq_len, layer, pe, dconv_w, alog_row, dtb_row, onorm, s0, s_stack):
    m = proj.shape[0]
    tb = TB_SAMPLE
    rows = tb * seq_len
    colspec = lambda off: pl.BlockSpec((rows, W_A), lambda i: (i, off // W_A))
    row = pl.BlockSpec((None, 1, LANES), lambda i: (layer, 0, 0))
    sspec = pl.BlockSpec((None, tb, H_A, DK, DV), lambda i: (layer, i, 0, 0, 0))
    in_specs = [colspec(COL_Q), colspec(COL_K), colspec(COL_V), colspec(COL_G),
                pl.BlockSpec((rows, LANES), lambda i: (i, COL_AB // LANES)),
                pl.BlockSpec((rows, N_DCONV), lambda i: (i, 0)),
                pl.BlockSpec((None, CONV_W, N_DCONV), lambda i: (layer, 0, 0)),
                row, row, row, sspec]
    args = [proj, proj, proj, proj, proj, pe, dconv_w, alog_row, dtb_row, onorm, s0]
    aliases = {}
    if s_stack is not None:
        in_specs.append(pl.BlockSpec(memory_space=pl.ANY))
        aliases = {len(args): 1}
        args.append(s_stack)
    return pl.pallas_call(
        functools.partial(_delta_sample_body, seq_len=seq_len),
        grid=(m // rows,),
        in_specs=in_specs,
        out_specs=[pl.BlockSpec((rows, W_A), lambda i: (i, 0)), sspec],
        out_shape=[jax.ShapeDtypeStruct((m, W_A), BF16), jax.ShapeDtypeStruct(s0.shape, F32)],
        input_output_aliases=aliases,
        scratch_shapes=[pltpu.VMEM((rows, W_A), F32)] * 3 + [pltpu.VMEM((rows, LANES), F32)] * 2,
        compiler_params=_cparams("parallel"),
        name="delta_sample",
    )(*args)


def _pad_rows(state, seq_len):
    b, r, c = state.shape
    return jnp.pad(state, ((0, 0), (0, seq_len - r), (0, 0))).reshape(b * seq_len, c)


def _trunk(x, mod, batch, seq_len, wts, states):
    (w_in, w_out, norm_g3, w1, w3, w2, dconv_w, alog_row, dtb_row, onorm, lconv_w, lconv_b,
     wa, wx, ba, bx, lam, final_g) = wts
    new_d, new_dc, new_l, new_lc = [], [], [], []
    s_stack = None
    for layer in range(DEPTH):
        x = _ffn(x, mod, norm_g3, w1, w3, w2, layer, 0)
        proj = _inproj(x, mod, norm_g3, w_in, layer)
        proj3 = proj.reshape(batch, seq_len, N_PROJ)
        if states is None:
            oa, sd = _delta_prompt(proj, batch, seq_len, layer, dconv_w, alog_row, dtb_row, onorm)
            new_d.append(sd)
            ob, hl = _lru(proj, seq_len, layer, lconv_w, lconv_b, wa, wx, ba, bx, lam)
            sl = hl.reshape(batch, SUBLANES, W_B)[:, SUBLANES - 1]
        else:
            s_delta, s_dconv, s_lru, s_lconv = states
            oa, s_stack = _delta_sample(proj, seq_len, layer, _pad_rows(s_dconv[layer], seq_len),
                                        dconv_w, alog_row, dtb_row, onorm, s_delta, s_stack)
            ob, hl = _lru(proj, seq_len, layer, lconv_w, lconv_b, wa, wx, ba, bx, lam,
                          pe=_pad_rows(s_lconv[layer], seq_len),
                          h0e=_pad_rows(s_lru[layer][:, None, :], seq_len))
            sl = hl.reshape(batch, seq_len, W_B)[:, seq_len - 1]
        x = _outproj(x, oa, ob, mod, w_out, layer)
        x = _ffn(x, mod, norm_g3, w1, w3, w2, layer, 1, final_g if layer == DEPTH - 1 else None)
        new_dc.append(proj3[:, seq_len - (CONV_W - 1):, COL_Q:COL_G])
        new_l.append(sl)
        new_lc.append(proj3[:, seq_len - (CONV_W - 1):, COL_X:COL_Y])
    new_delta = jnp.stack(new_d) if states is None else s_stack
    return x, new_delta, jnp.stack(new_dc), jnp.stack(new_l), jnp.stack(new_lc)


def kernel(x_prompt, x_sample, c_prompt, c_sample, state_delta, state_delta_conv, state_lru, state_lru_conv, w_in, w_out, norm_g, w_ada, b_ada, ffn_w1, ffn_w3, ffn_w2, dconv_w, d_alog, d_dtbias, d_onorm, lconv_w, lconv_b, lru_wa, lru_ba, lru_wx, lru_bx, lru_lam, final_g):
    bp, lp, _ = x_prompt.shape
    bs, ls, _ = x_sample.shape
    assert lp % CHUNK == 0 and SUBLANES % ls == 0 and bp <= SUBLANES

    w1 = ffn_w1.astype(BF16)
    w3 = ffn_w3.astype(BF16)
    w2 = ffn_w2.astype(BF16)
    w_in_r = jnp.concatenate(
        [w_in[..., :OFF_A], w_in[..., OFF_X:], w_in[..., OFF_A:OFF_X],
         jnp.zeros((DEPTH, D_MODEL, N_PROJ - N_IN), w_in.dtype)], axis=-1).astype(BF16)
    w_out_b = w_out.astype(BF16)
    wa = lru_wa.astype(BF16)
    wx = lru_wx.astype(BF16)
    norm_g3 = norm_g.reshape(DEPTH * 3, 1, D_MODEL)
    lane_row = lambda v: jnp.pad(v, ((0, 0), (0, LANES - v.shape[1]))).reshape(DEPTH, 1, LANES)
    vec = lambda v: v.reshape(DEPTH, 1, W_B)
    wts = (w_in_r, w_out_b, norm_g3, w1, w3, w2, dconv_w, lane_row(d_alog), lane_row(d_dtbias),
           d_onorm.reshape(DEPTH, 1, DV), lconv_w, vec(lconv_b), wa, wx, vec(lru_ba), vec(lru_bx),
           vec(lru_lam), final_g)

    ms = bs * ls
    c_all = jnp.concatenate([jnp.repeat(c_sample, ls, axis=0), c_prompt,
                             jnp.zeros((SUBLANES - bp, D_MODEL), F32)], axis=0)
    mod_all = _ada(c_all, w_ada, b_ada)
    mod_p = _Mod(mod_all[:, ms:ms + bp].reshape(DEPTH, bp, 1, 9 * D_MODEL), False, lp)
    mod_s = _Mod(mod_all, True, ls)

    yp, pd, pdc, pl_, plc = _trunk(x_prompt.reshape(bp * lp, D_MODEL), mod_p, bp, lp, wts, None)
    ys, sd, sdc, sl_, slc = _trunk(x_sample.reshape(bs * ls, D_MODEL), mod_s, bs, ls, wts,
                                   (state_delta, state_delta_conv, state_lru, state_lru_conv))
    return (yp.reshape(bp, lp, D_MODEL), ys.reshape(bs, ls, D_MODEL), pd, pdc, pl_, plc, sd, sdc, sl_, slc)
```

```python
import functools

import jax
import jax.numpy as jnp
from jax import lax
from jax.experimental import pallas as pl
from jax.experimental.pallas import tpu as pltpu

F32 = jnp.float32
BF16 = jnp.bfloat16

D_MODEL = 2048
DEPTH = 4
H_A = 8
DK = 128
DV = 128
W_QK = H_A * DK
W_A = H_A * DV
W_B = D_MODEL - W_A
LRU_BLOCKS = 8
LRU_BW = W_B // LRU_BLOCKS
LRU_C = 8.0
CONV_W = 4
D_FF = ((8 * D_MODEL // 3 + 127) // 128) * 128
HALF = 0.5
EPS = 1e-6
N_DCONV = 2 * W_QK + W_A
OFF_G = 2 * W_QK + W_A
OFF_A = OFF_G + W_A
OFF_X = OFF_A + 2 * H_A
N_IN = OFF_X + 2 * W_B

LANES = 128
SUBLANES = 8
MXU_DIM = 256
VMEM_LIMIT = 56 * 1024 * 1024

COL_Q, COL_K, COL_V, COL_G = 0, W_QK, 2 * W_QK, 2 * W_QK + W_A
COL_X = COL_G + W_A
COL_Y = COL_X + W_B
COL_AB = COL_Y + W_B
N_PROJ = COL_AB + MXU_DIM

TM_ROWS = 512
TM_PROJ = 1024
TF_FFN = 1024
TF_EMIT = 256
TN_PROJ = 1280
TN_ADA = 1024
CHUNK = 128
HG_PROMPT = 2
FACTOR_CHUNKS = 4
TB_SAMPLE = 16

assert N_PROJ % TN_PROJ == 0 and (9 * D_MODEL) % TN_ADA == 0 and D_FF % LANES == 0


def _cparams(*sem):
    return pltpu.CompilerParams(dimension_semantics=sem, vmem_limit_bytes=VMEM_LIMIT)


def _dot(a, b):
    return jnp.dot(a, b, preferred_element_type=F32)


def _dot_nt(a, b):
    return lax.dot_general(a, b, (((1,), (1,)), ((), ())), preferred_element_type=F32)


def _silu(x):
    return x * jax.nn.sigmoid(x)


def _expm1(x):
    u = jnp.exp(x)
    um1 = u - 1.0
    safe = um1 * x / jnp.log(jnp.where(um1 == 0.0, 2.0, u))
    return jnp.where(um1 == 0.0, x, jnp.where(x < -30.0, um1, safe))


def _norm_mod(x, g, shift, scale):
    var = jnp.mean(x * x, axis=-1, keepdims=True)
    y = x * lax.rsqrt(var + EPS) * g
    return y * (1.0 + scale) + shift


def _row_iota(shape, mod):
    return jnp.bitwise_and(lax.broadcasted_iota(jnp.int32, shape, 0), mod - 1)


def _conv_zero_past(x_ref, w):
    rows = x_ref.shape[0]
    head = x_ref[:SUBLANES, :]
    head_y = _conv_rows(head, None, w, lax.broadcasted_iota(jnp.int32, head.shape, 0))
    y = x_ref[SUBLANES:, :] * w[CONV_W - 1:CONV_W, :]
    for j in range(1, CONV_W):
        y = y + x_ref[SUBLANES - j:rows - j, :] * w[CONV_W - 1 - j:CONV_W - j, :]
    return jnp.concatenate([head_y, y], axis=0)


def _conv_rows(x, pe, w, t):
    rows = x.shape[0]
    y = x * w[CONV_W - 1:CONV_W, :]
    for j in range(1, CONV_W):
        term = pltpu.roll(x, j, axis=0)
        if pe is None:
            term = jnp.where(t >= j, term, 0.0)
        else:
            back = CONV_W - 1 - j
            prev = pe if back == 0 else pltpu.roll(pe, rows - back, axis=0)
            term = jnp.where(t >= j, term, prev)
        y = y + term * w[CONV_W - 1 - j:CONV_W - j, :]
    return y


def _ada_body(c_ref, w_ref, b_ref, o_ref):
    cs = _silu(c_ref[...]).astype(BF16)
    o_ref[...] = _dot(cs, w_ref[...].astype(BF16)) + b_ref[...]


def _ada(c_all, w_ada, b_ada):
    rows = c_all.shape[0]
    n9 = 9 * D_MODEL
    return pl.pallas_call(
        _ada_body,
        grid=(DEPTH, n9 // TN_ADA),
        in_specs=[
            pl.BlockSpec((rows, D_MODEL), lambda l, n: (0, 0)),
            pl.BlockSpec((None, D_MODEL, TN_ADA), lambda l, n: (l, 0, n)),
            pl.BlockSpec((None, 1, TN_ADA), lambda l, n: (l, 0, n)),
        ],
        out_specs=pl.BlockSpec((None, rows, TN_ADA), lambda l, n: (l, 0, n)),
        out_shape=jax.ShapeDtypeStruct((DEPTH, rows, n9), F32),
        compiler_params=_cparams("parallel", "parallel"),
        name="ada",
    )(c_all, w_ada, b_ada.reshape(DEPTH, 1, n9))


class _Mod:
    def __init__(self, arr, per_token, seq_len):
        self.arr = arr
        self.per_token = per_token
        self.seq_len = seq_len

    def spec(self, layer, tm, k, **kw):
        if self.per_token:
            return pl.BlockSpec((None, tm, D_MODEL), lambda i, *_: (layer, i, k), **kw)
        seq_len = self.seq_len
        return pl.BlockSpec((None, None, 1, D_MODEL), lambda i, *_: (layer, (i * tm) // seq_len, 0, k), **kw)


def _ffn_body(x_ref, g_ref, sh_ref, sc_ref, gt_ref, w1_ref, w3_ref, w2_ref, *rest, final, emit, tf):
    rest = list(rest)
    fg_ref = rest.pop(0) if final else None
    o_ref = rest.pop(0)
    if emit:
        w1b_ref, w3b_ref, w2b_ref = rest[:3]
        rest = rest[3:]
        w1b_ref[...] = w1_ref[...].astype(BF16)
        w3b_ref[...] = w3_ref[...].astype(BF16)
        w2b_ref[...] = w2_ref[...].astype(BF16)
        w1_ref, w3_ref, w2_ref = w1b_ref, w3b_ref, w2b_ref
    h_scr, acc_scr = rest
    f = pl.program_id(1)

    @pl.when(f == 0)
    def _():
        h = _norm_mod(x_ref[...], g_ref[...], sh_ref[...], sc_ref[...])
        h_scr[...] = h.astype(BF16)
        acc_scr[...] = jnp.zeros_like(acc_scr)

    def accumulate(width):
        h = h_scr[...]
        a = _dot(h, w1_ref[:, :width])
        b = _dot(h, w3_ref[:, :width])
        acc_scr[...] += _dot((_silu(a) * b).astype(BF16), w2_ref[:width, :])

    last = pl.num_programs(1) - 1
    tail = D_FF - (-(-D_FF // tf) - 1) * tf
    if tail == tf:
        accumulate(tf)
    else:
        pl.when(f < last)(lambda: accumulate(tf))
        pl.when(f == last)(lambda: accumulate(tail))

    @pl.when(f == last)
    def _():
        y = x_ref[...] + HALF * gt_ref[...] * acc_scr[...]
        if final:
            var = jnp.mean(y * y, axis=-1, keepdims=True)
            y = y * lax.rsqrt(var + EPS) * fg_ref[...]
        o_ref[...] = y


def _ffn(x, mod, norm_g3, w13, w2, layer, s, final_g=None, emit=False):
    m = x.shape[0]
    tm = min(TM_ROWS, m)
    tf = TF_EMIT if emit else TF_FFN
    steps = -(-D_FF // tf)
    sub = 2 * s
    final = final_g is not None
    w1, w3 = w13
    if emit:
        assert m == tm
        once = dict(pipeline_mode=pl.Buffered(1))
        up = pl.BlockSpec((None, None, D_MODEL, tf), lambda i, f: (layer, s, 0, f))
        down = pl.BlockSpec((None, None, tf, D_MODEL), lambda i, f: (layer, s, f, 0))
    else:
        once = {}
        up = pl.BlockSpec((D_MODEL, tf), lambda i, f: (0, f))
        down = pl.BlockSpec((tf, D_MODEL), lambda i, f: (f, 0))
    in_specs = [
        pl.BlockSpec((tm, D_MODEL), lambda i, f: (i, 0), **once),
        pl.BlockSpec((None, 1, D_MODEL), lambda i, f: (layer * 3 + sub, 0, 0)),
        mod.spec(layer, tm, 3 * sub, **once), mod.spec(layer, tm, 3 * sub + 1, **once),
        mod.spec(layer, tm, 3 * sub + 2, **once),
        up, up, down,
    ]
    args = [x, norm_g3, mod.arr, mod.arr, mod.arr, w1, w3, w2]
    if final:
        in_specs.append(pl.BlockSpec((1, D_MODEL), lambda i, f: (0, 0)))
        args.append(final_g.reshape(1, D_MODEL))
    out_specs = [pl.BlockSpec((tm, D_MODEL), lambda i, f: (i, 0))]
    out_shape = [jax.ShapeDtypeStruct((m, D_MODEL), F32)]
    if emit:
        out_specs += [pl.BlockSpec((D_MODEL, tf), lambda i, f: (0, f))] * 2 + [pl.BlockSpec((tf, D_MODEL), lambda i, f: (f, 0))]
        out_shape += [jax.ShapeDtypeStruct((D_MODEL, D_FF), BF16)] * 2 + [jax.ShapeDtypeStruct((D_FF, D_MODEL), BF16)]
    outs = pl.pallas_call(
        functools.partial(_ffn_body, final=final, emit=emit, tf=tf),
        grid=(m // tm, steps),
        in_specs=in_specs,
        out_specs=out_specs,
        out_shape=out_shape,
        scratch_shapes=[pltpu.VMEM((tm, D_MODEL), BF16), pltpu.VMEM((tm, D_MODEL), F32)],
        compiler_params=_cparams("parallel", "arbitrary"),
        name="ffn_cast" if emit else "ffn",
    )(*args)
    return (outs[0], (outs[1], outs[2]), outs[3]) if emit else outs[0]


def _inproj_body(x_ref, g_ref, sh_ref, sc_ref, w_ref, o_ref, h_scr):
    @pl.when(pl.program_id(1) == 0)
    def _():
        h_scr[...] = _norm_mod(x_ref[...], g_ref[...], sh_ref[...], sc_ref[...]).astype(BF16)

    o_ref[...] = _dot(h_scr[...], w_ref[...])


def _inproj(x, mod, norm_g3, w_in, layer):
    m = x.shape[0]
    tm = min(TM_PROJ, m)
    return pl.pallas_call(
        _inproj_body,
        grid=(m // tm, N_PROJ // TN_PROJ),
        in_specs=[
            pl.BlockSpec((tm, D_MODEL), lambda i, n: (i, 0)),
            pl.BlockSpec((None, 1, D_MODEL), lambda i, n: (layer * 3 + 1, 0, 0)),
            mod.spec(layer, tm, 3), mod.spec(layer, tm, 4),
            pl.BlockSpec((None, D_MODEL, TN_PROJ), lambda i, n: (layer, 0, n)),
        ],
        out_specs=pl.BlockSpec((tm, TN_PROJ), lambda i, n: (i, n)),
        out_shape=jax.ShapeDtypeStruct((m, N_PROJ), F32),
        scratch_shapes=[pltpu.VMEM((tm, D_MODEL), BF16)],
        compiler_params=_cparams("parallel", "arbitrary"),
        name="inproj",
    )(x, norm_g3, mod.arr, mod.arr, w_in)


def _outproj_body(x_ref, oa_ref, ob_ref, gt_ref, wa_ref, wb_ref, o_ref):
    acc = _dot(oa_ref[...], wa_ref[...]) + _dot(ob_ref[...], wb_ref[...])
    o_ref[...] = x_ref[...] + gt_ref[...] * acc


def _outproj(x, oa, ob, mod, w_out, layer):
    m = x.shape[0]
    tm = min(TM_ROWS, m)
    return pl.pallas_call(
        _outproj_body,
        grid=(m // tm,),
        in_specs=[
            pl.BlockSpec((tm, D_MODEL), lambda i: (i, 0)),
            pl.BlockSpec((tm, W_A), lambda i: (i, 0)),
            pl.BlockSpec((tm, W_B), lambda i: (i, 0)),
            mod.spec(layer, tm, 5),
            pl.BlockSpec((None, W_A, D_MODEL), lambda i: (layer, 0, 0)),
            pl.BlockSpec((None, W_B, D_MODEL), lambda i: (layer, 1, 0)),
        ],
        out_specs=pl.BlockSpec((tm, D_MODEL), lambda i: (i, 0)),
        out_shape=jax.ShapeDtypeStruct((m, D_MODEL), F32),
        compiler_params=_cparams("parallel"),
        name="outproj",
    )(x, oa, ob, mod.arr, w_out, w_out)


def _lru_body(*refs, seq_len, has_state):
    if has_state:
        (x_ref, y_ref, pe_ref, h0_ref, cw_ref, cb_ref, wa_ref, wx_ref, ba_ref, bx_ref, lam_ref,
         ob_ref, hl_ref, a_scr, b_scr) = refs
    else:
        (x_ref, y_ref, cw_ref, cb_ref, wa_ref, wx_ref, ba_ref, bx_ref, lam_ref,
         ob_ref, hl_ref, a_scr, b_scr) = refs
    rows = x_ref.shape[0]
    if has_state:
        xc = _conv_rows(x_ref[...], pe_ref[...], cw_ref[...], _row_iota(x_ref.shape, seq_len))
    else:
        xc = _conv_zero_past(x_ref, cw_ref[...])
    xc = xc + cb_ref[...]
    xcb = xc.astype(BF16)
    r = jax.nn.sigmoid(_dot(xcb, wa_ref[...]) + ba_ref[...])
    i = jax.nn.sigmoid(_dot(xcb, wx_ref[...]) + bx_ref[...])
    log_a = -LRU_C * r * jax.nn.softplus(-lam_ref[...])
    a = jnp.exp(log_a)
    b = jnp.sqrt(-_expm1(2.0 * log_a)) * (i * xc)
    if has_state:
        b = b + a * h0_ref[...]
    group = min(seq_len, SUBLANES)
    tg = _row_iota(x_ref.shape, group)
    d = 1
    while d < group:
        keep = tg >= d
        b = jnp.where(keep, a * pltpu.roll(b, d, axis=0) + b, b)
        a = jnp.where(keep, a * pltpu.roll(a, d, axis=0), a)
        d *= 2
    if seq_len > SUBLANES:
        a_scr[...] = a
        b_scr[...] = b

        def step(k, carry):
            r0 = pl.multiple_of(k * SUBLANES, SUBLANES)
            hk = b_scr[pl.ds(r0, SUBLANES), :] + a_scr[pl.ds(r0, SUBLANES), :] * carry
            b_scr[pl.ds(r0, SUBLANES), :] = hk
            return hk[SUBLANES - 1:SUBLANES, :]

        lax.fori_loop(0, rows // SUBLANES, step, jnp.zeros((1, LRU_BW), F32), unroll=8)
        h = b_scr[...]
    else:
        h = b
    ob_ref[...] = (h * jax.nn.gelu(y_ref[...])).astype(BF16)
    hl_ref[...] = h[rows - hl_ref.shape[0]:, :]


def _lru(proj, seq_len, layer, lconv_w, lconv_b, wa, wx, ba, bx, lam, pe=None, h0e=None):
    m = proj.shape[0]
    has_state = pe is not None
    rows = seq_len if not has_state else m
    nseq = m // rows
    hl_rows = rows if has_state else SUBLANES
    bx0, by0 = COL_X // LRU_BW, COL_Y // LRU_BW
    col = lambda b, n: (b, n)
    vec = pl.BlockSpec((None, 1, LRU_BW), lambda b, n: (layer, 0, n))
    blk = pl.BlockSpec((None, None, LRU_BW, LRU_BW), lambda b, n: (layer, n, 0, 0))
    in_specs = [pl.BlockSpec((rows, LRU_BW), lambda b, n: (b, bx0 + n)),
                pl.BlockSpec((rows, LRU_BW), lambda b, n: (b, by0 + n))]
    args = [proj, proj]
    if has_state:
        in_specs += [pl.BlockSpec((rows, LRU_BW), col), pl.BlockSpec((rows, LRU_BW), col)]
        args += [pe, h0e]
    in_specs += [pl.BlockSpec((None, CONV_W, LRU_BW), lambda b, n: (layer, 0, n)), vec, blk, blk, vec, vec, vec]
    args += [lconv_w, lconv_b, wa, wx, ba, bx, lam]
    return pl.pallas_call(
        functools.partial(_lru_body, seq_len=seq_len, has_state=has_state),
        grid=(nseq, LRU_BLOCKS),
        in_specs=in_specs,
        out_specs=[pl.BlockSpec((rows, LRU_BW), col), pl.BlockSpec((hl_rows, LRU_BW), col)],
        out_shape=[jax.ShapeDtypeStruct((m, W_B), BF16), jax.ShapeDtypeStruct((nseq * hl_rows, W_B), F32)],
        scratch_shapes=[pltpu.VMEM((rows, LRU_BW), F32), pltpu.VMEM((rows, LRU_BW), F32)],
        compiler_params=_cparams("parallel", "parallel"),
        name="lru",
    )(*args)


def _qkv_prep(q_conv, k_conv, v_conv, heads):
    q, k, v = _silu(q_conv), _silu(k_conv), _silu(v_conv)
    qs, ks = [], []
    for h in range(heads):
        sl = slice(h * DK, (h + 1) * DK)
        qh, kh = q[:, sl], k[:, sl]
        qs.append(qh * (lax.rsqrt(jnp.sum(qh * qh, axis=-1, keepdims=True) + EPS) * DK ** -0.5))
        ks.append(kh * lax.rsqrt(jnp.sum(kh * kh, axis=-1, keepdims=True) + EPS))
    return jnp.concatenate(qs, axis=1), jnp.concatenate(ks, axis=1), v


def _gate_norm(o, onorm, gate_pre):
    var = jnp.mean(o * o, axis=-1, keepdims=True)
    return o * lax.rsqrt(var + EPS) * onorm * _silu(gate_pre)


def _seg_cumsum(g, tg, group):
    d = 1
    while d < group:
        g = jnp.where(tg >= d, g + pltpu.roll(g, d, axis=0), g)
        d *= 2
    return g


def _delta_prompt_body(q_ref, k_ref, v_ref, gt_ref, ab_ref, cwq_ref, cwk_ref, cwv_ref, alog_ref, dtb_ref,
                       on_ref, oa_ref, s_ref, qn_scr, kn_scr, vv_scr, gc_scr, bt_scr, s_scr,
                       wq_scr, uc_scr, qk_scr, kdt_scr, sd_scr):
    rows = q_ref.shape[0]
    hg = q_ref.shape[1] // DK
    qn, kn, vv = _qkv_prep(_conv_zero_past(q_ref, cwq_ref[...]), _conv_zero_past(k_ref, cwk_ref[...]),
                           _conv_zero_past(v_ref, cwv_ref[...]), hg)
    qn_scr[...] = qn
    kn_scr[...] = kn
    vv_scr[...] = vv
    ab = ab_ref[...]
    g = -jnp.exp(alog_ref[...]) * jax.nn.softplus(ab + dtb_ref[...])
    gc_scr[...] = _seg_cumsum(g, _row_iota(ab.shape, CHUNK), CHUNK)
    bt_scr[...] = jax.nn.sigmoid(ab)
    s_scr[...] = jnp.zeros_like(s_scr)
    head0 = pl.program_id(1) * hg
    onorm = on_ref[...]

    def factor(step, carry):
        lane = lax.broadcasted_iota(jnp.int32, (CHUNK, LANES), 1)
        ri = lax.broadcasted_iota(jnp.int32, (CHUNK, CHUNK), 0)
        ci = lax.broadcasted_iota(jnp.int32, (CHUNK, CHUNK), 1)
        causal = ri >= ci
        strict = ri > ci
        blk = jnp.bitwise_xor(ri, ci)
        chains = []
        for cc in range(FACTOR_CHUNKS):
            c = step * FACTOR_CHUNKS + cc
            rs = pl.ds(pl.multiple_of(c * CHUNK, CHUNK), CHUNK)
            gcc = gc_scr[rs, :]
            btc = bt_scr[rs, :]
            for hh in range(hg):
                sl = slice(hh * DK, (hh + 1) * DK)
                gcol = jnp.sum(jnp.where(lane == head0 + hh, gcc, 0.0), axis=1, keepdims=True)
                bcol = jnp.sum(jnp.where(lane == head0 + hh + H_A, btc, 0.0), axis=1, keepdims=True)
                gmat = jnp.broadcast_to(gcol, (CHUNK, CHUNK))
                decay = jnp.where(causal, jnp.exp(jnp.where(causal, gmat - gmat.T, 0.0)), 0.0)
                q = qn_scr[rs, sl]
                k = kn_scr[rs, sl]
                kq = _dot_nt(jnp.concatenate([k, q], axis=0).astype(BF16), k.astype(BF16))
                m = jnp.where(strict, bcol * kq[:CHUNK] * decay, 0.0)
                eg = jnp.exp(gcol)
                glast = jnp.broadcast_to(gcol[CHUNK - 1:CHUNK, :], (CHUNK, 1))
                wq_scr[hh, c, CHUNK:, :] = (q * eg).astype(BF16)
                qk_scr[hh, rs, :] = (kq[CHUNK:] * decay).astype(BF16)
                kdt_scr[hh, rs, :] = (k * jnp.exp(glast - gcol)).T.astype(BF16)
                sd_scr[hh, pl.ds(pl.multiple_of(c * SUBLANES, SUBLANES), SUBLANES), :] = jnp.broadcast_to(
                    jnp.broadcast_to(eg, (CHUNK, DV))[CHUNK - 1:CHUNK, :], (SUBLANES, DV))
                chains.append((hh, c, rs, sl, m, bcol, eg))
        es = [-jnp.where(lax.shift_right_logical(blk, 1) == 0, ch[4], 0.0) for ch in chains]
        lg = 1
        while (1 << lg) < CHUNK:
            cross = lax.shift_right_logical(blk, lg) == 1
            offs = [jnp.where(cross, ch[4], 0.0) for ch in chains]
            ebs = [e.astype(BF16) for e in es]
            xs = [off + _dot(eb, off.astype(BF16)) for off, eb in zip(offs, ebs)]
            es = [e - (x + _dot(x.astype(BF16), eb)) for e, x, eb in zip(es, xs, ebs)]
            lg += 1
        for (hh, c, rs, sl, m, bcol, eg), e in zip(chains, es):
            rhs = jnp.concatenate([kn_scr[rs, sl] * (bcol * eg), vv_scr[rs, sl] * bcol], axis=1)
            sol = rhs + _dot(e.astype(BF16), rhs.astype(BF16))
            wq_scr[hh, c, :CHUNK, :] = sol[:, :DK].astype(BF16)
            uc_scr[hh, rs, :] = sol[:, DK:]
        return carry

    lax.fori_loop(0, rows // (CHUNK * FACTOR_CHUNKS), factor, 0)

    def recur(c, carry):
        rs = pl.ds(pl.multiple_of(c * CHUNK, CHUNK), CHUNK)
        s_old = [s_scr[hh] for hh in range(hg)]
        ps = [_dot(wq_scr[hh, c], s_old[hh].astype(BF16)) for hh in range(hg)]
        us = [(uc_scr[hh, rs, :] - ps[hh][:CHUNK]).astype(BF16) for hh in range(hg)]
        os_ = [ps[hh][CHUNK:] + _dot(qk_scr[hh, rs, :], us[hh]) for hh in range(hg)]
        for hh in range(hg):
            sdec = sd_scr[hh, pl.ds(pl.multiple_of(c * SUBLANES, SUBLANES), SUBLANES), :]
            s_scr[hh] = s_old[hh] * sdec[:1, :] + _dot(kdt_scr[hh, rs, :], us[hh])
        for hh in range(hg):
            sl = slice(hh * DK, (hh + 1) * DK)
            oa_ref[rs, sl] = _gate_norm(os_[hh], onorm, gt_ref[rs, sl]).astype(BF16)
        return carry

    lax.fori_loop(0, rows // CHUNK, recur, 0)
    s_ref[...] = s_scr[...]


def _delta_prompt(proj, batch, seq_len, layer, dconv_w, alog_row, dtb_row, onorm):
    m = proj.shape[0]
    hg = HG_PROMPT
    wcols = hg * DK
    nchunks = seq_len // CHUNK
    nq, nk, nv, ng = (c // wcols for c in (COL_Q, COL_K, COL_V, COL_G))
    colspec = lambda off: pl.BlockSpec((seq_len, wcols), lambda b, j: (b, off + j))
    cwspec = lambda off: pl.BlockSpec((None, CONV_W, wcols), lambda b, j: (layer, 0, off + j))
    row = pl.BlockSpec((None, 1, LANES), lambda b, j: (layer, 0, 0))
    return pl.pallas_call(
        _delta_prompt_body,
        grid=(batch, H_A // hg),
        in_specs=[colspec(nq), colspec(nk), colspec(nv), colspec(ng),
                  pl.BlockSpec((seq_len, LANES), lambda b, j: (b, COL_AB // LANES)),
                  cwspec(nq), cwspec(nk), cwspec(nv), row, row, row],
        out_specs=[pl.BlockSpec((seq_len, wcols), lambda b, j: (b, j)),
                   pl.BlockSpec((None, hg, DK, DV), lambda b, j: (b, j, 0, 0))],
        out_shape=[jax.ShapeDtypeStruct((m, W_A), BF16), jax.ShapeDtypeStruct((batch, H_A, DK, DV), F32)],
        scratch_shapes=[pltpu.VMEM((seq_len, wcols), F32)] * 3
        + [pltpu.VMEM((seq_len, LANES), F32)] * 2 + [pltpu.VMEM((hg, DK, DV), F32)]
        + [pltpu.VMEM((hg, nchunks, 2 * CHUNK, DK), BF16), pltpu.VMEM((hg, seq_len, DV), F32),
           pltpu.VMEM((hg, seq_len, CHUNK), BF16), pltpu.VMEM((hg, seq_len, CHUNK), BF16),
           pltpu.VMEM((hg, nchunks * SUBLANES, DV), F32)],
        compiler_params=_cparams("parallel", "parallel"),
        name="delta_prompt",
    )(proj, proj, proj, proj, proj, dconv_w, dconv_w, dconv_w, alog_row, dtb_row, onorm)


def _delta_sample_body(q_ref, k_ref, v_ref, gt_ref, ab_ref, pe_ref, cw_ref, alog_ref, dtb_ref, on_ref, s0_ref,
                       *rest, seq_len):
    oa_ref, s_ref, qn_scr, kn_scr, vv_scr, gc_scr, bt_scr = rest[-7:]
    rows = q_ref.shape[0]
    t = _row_iota(q_ref.shape, seq_len)
    pe = pe_ref[...]
    cw = cw_ref[...]
    qn, kn, vv = _qkv_prep(_conv_rows(q_ref[...], pe[:, COL_Q:COL_K], cw[:, COL_Q:COL_K], t),
                           _conv_rows(k_ref[...], pe[:, COL_K:COL_V], cw[:, COL_K:COL_V], t),
                           _conv_rows(v_ref[...], pe[:, COL_V:COL_G], cw[:, COL_V:COL_G], t), H_A)
    qn_scr[...] = qn
    kn_scr[...] = kn
    vv_scr[...] = vv
    ab = ab_ref[...]
    g = -jnp.exp(alog_ref[...]) * jax.nn.softplus(ab + dtb_ref[...])
    gc_scr[...] = _seg_cumsum(g, _row_iota(ab.shape, seq_len), seq_len)
    bt_scr[...] = jax.nn.sigmoid(ab)
    onorm = on_ref[...]
    per_tile = SUBLANES // seq_len

    def tile(p, carry):
        r0 = pl.multiple_of(p * SUBLANES, SUBLANES)
        rs = pl.ds(r0, SUBLANES)
        gcc = gc_scr[rs, :]
        btc = bt_scr[rs, :]
        tt = _row_iota((SUBLANES, LANES), seq_len)
        tt2 = _row_iota((SUBLANES, DK + DV), seq_len)
        ri = lax.broadcasted_iota(jnp.int32, (SUBLANES, LANES), 0)
        ri2 = _row_iota((2 * SUBLANES, LANES), SUBLANES)
        lane = lax.broadcasted_iota(jnp.int32, (LANES, LANES), 1)
        zpad = jnp.zeros((LANES - SUBLANES, DK), F32)
        staged = []
        for h in range(H_A):
            sl = slice(h * DK, (h + 1) * DK)
            gc = jnp.broadcast_to(gcc[:, h:h + 1], (SUBLANES, LANES))
            beta = jnp.broadcast_to(btc[:, H_A + h:H_A + h + 1], (SUBLANES, LANES))
            q = qn_scr[rs, sl]
            k = kn_scr[rs, sl]
            v = vv_scr[rs, sl]
            eg = jnp.exp(gc)
            decs, ms = [None], [None]
            for d in range(1, seq_len):
                ok = tt >= d
                dec = jnp.where(ok, jnp.exp(jnp.where(ok, gc - pltpu.roll(gc, d, axis=0), 0.0)), 0.0)
                kk = jnp.sum(k * pltpu.roll(k, d, axis=0), axis=-1, keepdims=True)
                decs.append(dec)
                md = beta * kk * dec
                ms.append(jnp.concatenate([md, md], axis=1))
            rhs = jnp.concatenate([k * (beta * eg), v * beta], axis=1)
            sol = rhs
            for step in range(1, seq_len):
                acc = ms[1] * pltpu.roll(sol, 1, axis=0)
                for d in range(2, seq_len):
                    acc = acc + ms[d] * pltpu.roll(sol, d, axis=0)
                sol = jnp.where(tt2 == step, rhs - acc, sol)
            lhs = jnp.concatenate([sol[:, :DK], q * eg], axis=0).astype(BF16)
            staged.append((sl, gc, q, k, eg, decs, sol[:, DK:], lhs))
        pps = []
        for h in range(H_A):
            lhs = staged[h][-1]
            pp = _dot(lhs, s0_ref[p * per_tile, h].astype(BF16))
            for bb in range(1, per_tile):
                pp = jnp.where(ri2 >= bb * seq_len, _dot(lhs, s0_ref[p * per_tile + bb, h].astype(BF16)), pp)
            pps.append(pp)
        updates = []
        for h in range(H_A):
            sl, gc, q, k, eg, decs, uc, _ = staged[h]
            pp = pps[h]
            u = uc - pp[:SUBLANES]
            o = pp[SUBLANES:] + jnp.sum(q * k, axis=-1, keepdims=True) * u
            for d in range(1, seq_len):
                qk = jnp.sum(q * pltpu.roll(k, d, axis=0), axis=-1, keepdims=True)
                o = o + (qk * decs[d]) * pltpu.roll(u, d, axis=0)
            oa_ref[rs, sl] = _gate_norm(o, onorm, gt_ref[rs, sl]).astype(BF16)
            glast = jnp.broadcast_to(gc[seq_len - 1:seq_len, :], (SUBLANES, LANES))
            for bb in range(1, per_tile):
                last = bb * seq_len + seq_len - 1
                glast = jnp.where(ri >= bb * seq_len, jnp.broadcast_to(gc[last:last + 1, :], (SUBLANES, LANES)), glast)
            kd = k * jnp.exp(glast - gc)
            kdt = jnp.concatenate([kd, zpad], axis=0).T
            updates.append((kdt, jnp.concatenate([u, zpad], axis=0).astype(BF16)))
        for h in range(H_A):
            kdt, upad = updates[h]
            eg = staged[h][4]
            for bb in range(per_tile):
                last = bb * seq_len + seq_len - 1
                cols = (lane >= bb * seq_len) & (lane < (bb + 1) * seq_len)
                kd_b = jnp.where(cols, kdt, 0.0).astype(BF16)
                s_ref[p * per_tile + bb, h] = (s0_ref[p * per_tile + bb, h] * eg[last:last + 1, :]
                                               + _dot(kd_b, upad))
        return carry

    lax.fori_loop(0, rows // SUBLANES, tile, 0)


def _delta_sample(proj, seq_len, layer, pe, dconv_w, alog_row, dtb_row, onorm, s0, s_stack):
    m = proj.shape[0]
    tb = TB_SAMPLE
    rows = tb * seq_len
    colspec = lambda off: pl.BlockSpec((rows, W_A), lambda i: (i, off // W_A))
    row = pl.BlockSpec((None, 1, LANES), lambda i: (layer, 0, 0))
    sspec = pl.BlockSpec((None, tb, H_A, DK, DV), lambda i: (layer, i, 0, 0, 0))
    in_specs = [colspec(COL_Q), colspec(COL_K), colspec(COL_V), colspec(COL_G),
                pl.BlockSpec((rows, LANES), lambda i: (i, COL_AB // LANES)),
                pl.BlockSpec((rows, N_DCONV), lambda i: (i, 0)),
                pl.BlockSpec((None, CONV_W, N_DCONV), lambda i: (layer, 0, 0)),
                row, row, row, sspec]
    args = [proj, proj, proj, proj, proj, pe, dconv_w, alog_row, dtb_row, onorm, s0]
    aliases = {}
    if s_stack is not None:
        in_specs.append(pl.BlockSpec(memory_space=pl.ANY))
        aliases = {len(args): 1}
        args.append(s_stack)
    return pl.pallas_call(
        functools.partial(_delta_sample_body, seq_len=seq_len),
        grid=(m // rows,),
        in_specs=in_specs,
        out_specs=[pl.BlockSpec((rows, W_A), lambda i: (i, 0)), sspec],
        out_shape=[jax.ShapeDtypeStruct((m, W_A), BF16), jax.ShapeDtypeStruct(s0.shape, F32)],
        input_output_aliases=aliases,
        scratch_shapes=[pltpu.VMEM((rows, W_A), F32)] * 3 + [pltpu.VMEM((rows, LANES), F32)] * 2,
        compiler_params=_cparams("parallel"),
        name="delta_sample",
    )(*args)


def _pad_rows(state, seq_len):
    b, r, c = state.shape
    return jnp.pad(state, ((0, 0), (0, seq_len - r), (0, 0))).reshape(b * seq_len, c)


def _trunk(x, mod, batch, seq_len, wts, states, ffn_cast):
    (w_in, w_out, norm_g3, ffn_f32, dconv_w, alog_row, dtb_row, onorm, lconv_w, lconv_b,
     wa, wx, ba, bx, lam, final_g) = wts
    new_d, new_dc, new_l, new_lc = [], [], [], []
    s_stack = None

    def ffn(x, layer, s, fg=None):
        if (layer, s) in ffn_cast:
            return _ffn(x, mod, norm_g3, *ffn_cast[(layer, s)], layer, s, fg)
        y, w13, w2 = _ffn(x, mod, norm_g3, ffn_f32[:2], ffn_f32[2], layer, s, fg, emit=True)
        ffn_cast[(layer, s)] = (w13, w2)
        return y

    for layer in range(DEPTH):
        x = ffn(x, layer, 0)
        proj = _inproj(x, mod, norm_g3, w_in, layer)
        proj3 = proj.reshape(batch, seq_len, N_PROJ)
        if states is None:
            oa, sd = _delta_prompt(proj, batch, seq_len, layer, dconv_w, alog_row, dtb_row, onorm)
            new_d.append(sd)
            ob, hl = _lru(proj, seq_len, layer, lconv_w, lconv_b, wa, wx, ba, bx, lam)
            sl = hl.reshape(batch, SUBLANES, W_B)[:, SUBLANES - 1]
        else:
            s_delta, s_dconv, s_lru, s_lconv = states
            oa, s_stack = _delta_sample(proj, seq_len, layer, _pad_rows(s_dconv[layer], seq_len),
                                        dconv_w, alog_row, dtb_row, onorm, s_delta, s_stack)
            ob, hl = _lru(proj, seq_len, layer, lconv_w, lconv_b, wa, wx, ba, bx, lam,
                          pe=_pad_rows(s_lconv[layer], seq_len),
                          h0e=_pad_rows(s_lru[layer][:, None, :], seq_len))
            sl = hl.reshape(batch, seq_len, W_B)[:, seq_len - 1]
        x = _outproj(x, oa, ob, mod, w_out, layer)
        x = ffn(x, layer, 1, final_g if layer == DEPTH - 1 else None)
        new_dc.append(proj3[:, seq_len - (CONV_W - 1):, COL_Q:COL_G])
        new_l.append(sl)
        new_lc.append(proj3[:, seq_len - (CONV_W - 1):, COL_X:COL_Y])
    new_delta = jnp.stack(new_d) if states is None else s_stack
    return x, new_delta, jnp.stack(new_dc), jnp.stack(new_l), jnp.stack(new_lc)


def kernel(x_prompt, x_sample, c_prompt, c_sample, state_delta, state_delta_conv, state_lru, state_lru_conv, w_in, w_out, norm_g, w_ada, b_ada, ffn_w1, ffn_w3, ffn_w2, dconv_w, d_alog, d_dtbias, d_onorm, lconv_w, lconv_b, lru_wa, lru_ba, lru_wx, lru_bx, lru_lam, final_g):
    bp, lp, _ = x_prompt.shape
    bs, ls, _ = x_sample.shape
    assert lp % CHUNK == 0 and SUBLANES % ls == 0 and bp <= SUBLANES

    w_in_r = jnp.concatenate(
        [w_in[..., :OFF_A], w_in[..., OFF_X:], w_in[..., OFF_A:OFF_X],
         jnp.zeros((DEPTH, D_MODEL, N_PROJ - N_IN), w_in.dtype)], axis=-1).astype(BF16)
    w_out_b = w_out.astype(BF16)
    wa = lru_wa.astype(BF16)
    wx = lru_wx.astype(BF16)
    norm_g3 = norm_g.reshape(DEPTH * 3, 1, D_MODEL)
    lane_row = lambda v: jnp.pad(v, ((0, 0), (0, LANES - v.shape[1]))).reshape(DEPTH, 1, LANES)
    vec = lambda v: v.reshape(DEPTH, 1, W_B)
    wts = (w_in_r, w_out_b, norm_g3, (ffn_w1, ffn_w3, ffn_w2), dconv_w, lane_row(d_alog), lane_row(d_dtbias),
           d_onorm.reshape(DEPTH, 1, DV), lconv_w, vec(lconv_b), wa, wx, vec(lru_ba), vec(lru_bx),
           vec(lru_lam), final_g)

    ms = bs * ls
    c_all = jnp.concatenate([jnp.repeat(c_sample, ls, axis=0), c_prompt,
                             jnp.zeros((SUBLANES - bp, D_MODEL), F32)], axis=0)
    mod_all = _ada(c_all, w_ada, b_ada)
    mod_p = _Mod(mod_all[:, ms:ms + bp].reshape(DEPTH, bp, 1, 9 * D_MODEL), False, lp)
    mod_s = _Mod(mod_all, True, ls)

    ffn_cast = {}
    ys, sd, sdc, sl_, slc = _trunk(x_sample.reshape(bs * ls, D_MODEL), mod_s, bs, ls, wts,
                                   (state_delta, state_delta_conv, state_lru, state_lru_conv), ffn_cast)
    yp, pd, pdc, pl_, plc = _trunk(x_prompt.reshape(bp * lp, D_MODEL), mod_p, bp, lp, wts, None, ffn_cast)
    return (yp.reshape(bp, lp, D_MODEL), ys.reshape(bs, ls, D_MODEL), pd, pdc, pl_, plc, sd, sdc, sl_, slc)
```

```python
import functools

import jax
import jax.numpy as jnp
from jax import lax
from jax.experimental import pallas as pl
from jax.experimental.pallas import tpu as pltpu

F32 = jnp.float32
BF16 = jnp.bfloat16

D_MODEL = 2048
DEPTH = 4
H_A = 8
DK = 128
DV = 128
W_QK = H_A * DK
W_A = H_A * DV
W_B = D_MODEL - W_A
LRU_BLOCKS = 8
LRU_BW = W_B // LRU_BLOCKS
LRU_C = 8.0
CONV_W = 4
D_FF = ((8 * D_MODEL // 3 + 127) // 128) * 128
HALF = 0.5
EPS = 1e-6
N_DCONV = 2 * W_QK + W_A
OFF_G = 2 * W_QK + W_A
OFF_A = OFF_G + W_A
OFF_X = OFF_A + 2 * H_A
N_IN = OFF_X + 2 * W_B

LANES = 128
SUBLANES = 8
VMEM_LIMIT = 56 * 1024 * 1024

COL_Q, COL_K, COL_V, COL_G = 0, W_QK, 2 * W_QK, 2 * W_QK + W_A
COL_AB = OFF_A
COL_X = OFF_X
COL_Y = COL_X + W_B
LRU_SHIFT = COL_X % LANES

TM_ROWS = 512
TM_PROJ = 1024
TF_FFN = 512
AHEAD_ROWS = 64
TF_EMIT = 256
TN_PROJ = 1280
TN_PROJ_EMIT = 640
N_PROJ = -(-N_IN // TN_PROJ) * TN_PROJ
TN_ADA = 1024
CHUNK = 128
HG_PROMPT = 2
FACTOR_CHUNKS = 4
TB_SAMPLE = 16

assert N_PROJ % TN_PROJ_EMIT == 0 and N_PROJ >= -(-N_IN // LANES) * LANES
assert (9 * D_MODEL) % TN_ADA == 0 and D_FF % LANES == 0
assert COL_AB % LANES == 0 and COL_Y % LANES == LRU_SHIFT and 2 * H_A <= LRU_SHIFT


def _cparams(*sem):
    return pltpu.CompilerParams(dimension_semantics=sem, vmem_limit_bytes=VMEM_LIMIT)


def _dot(a, b):
    return jnp.dot(a, b, preferred_element_type=F32)


def _dot_nt(a, b):
    return lax.dot_general(a, b, (((1,), (1,)), ((), ())), preferred_element_type=F32)


def _silu(x):
    return x * jax.nn.sigmoid(x)


def _expm1(x):
    u = jnp.exp(x)
    um1 = u - 1.0
    safe = um1 * x / jnp.log(jnp.where(um1 == 0.0, 2.0, u))
    return jnp.where(um1 == 0.0, x, jnp.where(x < -30.0, um1, safe))


def _norm_mod(x, g, shift, scale):
    var = jnp.mean(x * x, axis=-1, keepdims=True)
    y = x * lax.rsqrt(var + EPS) * g
    return y * (1.0 + scale) + shift


def _row_iota(shape, mod):
    return jnp.bitwise_and(lax.broadcasted_iota(jnp.int32, shape, 0), mod - 1)


def _conv_zero_past(x_ref, w):
    rows = x_ref.shape[0]
    head = x_ref[:SUBLANES, :]
    head_y = _conv_rows(head, None, w, lax.broadcasted_iota(jnp.int32, head.shape, 0))
    y = x_ref[SUBLANES:, :] * w[CONV_W - 1:CONV_W, :]
    for j in range(1, CONV_W):
        y = y + x_ref[SUBLANES - j:rows - j, :] * w[CONV_W - 1 - j:CONV_W - j, :]
    return jnp.concatenate([head_y, y], axis=0)


def _conv_rows(x, pe, w, t):
    rows = x.shape[0]
    y = x * w[CONV_W - 1:CONV_W, :]
    for j in range(1, CONV_W):
        term = pltpu.roll(x, j, axis=0)
        if pe is None:
            term = jnp.where(t >= j, term, 0.0)
        else:
            back = CONV_W - 1 - j
            prev = pe if back == 0 else pltpu.roll(pe, rows - back, axis=0)
            term = jnp.where(t >= j, term, prev)
        y = y + term * w[CONV_W - 1 - j:CONV_W - j, :]
    return y


def _ada_body(c_ref, w_ref, b_ref, o_ref):
    cs = _silu(c_ref[...]).astype(BF16)
    o_ref[...] = _dot(cs, w_ref[...].astype(BF16)) + b_ref[...]


def _ada(c_all, w_ada, b_ada):
    rows = c_all.shape[0]
    n9 = 9 * D_MODEL
    return pl.pallas_call(
        _ada_body,
        grid=(DEPTH, n9 // TN_ADA),
        in_specs=[
            pl.BlockSpec((rows, D_MODEL), lambda l, n: (0, 0)),
            pl.BlockSpec((None, D_MODEL, TN_ADA), lambda l, n: (l, 0, n)),
            pl.BlockSpec((None, 1, TN_ADA), lambda l, n: (l, 0, n)),
        ],
        out_specs=pl.BlockSpec((None, rows, TN_ADA), lambda l, n: (l, 0, n)),
        out_shape=jax.ShapeDtypeStruct((DEPTH, rows, n9), F32),
        compiler_params=_cparams("parallel", "parallel"),
        name="ada",
    )(c_all, w_ada, b_ada.reshape(DEPTH, 1, n9))


class _Mod:
    def __init__(self, arr, per_token, seq_len):
        self.arr = arr
        self.per_token = per_token
        self.seq_len = seq_len

    def spec(self, layer, tm, k, next_of=None, **kw):
        tile = (lambda i: i) if next_of is None else (lambda i: jnp.minimum(i + 1, next_of - 1))
        if self.per_token:
            return pl.BlockSpec((None, tm, D_MODEL), lambda i, *_: (layer, tile(i), k), **kw)
        seq_len = self.seq_len
        return pl.BlockSpec((None, None, 1, D_MODEL), lambda i, *_: (layer, (tile(i) * tm) // seq_len, 0, k), **kw)


def _ffn_body(x_ref, g_ref, sh_ref, sc_ref, gt_ref, w1_ref, w3_ref, w2_ref, *rest, final, emit, ahead, tf):
    rest = list(rest)
    fg_ref = rest.pop(0) if final else None
    if ahead:
        xn_ref, shn_ref, scn_ref = rest[:3]
        rest = rest[3:]
    o_ref = rest.pop(0)
    if emit:
        w1b_ref, w3b_ref, w2b_ref = rest[:3]
        rest = rest[3:]
        w1b_ref[...] = w1_ref[...].astype(BF16)
        w3b_ref[...] = w3_ref[...].astype(BF16)
        w2b_ref[...] = w2_ref[...].astype(BF16)
        w1_ref, w3_ref, w2_ref = w1b_ref, w3b_ref, w2b_ref
    h_scr, acc_scr = rest
    i = pl.program_id(0)
    f = pl.program_id(1)
    tm = x_ref.shape[0]
    slot = lax.rem(i, 2) if ahead else 0

    @pl.when(f == 0)
    def _():
        acc_scr[...] = jnp.zeros_like(acc_scr)

    @pl.when((f == 0) & (i == 0) if ahead else f == 0)
    def _():
        h_scr[0] = _norm_mod(x_ref[...], g_ref[...], sh_ref[...], sc_ref[...]).astype(BF16)

    def accumulate(width):
        h = h_scr[slot]
        a = _dot(h, w1_ref[:, :width])
        b = _dot(h, w3_ref[:, :width])
        acc_scr[...] += _dot((_silu(a) * b).astype(BF16), w2_ref[:width, :])
        if ahead:
            r0 = pl.multiple_of(jnp.minimum(f, tm // AHEAD_ROWS - 1) * AHEAD_ROWS, AHEAD_ROWS)
            rs = pl.ds(r0, AHEAD_ROWS)
            hn = _norm_mod(xn_ref[rs, :], g_ref[...], shn_ref[...], scn_ref[...])
            h_scr[1 - slot, rs, :] = hn.astype(BF16)

    last = pl.num_programs(1) - 1
    tail = D_FF - (-(-D_FF // tf) - 1) * tf
    if tail == tf:
        accumulate(tf)
    else:
        pl.when(f < last)(lambda: accumulate(tf))
        pl.when(f == last)(lambda: accumulate(tail))

    @pl.when(f == last)
    def _():
        y = x_ref[...] + HALF * gt_ref[...] * acc_scr[...]
        if final:
            var = jnp.mean(y * y, axis=-1, keepdims=True)
            y = y * lax.rsqrt(var + EPS) * fg_ref[...]
        o_ref[...] = y


def _ffn(x, mod, norm_g3, w13, w2, layer, s, final_g=None, emit=False):
    m = x.shape[0]
    tm = min(TM_ROWS, m)
    tf = TF_EMIT if emit else TF_FFN
    steps = -(-D_FF // tf)
    sub = 2 * s
    final = final_g is not None
    w1, w3 = w13
    if emit:
        assert m == tm
        once = dict(pipeline_mode=pl.Buffered(1))
        up = pl.BlockSpec((None, None, D_MODEL, tf), lambda i, f: (layer, s, 0, f))
        down = pl.BlockSpec((None, None, tf, D_MODEL), lambda i, f: (layer, s, f, 0))
    else:
        once = {}
        up = pl.BlockSpec((D_MODEL, tf), lambda i, f: (0, f))
        down = pl.BlockSpec((tf, D_MODEL), lambda i, f: (f, 0))
    in_specs = [
        pl.BlockSpec((tm, D_MODEL), lambda i, f: (i, 0), **once),
        pl.BlockSpec((None, 1, D_MODEL), lambda i, f: (layer * 3 + sub, 0, 0)),
        mod.spec(layer, tm, 3 * sub, **once), mod.spec(layer, tm, 3 * sub + 1, **once),
        mod.spec(layer, tm, 3 * sub + 2, **once),
        up, up, down,
    ]
    args = [x, norm_g3, mod.arr, mod.arr, mod.arr, w1, w3, w2]
    if final:
        in_specs.append(pl.BlockSpec((1, D_MODEL), lambda i, f: (0, 0)))
        args.append(final_g.reshape(1, D_MODEL))
    ntiles = m // tm
    ahead = ntiles > 1 and not mod.per_token
    if ahead:
        assert tm % AHEAD_ROWS == 0 and steps >= tm // AHEAD_ROWS
        in_specs += [pl.BlockSpec((tm, D_MODEL), lambda i, f: (jnp.minimum(i + 1, ntiles - 1), 0)),
                     mod.spec(layer, tm, 3 * sub, next_of=ntiles), mod.spec(layer, tm, 3 * sub + 1, next_of=ntiles)]
        args += [x, mod.arr, mod.arr]
    out_specs = [pl.BlockSpec((tm, D_MODEL), lambda i, f: (i, 0))]
    out_shape = [jax.ShapeDtypeStruct((m, D_MODEL), F32)]
    if emit:
        out_specs += [pl.BlockSpec((D_MODEL, tf), lambda i, f: (0, f))] * 2 + [pl.BlockSpec((tf, D_MODEL), lambda i, f: (f, 0))]
        out_shape += [jax.ShapeDtypeStruct((D_MODEL, D_FF), BF16)] * 2 + [jax.ShapeDtypeStruct((D_FF, D_MODEL), BF16)]
    outs = pl.pallas_call(
        functools.partial(_ffn_body, final=final, emit=emit, ahead=ahead, tf=tf),
        grid=(ntiles, steps),
        in_specs=in_specs,
        out_specs=out_specs,
        out_shape=out_shape,
        scratch_shapes=[pltpu.VMEM((2 if ahead else 1, tm, D_MODEL), BF16), pltpu.VMEM((tm, D_MODEL), F32)],
        compiler_params=_cparams("arbitrary" if ahead else "parallel", "arbitrary"),
        name="ffn_cast" if emit else "ffn",
    )(*args)
    return (outs[0], (outs[1], outs[2]), outs[3]) if emit else outs[0]


def _inproj_body(x_ref, g_ref, sh_ref, sc_ref, w_ref, o_ref, *rest, emit):
    h_scr = rest[-1]
    n = pl.program_id(1)

    @pl.when(n == 0)
    def _():
        h_scr[...] = _norm_mod(x_ref[...], g_ref[...], sh_ref[...], sc_ref[...]).astype(BF16)

    if emit:
        wb_ref = rest[0]
        col = lax.broadcasted_iota(jnp.int32, w_ref.shape, 1) + n * w_ref.shape[1]
        wb_ref[...] = jnp.where(col < N_IN, w_ref[...], 0.0).astype(BF16)
        w_ref = wb_ref
    o_ref[...] = _dot(h_scr[...], w_ref[...])


def _inproj(x, mod, norm_g3, w_in, layer, emit=False):
    m = x.shape[0]
    tm = min(TM_PROJ, m)
    tn = TN_PROJ_EMIT if emit else TN_PROJ
    if emit:
        assert m == tm
        once = dict(pipeline_mode=pl.Buffered(1))
        wspec = pl.BlockSpec((None, D_MODEL, tn), lambda i, n: (layer, 0, n))
    else:
        once = {}
        wspec = pl.BlockSpec((D_MODEL, tn), lambda i, n: (0, n))
    out_specs = [pl.BlockSpec((tm, tn), lambda i, n: (i, n))]
    out_shape = [jax.ShapeDtypeStruct((m, N_PROJ), F32)]
    if emit:
        out_specs.append(pl.BlockSpec((D_MODEL, tn), lambda i, n: (0, n)))
        out_shape.append(jax.ShapeDtypeStruct((D_MODEL, N_PROJ), BF16))
    outs = pl.pallas_call(
        functools.partial(_inproj_body, emit=emit),
        grid=(m // tm, N_PROJ // tn),
        in_specs=[
            pl.BlockSpec((tm, D_MODEL), lambda i, n: (i, 0), **once),
            pl.BlockSpec((None, 1, D_MODEL), lambda i, n: (layer * 3 + 1, 0, 0)),
            mod.spec(layer, tm, 3, **once), mod.spec(layer, tm, 4, **once),
            wspec,
        ],
        out_specs=out_specs,
        out_shape=out_shape,
        scratch_shapes=[pltpu.VMEM((tm, D_MODEL), BF16)],
        compiler_params=_cparams("parallel", "arbitrary"),
        name="inproj_cast" if emit else "inproj",
    )(x, norm_g3, mod.arr, mod.arr, w_in)
    return tuple(outs) if emit else outs[0]


def _outproj_body(x_ref, oa_ref, ob_ref, gt_ref, wa_ref, wb_ref, o_ref):
    acc = _dot(oa_ref[...], wa_ref[...]) + _dot(ob_ref[...], wb_ref[...])
    o_ref[...] = x_ref[...] + gt_ref[...] * acc


def _outproj(x, oa, ob, mod, w_out, layer):
    m = x.shape[0]
    tm = min(TM_ROWS, m)
    return pl.pallas_call(
        _outproj_body,
        grid=(m // tm,),
        in_specs=[
            pl.BlockSpec((tm, D_MODEL), lambda i: (i, 0)),
            pl.BlockSpec((tm, W_A), lambda i: (i, 0)),
            pl.BlockSpec((tm, W_B), lambda i: (i, 0)),
            mod.spec(layer, tm, 5),
            pl.BlockSpec((None, W_A, D_MODEL), lambda i: (layer, 0, 0)),
            pl.BlockSpec((None, W_B, D_MODEL), lambda i: (layer, 1, 0)),
        ],
        out_specs=pl.BlockSpec((tm, D_MODEL), lambda i: (i, 0)),
        out_shape=jax.ShapeDtypeStruct((m, D_MODEL), F32),
        compiler_params=_cparams("parallel"),
        name="outproj",
    )(x, oa, ob, mod.arr, w_out, w_out)


def _lru_body(x0_ref, x1_ref, y0_ref, y1_ref, *refs, seq_len, has_state):
    if has_state:
        (pe_ref, h0_ref, cw_ref, cb_ref, wa_ref, wx_ref, ba_ref, bx_ref, lam_ref,
         ob_ref, hl_ref, a_scr, b_scr) = refs
    else:
        (cw_ref, cb_ref, wa_ref, wx_ref, ba_ref, bx_ref, lam_ref,
         ob_ref, hl_ref, a_scr, b_scr) = refs
    rows = x0_ref.shape[0]
    x = jnp.concatenate([x0_ref[:, LRU_SHIFT:], x1_ref[:, :LRU_SHIFT]], axis=1)
    y = jnp.concatenate([y0_ref[:, LRU_SHIFT:], y1_ref[:, :LRU_SHIFT]], axis=1)
    if has_state:
        xc = _conv_rows(x, pe_ref[...], cw_ref[...], _row_iota(x.shape, seq_len))
    else:
        a_scr[...] = x
        xc = _conv_zero_past(a_scr, cw_ref[...])
    xc = xc + cb_ref[...]
    xcb = xc.astype(BF16)
    r = jax.nn.sigmoid(_dot(xcb, wa_ref[...]) + ba_ref[...])
    i = jax.nn.sigmoid(_dot(xcb, wx_ref[...]) + bx_ref[...])
    log_a = -LRU_C * r * jax.nn.softplus(-lam_ref[...])
    a = jnp.exp(log_a)
    b = jnp.sqrt(-_expm1(2.0 * log_a)) * (i * xc)
    if has_state:
        b = b + a * h0_ref[...]
    group = min(seq_len, SUBLANES)
    tg = _row_iota(x.shape, group)
    d = 1
    while d < group:
        keep = tg >= d
        b = jnp.where(keep, a * pltpu.roll(b, d, axis=0) + b, b)
        a = jnp.where(keep, a * pltpu.roll(a, d, axis=0), a)
        d *= 2
    if seq_len > SUBLANES:
        a_scr[...] = a
        b_scr[...] = b

        def step(k, carry):
            r0 = pl.multiple_of(k * SUBLANES, SUBLANES)
            hk = b_scr[pl.ds(r0, SUBLANES), :] + a_scr[pl.ds(r0, SUBLANES), :] * carry
            b_scr[pl.ds(r0, SUBLANES), :] = hk
            return hk[SUBLANES - 1:SUBLANES, :]

        lax.fori_loop(0, rows // SUBLANES, step, jnp.zeros((1, LRU_BW), F32), unroll=8)
        h = b_scr[...]
    else:
        h = b
    ob_ref[...] = (h * jax.nn.gelu(y)).astype(BF16)
    hl_ref[...] = h[rows - hl_ref.shape[0]:, :]


def _lru(proj, seq_len, layer, lconv_w, lconv_b, wa, wx, ba, bx, lam, pe=None, h0e=None):
    m = proj.shape[0]
    has_state = pe is not None
    rows = seq_len if not has_state else m
    nseq = m // rows
    hl_rows = rows if has_state else SUBLANES
    bx0, by0 = COL_X // LRU_BW, COL_Y // LRU_BW
    col = lambda b, n: (b, n)
    vec = pl.BlockSpec((None, 1, LRU_BW), lambda b, n: (layer, 0, n))
    blk = pl.BlockSpec((None, None, LRU_BW, LRU_BW), lambda b, n: (layer, n, 0, 0))
    in_specs = [pl.BlockSpec((rows, LRU_BW), lambda b, n: (b, bx0 + n)),
                pl.BlockSpec((rows, LRU_BW), lambda b, n: (b, bx0 + n + 1)),
                pl.BlockSpec((rows, LRU_BW), lambda b, n: (b, by0 + n)),
                pl.BlockSpec((rows, LRU_BW), lambda b, n: (b, by0 + n + 1))]
    args = [proj, proj, proj, proj]
    if has_state:
        in_specs += [pl.BlockSpec((rows, LRU_BW), col), pl.BlockSpec((rows, LRU_BW), col)]
        args += [pe, h0e]
    in_specs += [pl.BlockSpec((None, CONV_W, LRU_BW), lambda b, n: (layer, 0, n)), vec, blk, blk, vec, vec, vec]
    args += [lconv_w, lconv_b, wa, wx, ba, bx, lam]
    return pl.pallas_call(
        functools.partial(_lru_body, seq_len=seq_len, has_state=has_state),
        grid=(nseq, LRU_BLOCKS),
        in_specs=in_specs,
        out_specs=[pl.BlockSpec((rows, LRU_BW), col), pl.BlockSpec((hl_rows, LRU_BW), col)],
        out_shape=[jax.ShapeDtypeStruct((m, W_B), BF16), jax.ShapeDtypeStruct((nseq * hl_rows, W_B), F32)],
        scratch_shapes=[pltpu.VMEM((rows, LRU_BW), F32), pltpu.VMEM((rows, LRU_BW), F32)],
        compiler_params=_cparams("parallel", "parallel"),
        name="lru",
    )(*args)


def _qkv_prep(q_conv, k_conv, v_conv, heads):
    q, k, v = _silu(q_conv), _silu(k_conv), _silu(v_conv)
    qs, ks = [], []
    for h in range(heads):
        sl = slice(h * DK, (h + 1) * DK)
        qh, kh = q[:, sl], k[:, sl]
        qs.append(qh * (lax.rsqrt(jnp.sum(qh * qh, axis=-1, keepdims=True) + EPS) * DK ** -0.5))
        ks.append(kh * lax.rsqrt(jnp.sum(kh * kh, axis=-1, keepdims=True) + EPS))
    return jnp.concatenate(qs, axis=1), jnp.concatenate(ks, axis=1), v


def _gate_norm(o, onorm, gate_pre):
    var = jnp.mean(o * o, axis=-1, keepdims=True)
    return o * lax.rsqrt(var + EPS) * onorm * _silu(gate_pre)


def _seg_cumsum(g, tg, group):
    d = 1
    while d < group:
        g = jnp.where(tg >= d, g + pltpu.roll(g, d, axis=0), g)
        d *= 2
    return g


def _delta_prompt_body(q_ref, k_ref, v_ref, gt_ref, ab_ref, cwq_ref, cwk_ref, cwv_ref, alog_ref, dtb_ref,
                       on_ref, oa_ref, s_ref, qn_scr, kn_scr, vv_scr, gc_scr, bt_scr, s_scr,
                       wq_scr, uc_scr, qk_scr, kdt_scr, sd_scr):
    rows = q_ref.shape[0]
    hg = q_ref.shape[1] // DK
    qn, kn, vv = _qkv_prep(_conv_zero_past(q_ref, cwq_ref[...]), _conv_zero_past(k_ref, cwk_ref[...]),
                           _conv_zero_past(v_ref, cwv_ref[...]), hg)
    qn_scr[...] = qn
    kn_scr[...] = kn
    vv_scr[...] = vv
    ab = ab_ref[...]
    g = -jnp.exp(alog_ref[...]) * jax.nn.softplus(ab + dtb_ref[...])
    gc_scr[...] = _seg_cumsum(g, _row_iota(ab.shape, CHUNK), CHUNK)
    bt_scr[...] = jax.nn.sigmoid(ab)
    s_scr[...] = jnp.zeros_like(s_scr)
    head0 = pl.program_id(1) * hg
    onorm = on_ref[...]

    def factor(step, carry):
        lane = lax.broadcasted_iota(jnp.int32, (CHUNK, LANES), 1)
        ri = lax.broadcasted_iota(jnp.int32, (CHUNK, CHUNK), 0)
        ci = lax.broadcasted_iota(jnp.int32, (CHUNK, CHUNK), 1)
        causal = ri >= ci
        strict = ri > ci
        blk = jnp.bitwise_xor(ri, ci)
        chains = []
        for cc in range(FACTOR_CHUNKS):
            c = step * FACTOR_CHUNKS + cc
            rs = pl.ds(pl.multiple_of(c * CHUNK, CHUNK), CHUNK)
            gcc = gc_scr[rs, :]
            btc = bt_scr[rs, :]
            for hh in range(hg):
                sl = slice(hh * DK, (hh + 1) * DK)
                gcol = jnp.sum(jnp.where(lane == head0 + hh, gcc, 0.0), axis=1, keepdims=True)
                bcol = jnp.sum(jnp.where(lane == head0 + hh + H_A, btc, 0.0), axis=1, keepdims=True)
                gmat = jnp.broadcast_to(gcol, (CHUNK, CHUNK))
                decay = jnp.where(causal, jnp.exp(jnp.where(causal, gmat - gmat.T, 0.0)), 0.0)
                q = qn_scr[rs, sl]
                k = kn_scr[rs, sl]
                kq = _dot_nt(jnp.concatenate([k, q], axis=0).astype(BF16), k.astype(BF16))
                m = jnp.where(strict, bcol * kq[:CHUNK] * decay, 0.0)
                eg = jnp.exp(gcol)
                glast = jnp.broadcast_to(gcol[CHUNK - 1:CHUNK, :], (CHUNK, 1))
                wq_scr[hh, c, CHUNK:, :] = (q * eg).astype(BF16)
                qk_scr[hh, rs, :] = (kq[CHUNK:] * decay).astype(BF16)
                kdt_scr[hh, rs, :] = (k * jnp.exp(glast - gcol)).T.astype(BF16)
                sd_scr[hh, pl.ds(pl.multiple_of(c * SUBLANES, SUBLANES), SUBLANES), :] = jnp.broadcast_to(
                    jnp.broadcast_to(eg, (CHUNK, DV))[CHUNK - 1:CHUNK, :], (SUBLANES, DV))
                chains.append((hh, c, rs, sl, m, bcol, eg))
        es = [-jnp.where(lax.shift_right_logical(blk, 1) == 0, ch[4], 0.0) for ch in chains]
        lg = 1
        while (1 << lg) < CHUNK:
            cross = lax.shift_right_logical(blk, lg) == 1
            offs = [jnp.where(cross, ch[4], 0.0) for ch in chains]
            ebs = [e.astype(BF16) for e in es]
            xs = [off + _dot(eb, off.astype(BF16)) for off, eb in zip(offs, ebs)]
            es = [e - (x + _dot(x.astype(BF16), eb)) for e, x, eb in zip(es, xs, ebs)]
            lg += 1
        for (hh, c, rs, sl, m, bcol, eg), e in zip(chains, es):
            rhs = jnp.concatenate([kn_scr[rs, sl] * (bcol * eg), vv_scr[rs, sl] * bcol], axis=1)
            sol = rhs + _dot(e.astype(BF16), rhs.astype(BF16))
            wq_scr[hh, c, :CHUNK, :] = sol[:, :DK].astype(BF16)
            uc_scr[hh, rs, :] = sol[:, DK:]
        return carry

    lax.fori_loop(0, rows // (CHUNK * FACTOR_CHUNKS), factor, 0)

    def recur(c, carry):
        rs = pl.ds(pl.multiple_of(c * CHUNK, CHUNK), CHUNK)
        s_old = [s_scr[hh] for hh in range(hg)]
        ps = [_dot(wq_scr[hh, c], s_old[hh].astype(BF16)) for hh in range(hg)]
        us = [(uc_scr[hh, rs, :] - ps[hh][:CHUNK]).astype(BF16) for hh in range(hg)]
        os_ = [ps[hh][CHUNK:] + _dot(qk_scr[hh, rs, :], us[hh]) for hh in range(hg)]
        for hh in range(hg):
            sdec = sd_scr[hh, pl.ds(pl.multiple_of(c * SUBLANES, SUBLANES), SUBLANES), :]
            s_scr[hh] = s_old[hh] * sdec[:1, :] + _dot(kdt_scr[hh, rs, :], us[hh])
        for hh in range(hg):
            sl = slice(hh * DK, (hh + 1) * DK)
            oa_ref[rs, sl] = _gate_norm(os_[hh], onorm, gt_ref[rs, sl]).astype(BF16)
        return carry

    lax.fori_loop(0, rows // CHUNK, recur, 0)
    s_ref[...] = s_scr[...]


def _delta_prompt(proj, batch, seq_len, layer, dconv_w, alog_row, dtb_row, onorm):
    m = proj.shape[0]
    hg = HG_PROMPT
    wcols = hg * DK
    nchunks = seq_len // CHUNK
    nq, nk, nv, ng = (c // wcols for c in (COL_Q, COL_K, COL_V, COL_G))
    colspec = lambda off: pl.BlockSpec((seq_len, wcols), lambda b, j: (b, off + j))
    cwspec = lambda off: pl.BlockSpec((None, CONV_W, wcols), lambda b, j: (layer, 0, off + j))
    row = pl.BlockSpec((None, 1, LANES), lambda b, j: (layer, 0, 0))
    return pl.pallas_call(
        _delta_prompt_body,
        grid=(batch, H_A // hg),
        in_specs=[colspec(nq), colspec(nk), colspec(nv), colspec(ng),
                  pl.BlockSpec((seq_len, LANES), lambda b, j: (b, COL_AB // LANES)),
                  cwspec(nq), cwspec(nk), cwspec(nv), row, row, row],
        out_specs=[pl.BlockSpec((seq_len, wcols), lambda b, j: (b, j)),
                   pl.BlockSpec((None, hg, DK, DV), lambda b, j: (b, j, 0, 0))],
        out_shape=[jax.ShapeDtypeStruct((m, W_A), BF16), jax.ShapeDtypeStruct((batch, H_A, DK, DV), F32)],
        scratch_shapes=[pltpu.VMEM((seq_len, wcols), F32)] * 3
        + [pltpu.VMEM((seq_len, LANES), F32)] * 2 + [pltpu.VMEM((hg, DK, DV), F32)]
        + [pltpu.VMEM((hg, nchunks, 2 * CHUNK, DK), BF16), pltpu.VMEM((hg, seq_len, DV), F32),
           pltpu.VMEM((hg, seq_len, CHUNK), BF16), pltpu.VMEM((hg, seq_len, CHUNK), BF16),
           pltpu.VMEM((hg, nchunks * SUBLANES, DV), F32)],
        compiler_params=_cparams("parallel", "parallel"),
        name="delta_prompt",
    )(proj, proj, proj, proj, proj, dconv_w, dconv_w, dconv_w, alog_row, dtb_row, onorm)


def _delta_sample_body(q_ref, k_ref, v_ref, gt_ref, ab_ref, pe_ref, cw_ref, alog_ref, dtb_ref, on_ref, s0_ref,
                       *rest, seq_len):
    oa_ref, s_ref, qn_scr, kn_scr, vv_scr, gc_scr, bt_scr = rest[-7:]
    rows = q_ref.shape[0]
    t = _row_iota(q_ref.shape, seq_len)
    pe = pe_ref[...]
    cw = cw_ref[...]
    qn, kn, vv = _qkv_prep(_conv_rows(q_ref[...], pe[:, COL_Q:COL_K], cw[:, COL_Q:COL_K], t),
                           _conv_rows(k_ref[...], pe[:, COL_K:COL_V], cw[:, COL_K:COL_V], t),
                           _conv_rows(v_ref[...], pe[:, COL_V:COL_G], cw[:, COL_V:COL_G], t), H_A)
    qn_scr[...] = qn
    kn_scr[...] = kn
    vv_scr[...] = vv
    ab = ab_ref[...]
    g = -jnp.exp(alog_ref[...]) * jax.nn.softplus(ab + dtb_ref[...])
    gc_scr[...] = _seg_cumsum(g, _row_iota(ab.shape, seq_len), seq_len)
    bt_scr[...] = jax.nn.sigmoid(ab)
    onorm = on_ref[...]
    per_tile = SUBLANES // seq_len

    def tile(p, carry):
        r0 = pl.multiple_of(p * SUBLANES, SUBLANES)
        rs = pl.ds(r0, SUBLANES)
        gcc = gc_scr[rs, :]
        btc = bt_scr[rs, :]
        tt = _row_iota((SUBLANES, LANES), seq_len)
        tt2 = _row_iota((SUBLANES, DK + DV), seq_len)
        ri = lax.broadcasted_iota(jnp.int32, (SUBLANES, LANES), 0)
        ri2 = _row_iota((2 * SUBLANES, LANES), SUBLANES)
        lane = lax.broadcasted_iota(jnp.int32, (LANES, LANES), 1)
        zpad = jnp.zeros((LANES - SUBLANES, DK), F32)
        staged = []
        for h in range(H_A):
            sl = slice(h * DK, (h + 1) * DK)
            gc = jnp.broadcast_to(gcc[:, h:h + 1], (SUBLANES, LANES))
            beta = jnp.broadcast_to(btc[:, H_A + h:H_A + h + 1], (SUBLANES, LANES))
            q = qn_scr[rs, sl]
            k = kn_scr[rs, sl]
            v = vv_scr[rs, sl]
            eg = jnp.exp(gc)
            decs, ms = [None], [None]
            for d in range(1, seq_len):
                ok = tt >= d
                dec = jnp.where(ok, jnp.exp(jnp.where(ok, gc - pltpu.roll(gc, d, axis=0), 0.0)), 0.0)
                kk = jnp.sum(k * pltpu.roll(k, d, axis=0), axis=-1, keepdims=True)
                decs.append(dec)
                md = beta * kk * dec
                ms.append(jnp.concatenate([md, md], axis=1))
            rhs = jnp.concatenate([k * (beta * eg), v * beta], axis=1)
            sol = rhs
            for step in range(1, seq_len):
                acc = ms[1] * pltpu.roll(sol, 1, axis=0)
                for d in range(2, seq_len):
                    acc = acc + ms[d] * pltpu.roll(sol, d, axis=0)
                sol = jnp.where(tt2 == step, rhs - acc, sol)
            lhs = jnp.concatenate([sol[:, :DK], q * eg], axis=0).astype(BF16)
            staged.append((sl, gc, q, k, eg, decs, sol[:, DK:], lhs))
        pps = []
        for h in range(H_A):
            lhs = staged[h][-1]
            pp = _dot(lhs, s0_ref[p * per_tile, h].astype(BF16))
            for bb in range(1, per_tile):
                pp = jnp.where(ri2 >= bb * seq_len, _dot(lhs, s0_ref[p * per_tile + bb, h].astype(BF16)), pp)
            pps.append(pp)
        updates = []
        for h in range(H_A):
            sl, gc, q, k, eg, decs, uc, _ = staged[h]
            pp = pps[h]
            u = uc - pp[:SUBLANES]
            o = pp[SUBLANES:] + jnp.sum(q * k, axis=-1, keepdims=True) * u
            for d in range(1, seq_len):
                qk = jnp.sum(q * pltpu.roll(k, d, axis=0), axis=-1, keepdims=True)
                o = o + (qk * decs[d]) * pltpu.roll(u, d, axis=0)
            oa_ref[rs, sl] = _gate_norm(o, onorm, gt_ref[rs, sl]).astype(BF16)
            glast = jnp.broadcast_to(gc[seq_len - 1:seq_len, :], (SUBLANES, LANES))
            for bb in range(1, per_tile):
                last = bb * seq_len + seq_len - 1
                glast = jnp.where(ri >= bb * seq_len, jnp.broadcast_to(gc[last:last + 1, :], (SUBLANES, LANES)), glast)
            kd = k * jnp.exp(glast - gc)
            kdt = jnp.concatenate([kd, zpad], axis=0).T
            updates.append((kdt, jnp.concatenate([u, zpad], axis=0).astype(BF16)))
        for h in range(H_A):
            kdt, upad = updates[h]
            eg = staged[h][4]
            for bb in range(per_tile):
                last = bb * seq_len + seq_len - 1
                cols = (lane >= bb * seq_len) & (lane < (bb + 1) * seq_len)
                kd_b = jnp.where(cols, kdt, 0.0).astype(BF16)
                s_ref[p * per_tile + bb, h] = (s0_ref[p * per_tile + bb, h] * eg[last:last + 1, :]
                                               + _dot(kd_b, upad))
        return carry

    lax.fori_loop(0, rows // SUBLANES, tile, 0)


def _delta_sample(proj, seq_len, layer, pe, dconv_w, alog_row, dtb_row, onorm, s0, s_stack):
    m = proj.shape[0]
    tb = TB_SAMPLE
    rows = tb * seq_len
    colspec = lambda off: pl.BlockSpec((rows, W_A), lambda i: (i, off // W_A))
    row = pl.BlockSpec((None, 1, LANES), lambda i: (layer, 0, 0))
    sspec = pl.BlockSpec((None, tb, H_A, DK, DV), lambda i: (layer, i, 0, 0, 0))
    in_specs = [colspec(COL_Q), colspec(COL_K), colspec(COL_V), colspec(COL_G),
                pl.BlockSpec((rows, LANES), lambda i: (i, COL_AB // LANES)),
                pl.BlockSpec((rows, N_DCONV), lambda i: (i, 0)),
                pl.BlockSpec((None, CONV_W, N_DCONV), lambda i: (layer, 0, 0)),
                row, row, row, sspec]
    args = [proj, proj, proj, proj, proj, pe, dconv_w, alog_row, dtb_row, onorm, s0]
    aliases = {}
    if s_stack is not None:
        in_specs.append(pl.BlockSpec(memory_space=pl.ANY))
        aliases = {len(args): 1}
        args.append(s_stack)
    return pl.pallas_call(
        functools.partial(_delta_sample_body, seq_len=seq_len),
        grid=(m // rows,),
        in_specs=in_specs,
        out_specs=[pl.BlockSpec((rows, W_A), lambda i: (i, 0)), sspec],
        out_shape=[jax.ShapeDtypeStruct((m, W_A), BF16), jax.ShapeDtypeStruct(s0.shape, F32)],
        input_output_aliases=aliases,
        scratch_shapes=[pltpu.VMEM((rows, W_A), F32)] * 3 + [pltpu.VMEM((rows, LANES), F32)] * 2,
        compiler_params=_cparams("parallel"),
        name="delta_sample",
    )(*args)


def _pad_rows(state, seq_len):
    b, r, c = state.shape
    return jnp.pad(state, ((0, 0), (0, seq_len - r), (0, 0))).reshape(b * seq_len, c)


def _trunk(x, mod, batch, seq_len, wts, states, ffn_cast):
    (w_in, w_out, norm_g3, ffn_f32, dconv_w, alog_row, dtb_row, onorm, lconv_w, lconv_b,
     wa, wx, ba, bx, lam, final_g) = wts
    new_d, new_dc, new_l, new_lc = [], [], [], []
    s_stack = None

    def ffn(x, layer, s, fg=None):
        if (layer, s) in ffn_cast:
            return _ffn(x, mod, norm_g3, *ffn_cast[(layer, s)], layer, s, fg)
        y, w13, w2 = _ffn(x, mod, norm_g3, ffn_f32[:2], ffn_f32[2], layer, s, fg, emit=True)
        ffn_cast[(layer, s)] = (w13, w2)
        return y

    for layer in range(DEPTH):
        x = ffn(x, layer, 0)
        if ("in", layer) in ffn_cast:
            proj = _inproj(x, mod, norm_g3, ffn_cast[("in", layer)], layer)
        else:
            proj, ffn_cast[("in", layer)] = _inproj(x, mod, norm_g3, w_in, layer, emit=True)
        proj3 = proj.reshape(batch, seq_len, N_PROJ)
        if states is None:
            oa, sd = _delta_prompt(proj, batch, seq_len, layer, dconv_w, alog_row, dtb_row, onorm)
            new_d.append(sd)
            ob, hl = _lru(proj, seq_len, layer, lconv_w, lconv_b, wa, wx, ba, bx, lam)
            sl = hl.reshape(batch, SUBLANES, W_B)[:, SUBLANES - 1]
        else:
            s_delta, s_dconv, s_lru, s_lconv = states
            oa, s_stack = _delta_sample(proj, seq_len, layer, _pad_rows(s_dconv[layer], seq_len),
                                        dconv_w, alog_row, dtb_row, onorm, s_delta, s_stack)
            ob, hl = _lru(proj, seq_len, layer, lconv_w, lconv_b, wa, wx, ba, bx, lam,
                          pe=_pad_rows(s_lconv[layer], seq_len),
                          h0e=_pad_rows(s_lru[layer][:, None, :], seq_len))
            sl = hl.reshape(batch, seq_len, W_B)[:, seq_len - 1]
        x = _outproj(x, oa, ob, mod, w_out, layer)
        x = ffn(x, layer, 1, final_g if layer == DEPTH - 1 else None)
        new_dc.append(proj3[:, seq_len - (CONV_W - 1):, COL_Q:COL_G])
        new_l.append(sl)
        new_lc.append(proj3[:, seq_len - (CONV_W - 1):, COL_X:COL_Y])
    new_delta = jnp.stack(new_d) if states is None else s_stack
    return x, new_delta, jnp.stack(new_dc), jnp.stack(new_l), jnp.stack(new_lc)


def kernel(x_prompt, x_sample, c_prompt, c_sample, state_delta, state_delta_conv, state_lru, state_lru_conv, w_in, w_out, norm_g, w_ada, b_ada, ffn_w1, ffn_w3, ffn_w2, dconv_w, d_alog, d_dtbias, d_onorm, lconv_w, lconv_b, lru_wa, lru_ba, lru_wx, lru_bx, lru_lam, final_g):
    bp, lp, _ = x_prompt.shape
    bs, ls, _ = x_sample.shape
    assert lp % CHUNK == 0 and SUBLANES % ls == 0 and bp <= SUBLANES

    w_out_b = w_out.astype(BF16)
    wa = lru_wa.astype(BF16)
    wx = lru_wx.astype(BF16)
    norm_g3 = norm_g.reshape(DEPTH * 3, 1, D_MODEL)
    lane_row = lambda v: jnp.pad(v, ((0, 0), (0, LANES - v.shape[1]))).reshape(DEPTH, 1, LANES)
    vec = lambda v: v.reshape(DEPTH, 1, W_B)
    wts = (w_in, w_out_b, norm_g3, (ffn_w1, ffn_w3, ffn_w2), dconv_w, lane_row(d_alog), lane_row(d_dtbias),
           d_onorm.reshape(DEPTH, 1, DV), lconv_w, vec(lconv_b), wa, wx, vec(lru_ba), vec(lru_bx),
           vec(lru_lam), final_g)

    ms = bs * ls
    c_all = jnp.concatenate([jnp.repeat(c_sample, ls, axis=0), c_prompt,
                             jnp.zeros((SUBLANES - bp, D_MODEL), F32)], axis=0)
    mod_all = _ada(c_all, w_ada, b_ada)
    mod_p = _Mod(mod_all[:, ms:ms + bp].reshape(DEPTH, bp, 1, 9 * D_MODEL), False, lp)
    mod_s = _Mod(mod_all, True, ls)

    ffn_cast = {}
    ys, sd, sdc, sl_, slc = _trunk(x_sample.reshape(bs * ls, D_MODEL), mod_s, bs, ls, wts,
                                   (state_delta, state_delta_conv, state_lru, state_lru_conv), ffn_cast)
    yp, pd, pdc, pl_, plc = _trunk(x_prompt.reshape(bp * lp, D_MODEL), mod_p, bp, lp, wts, None, ffn_cast)
    return (yp.reshape(bp, lp, D_MODEL), ys.reshape(bs, ls, D_MODEL), pd, pdc, pl_, plc, sd, sdc, sl_, slc)
```

```python
import functools

import jax
import jax.numpy as jnp
from jax import lax
from jax.experimental import pallas as pl
from jax.experimental.pallas import tpu as pltpu

F32 = jnp.float32
BF16 = jnp.bfloat16

D_MODEL = 2048
DEPTH = 4
H_A = 8
DK = 128
DV = 128
W_QK = H_A * DK
W_A = H_A * DV
W_B = D_MODEL - W_A
LRU_BLOCKS = 8
LRU_BW = W_B // LRU_BLOCKS
LRU_C = 8.0
CONV_W = 4
D_FF = ((8 * D_MODEL // 3 + 127) // 128) * 128
HALF = 0.5
EPS = 1e-6
N_DCONV = 2 * W_QK + W_A
OFF_G = 2 * W_QK + W_A
OFF_A = OFF_G + W_A
OFF_X = OFF_A + 2 * H_A
N_IN = OFF_X + 2 * W_B

LANES = 128
SUBLANES = 8
VMEM_LIMIT = 56 * 1024 * 1024

COL_Q, COL_K, COL_V, COL_G = 0, W_QK, 2 * W_QK, 2 * W_QK + W_A
COL_X = COL_G + W_A
COL_Y = COL_X + W_B
COL_AB = COL_Y + W_B
N_PROJ = COL_AB + 2 * LANES

TM_ROWS = 512
TM_PROJ = 1024
TF_FFN = 512
TF_EMIT = 256
TN_PROJ = 1280
TN_ADA = 1024
CHUNK = 128
HG_PROMPT = 2
FACTOR_CHUNKS = 8
TB_SAMPLE = 16

assert N_PROJ % TN_PROJ == 0 and (9 * D_MODEL) % TN_ADA == 0 and D_FF % LANES == 0


def _cparams(*sem):
    return pltpu.CompilerParams(dimension_semantics=sem, vmem_limit_bytes=VMEM_LIMIT)


def _dot(a, b):
    return jnp.dot(a, b, preferred_element_type=F32)


def _dot_nt(a, b):
    return lax.dot_general(a, b, (((1,), (1,)), ((), ())), preferred_element_type=F32)


def _silu(x):
    return x * jax.nn.sigmoid(x)


def _expm1(x):
    u = jnp.exp(x)
    um1 = u - 1.0
    safe = um1 * x / jnp.log(jnp.where(um1 == 0.0, 2.0, u))
    return jnp.where(um1 == 0.0, x, jnp.where(x < -30.0, um1, safe))


def _norm_mod(x, g, shift, scale):
    var = jnp.mean(x * x, axis=-1, keepdims=True)
    return x * lax.rsqrt(var + EPS) * (g * (1.0 + scale)) + shift


def _row_iota(shape, mod):
    return jnp.bitwise_and(lax.broadcasted_iota(jnp.int32, shape, 0), mod - 1)


def _conv_zero_past(x_ref, w):
    rows = x_ref.shape[0]
    head = x_ref[:SUBLANES, :]
    head_y = _conv_rows(head, None, w, lax.broadcasted_iota(jnp.int32, head.shape, 0))
    y = x_ref[SUBLANES:, :] * w[CONV_W - 1:CONV_W, :]
    for j in range(1, CONV_W):
        y = y + x_ref[SUBLANES - j:rows - j, :] * w[CONV_W - 1 - j:CONV_W - j, :]
    return jnp.concatenate([head_y, y], axis=0)


def _conv_rows(x, pe, w, t):
    rows = x.shape[0]
    y = x * w[CONV_W - 1:CONV_W, :]
    for j in range(1, CONV_W):
        term = pltpu.roll(x, j, axis=0)
        if pe is None:
            term = jnp.where(t >= j, term, 0.0)
        else:
            back = CONV_W - 1 - j
            prev = pe if back == 0 else pltpu.roll(pe, rows - back, axis=0)
            term = jnp.where(t >= j, term, prev)
        y = y + term * w[CONV_W - 1 - j:CONV_W - j, :]
    return y


def _ada_body(c_ref, w_ref, b_ref, o_ref, *, nseq, reps):
    cs = _silu(c_ref[...]).astype(BF16)
    m = _dot(cs, w_ref[...].astype(BF16)) + b_ref[...]
    hi = m[:nseq].astype(BF16)
    r1 = m[:nseq] - hi.astype(F32)
    mid = r1.astype(BF16)
    lo = (r1 - mid.astype(F32)).astype(BF16)
    rows = nseq * reps
    src = lax.broadcasted_iota(jnp.int32, (rows, 3 * nseq), 1)
    dst = lax.div(lax.broadcasted_iota(jnp.int32, (rows, 3 * nseq), 0), reps)
    hit = (src == dst) | (src == dst + nseq) | (src == dst + 2 * nseq)
    pick = jnp.where(hit, 1.0, 0.0).astype(BF16)
    o_ref[:rows, :] = _dot(pick, jnp.concatenate([hi, mid, lo], axis=0))
    o_ref[rows:, :] = m[nseq:]


def _ada(c_all, w_ada, b_ada, nseq, reps):
    rows = c_all.shape[0]
    out_rows = nseq * reps + rows - nseq
    n9 = 9 * D_MODEL
    return pl.pallas_call(
        functools.partial(_ada_body, nseq=nseq, reps=reps),
        grid=(DEPTH, n9 // TN_ADA),
        in_specs=[
            pl.BlockSpec((rows, D_MODEL), lambda l, n: (0, 0)),
            pl.BlockSpec((None, D_MODEL, TN_ADA), lambda l, n: (l, 0, n)),
            pl.BlockSpec((None, 1, TN_ADA), lambda l, n: (l, 0, n)),
        ],
        out_specs=pl.BlockSpec((None, out_rows, TN_ADA), lambda l, n: (l, 0, n)),
        out_shape=jax.ShapeDtypeStruct((DEPTH, out_rows, n9), F32),
        compiler_params=_cparams("parallel", "parallel"),
        name="ada",
    )(c_all, w_ada, b_ada.reshape(DEPTH, 1, n9))


class _Mod:
    def __init__(self, arr, per_token, seq_len):
        self.arr = arr
        self.per_token = per_token
        self.seq_len = seq_len

    def spec(self, layer, tm, k, **kw):
        if self.per_token:
            return pl.BlockSpec((None, tm, D_MODEL), lambda i, *_: (layer, i, k), **kw)
        seq_len = self.seq_len
        return pl.BlockSpec((None, None, 1, D_MODEL), lambda i, *_: (layer, (i * tm) // seq_len, 0, k), **kw)


def _ffn_body(x_ref, g_ref, sh_ref, sc_ref, gt_ref, w1_ref, w3_ref, w2_ref, *rest, final, emit, tf):
    rest = list(rest)
    fg_ref = rest.pop(0) if final else None
    o_ref = rest.pop(0)
    if emit:
        w1b_ref, w3b_ref, w2b_ref = rest[:3]
        rest = rest[3:]
        w1b_ref[...] = w1_ref[...].astype(BF16)
        w3b_ref[...] = w3_ref[...].astype(BF16)
        w2b_ref[...] = w2_ref[...].astype(BF16)
        w1_ref, w3_ref, w2_ref = w1b_ref, w3b_ref, w2b_ref
    h_scr, acc_scr = rest
    f = pl.program_id(1)

    @pl.when(f == 0)
    def _():
        h = _norm_mod(x_ref[...], g_ref[...], sh_ref[...], sc_ref[...])
        h_scr[...] = h.astype(BF16)
        acc_scr[...] = jnp.zeros_like(acc_scr)

    def accumulate(width):
        h = h_scr[...]
        a = _dot(h, w1_ref[:, :width])
        b = _dot(h, w3_ref[:, :width])
        acc_scr[...] += _dot((_silu(a) * b).astype(BF16), w2_ref[:width, :])

    last = pl.num_programs(1) - 1
    tail = D_FF - (-(-D_FF // tf) - 1) * tf
    if tail == tf:
        accumulate(tf)
    else:
        pl.when(f < last)(lambda: accumulate(tf))
        pl.when(f == last)(lambda: accumulate(tail))

    @pl.when(f == last)
    def _():
        y = x_ref[...] + HALF * gt_ref[...] * acc_scr[...]
        if final:
            var = jnp.mean(y * y, axis=-1, keepdims=True)
            y = y * lax.rsqrt(var + EPS) * fg_ref[...]
        o_ref[...] = y


def _ffn(x, mod, norm_g3, w13, w2, layer, s, final_g=None, emit=False):
    m = x.shape[0]
    tm = min(TM_ROWS, m)
    tf = TF_EMIT if emit else TF_FFN
    steps = -(-D_FF // tf)
    sub = 2 * s
    final = final_g is not None
    w1, w3 = w13
    if emit:
        assert m == tm
        once = dict(pipeline_mode=pl.Buffered(1))
        up = pl.BlockSpec((None, None, D_MODEL, tf), lambda i, f: (layer, s, 0, f))
        down = pl.BlockSpec((None, None, tf, D_MODEL), lambda i, f: (layer, s, f, 0))
    else:
        once = {}
        up = pl.BlockSpec((D_MODEL, tf), lambda i, f: (0, f))
        down = pl.BlockSpec((tf, D_MODEL), lambda i, f: (f, 0))
    in_specs = [
        pl.BlockSpec((tm, D_MODEL), lambda i, f: (i, 0), **once),
        pl.BlockSpec((None, 1, D_MODEL), lambda i, f: (layer * 3 + sub, 0, 0)),
        mod.spec(layer, tm, 3 * sub, **once), mod.spec(layer, tm, 3 * sub + 1, **once),
        mod.spec(layer, tm, 3 * sub + 2, **once),
        up, up, down,
    ]
    args = [x, norm_g3, mod.arr, mod.arr, mod.arr, w1, w3, w2]
    if final:
        in_specs.append(pl.BlockSpec((1, D_MODEL), lambda i, f: (0, 0)))
        args.append(final_g.reshape(1, D_MODEL))
    out_specs = [pl.BlockSpec((tm, D_MODEL), lambda i, f: (i, 0))]
    out_shape = [jax.ShapeDtypeStruct((m, D_MODEL), F32)]
    if emit:
        out_specs += [pl.BlockSpec((D_MODEL, tf), lambda i, f: (0, f))] * 2 + [pl.BlockSpec((tf, D_MODEL), lambda i, f: (f, 0))]
        out_shape += [jax.ShapeDtypeStruct((D_MODEL, D_FF), BF16)] * 2 + [jax.ShapeDtypeStruct((D_FF, D_MODEL), BF16)]
    outs = pl.pallas_call(
        functools.partial(_ffn_body, final=final, emit=emit, tf=tf),
        grid=(m // tm, steps),
        in_specs=in_specs,
        out_specs=out_specs,
        out_shape=out_shape,
        scratch_shapes=[pltpu.VMEM((tm, D_MODEL), BF16), pltpu.VMEM((tm, D_MODEL), F32)],
        compiler_params=_cparams("parallel", "arbitrary"),
        name="ffn_cast" if emit else "ffn",
    )(*args)
    return (outs[0], (outs[1], outs[2]), outs[3]) if emit else outs[0]


def _inproj_body(x_ref, g_ref, sh_ref, sc_ref, w_ref, o_ref, h_scr):
    @pl.when(pl.program_id(1) == 0)
    def _():
        h_scr[...] = _norm_mod(x_ref[...], g_ref[...], sh_ref[...], sc_ref[...]).astype(BF16)

    o_ref[...] = _dot(h_scr[...], w_ref[...])


def _inproj(x, mod, norm_g3, w_in, layer):
    m = x.shape[0]
    tm = min(TM_PROJ, m)
    return pl.pallas_call(
        _inproj_body,
        grid=(m // tm, N_PROJ // TN_PROJ),
        in_specs=[
            pl.BlockSpec((tm, D_MODEL), lambda i, n: (i, 0)),
            pl.BlockSpec((None, 1, D_MODEL), lambda i, n: (layer * 3 + 1, 0, 0)),
            mod.spec(layer, tm, 3), mod.spec(layer, tm, 4),
            pl.BlockSpec((None, D_MODEL, TN_PROJ), lambda i, n: (layer, 0, n)),
        ],
        out_specs=pl.BlockSpec((tm, TN_PROJ), lambda i, n: (i, n)),
        out_shape=jax.ShapeDtypeStruct((m, N_PROJ), F32),
        scratch_shapes=[pltpu.VMEM((tm, D_MODEL), BF16)],
        compiler_params=_cparams("parallel", "arbitrary"),
        name="inproj",
    )(x, norm_g3, mod.arr, mod.arr, w_in)


def _outproj_body(x_ref, oa_ref, ob_ref, gt_ref, wa_ref, wb_ref, o_ref):
    acc = _dot(oa_ref[...], wa_ref[...]) + _dot(ob_ref[...], wb_ref[...])
    o_ref[...] = x_ref[...] + gt_ref[...] * acc


def _outproj(x, oa, ob, mod, w_out, layer):
    m = x.shape[0]
    tm = min(TM_ROWS, m)
    return pl.pallas_call(
        _outproj_body,
        grid=(m // tm,),
        in_specs=[
            pl.BlockSpec((tm, D_MODEL), lambda i: (i, 0)),
            pl.BlockSpec((tm, W_A), lambda i: (i, 0)),
            pl.BlockSpec((tm, W_B), lambda i: (i, 0)),
            mod.spec(layer, tm, 5),
            pl.BlockSpec((None, W_A, D_MODEL), lambda i: (layer, 0, 0)),
            pl.BlockSpec((None, W_B, D_MODEL), lambda i: (layer, 1, 0)),
        ],
        out_specs=pl.BlockSpec((tm, D_MODEL), lambda i: (i, 0)),
        out_shape=jax.ShapeDtypeStruct((m, D_MODEL), F32),
        compiler_params=_cparams("parallel"),
        name="outproj",
    )(x, oa, ob, mod.arr, w_out, w_out)


def _lru_body(*refs, seq_len, has_state):
    if has_state:
        (x_ref, y_ref, pe_ref, h0_ref, cw_ref, cb_ref, wa_ref, wx_ref, ba_ref, bx_ref, lam_ref,
         ob_ref, hl_ref, a_scr, b_scr) = refs
    else:
        (x_ref, y_ref, cw_ref, cb_ref, wa_ref, wx_ref, ba_ref, bx_ref, lam_ref,
         ob_ref, hl_ref, a_scr, b_scr) = refs
    rows = x_ref.shape[0]
    if has_state:
        xc = _conv_rows(x_ref[...], pe_ref[...], cw_ref[...], _row_iota(x_ref.shape, seq_len))
    else:
        xc = _conv_zero_past(x_ref, cw_ref[...])
    xc = xc + cb_ref[...]
    xcb = xc.astype(BF16)
    r = jax.nn.sigmoid(_dot(xcb, wa_ref[...]) + ba_ref[...])
    i = jax.nn.sigmoid(_dot(xcb, wx_ref[...]) + bx_ref[...])
    log_a = -LRU_C * r * jax.nn.softplus(-lam_ref[...])
    a = jnp.exp(log_a)
    b = jnp.sqrt(-_expm1(2.0 * log_a)) * (i * xc)
    if has_state:
        b = b + a * h0_ref[...]
    group = min(seq_len, SUBLANES)
    tg = _row_iota(x_ref.shape, group)
    d = 1
    while d < group:
        keep = tg >= d
        b = jnp.where(keep, a * pltpu.roll(b, d, axis=0) + b, b)
        a = jnp.where(keep, a * pltpu.roll(a, d, axis=0), a)
        d *= 2
    if seq_len > SUBLANES:
        a_scr[...] = a
        b_scr[...] = b

        def step(k, carry):
            r0 = pl.multiple_of(k * SUBLANES, SUBLANES)
            hk = b_scr[pl.ds(r0, SUBLANES), :] + a_scr[pl.ds(r0, SUBLANES), :] * carry
            b_scr[pl.ds(r0, SUBLANES), :] = hk
            return hk[SUBLANES - 1:SUBLANES, :]

        lax.fori_loop(0, rows // SUBLANES, step, jnp.zeros((1, LRU_BW), F32), unroll=8)
        h = b_scr[...]
    else:
        h = b
    ob_ref[...] = (h * jax.nn.gelu(y_ref[...])).astype(BF16)
    hl_ref[...] = h[rows - hl_ref.shape[0]:, :]


def _lru(proj, seq_len, layer, lconv_w, lconv_b, wa, wx, ba, bx, lam, pe=None, h0e=None):
    m = proj.shape[0]
    has_state = pe is not None
    rows = seq_len if not has_state else m
    nseq = m // rows
    hl_rows = rows if has_state else SUBLANES
    bx0, by0 = COL_X // LRU_BW, COL_Y // LRU_BW
    col = lambda b, n: (b, n)
    vec = pl.BlockSpec((None, 1, LRU_BW), lambda b, n: (layer, 0, n))
    blk = pl.BlockSpec((None, None, LRU_BW, LRU_BW), lambda b, n: (layer, n, 0, 0))
    in_specs = [pl.BlockSpec((rows, LRU_BW), lambda b, n: (b, bx0 + n)),
                pl.BlockSpec((rows, LRU_BW), lambda b, n: (b, by0 + n))]
    args = [proj, proj]
    if has_state:
        in_specs += [pl.BlockSpec((rows, LRU_BW), col), pl.BlockSpec((rows, LRU_BW), col)]
        args += [pe, h0e]
    in_specs += [pl.BlockSpec((None, CONV_W, LRU_BW), lambda b, n: (layer, 0, n)), vec, blk, blk, vec, vec, vec]
    args += [lconv_w, lconv_b, wa, wx, ba, bx, lam]
    return pl.pallas_call(
        functools.partial(_lru_body, seq_len=seq_len, has_state=has_state),
        grid=(nseq, LRU_BLOCKS),
        in_specs=in_specs,
        out_specs=[pl.BlockSpec((rows, LRU_BW), col), pl.BlockSpec((hl_rows, LRU_BW), col)],
        out_shape=[jax.ShapeDtypeStruct((m, W_B), BF16), jax.ShapeDtypeStruct((nseq * hl_rows, W_B), F32)],
        scratch_shapes=[pltpu.VMEM((rows, LRU_BW), F32), pltpu.VMEM((rows, LRU_BW), F32)],
        compiler_params=_cparams("parallel", "parallel"),
        name="lru",
    )(*args)


def _qkv_prep(q_conv, k_conv, v_conv, heads):
    q, k, v = _silu(q_conv), _silu(k_conv), _silu(v_conv)
    qs, ks = [], []
    for h in range(heads):
        sl = slice(h * DK, (h + 1) * DK)
        qh, kh = q[:, sl], k[:, sl]
        qs.append(qh * (lax.rsqrt(jnp.sum(qh * qh, axis=-1, keepdims=True) + EPS) * DK ** -0.5))
        ks.append(kh * lax.rsqrt(jnp.sum(kh * kh, axis=-1, keepdims=True) + EPS))
    return jnp.concatenate(qs, axis=1), jnp.concatenate(ks, axis=1), v


def _gate_norm(o, onorm, gate_pre):
    var = jnp.mean(o * o, axis=-1, keepdims=True)
    return o * lax.rsqrt(var + EPS) * onorm * _silu(gate_pre)


def _seg_cumsum(g, tg, group):
    d = 1
    while d < group:
        g = jnp.where(tg >= d, g + pltpu.roll(g, d, axis=0), g)
        d *= 2
    return g


def _delta_prompt_body(q_ref, k_ref, v_ref, gt_ref, ab_ref, cwq_ref, cwk_ref, cwv_ref, alog_ref, dtb_ref,
                       on_ref, oa_ref, s_ref, qn_scr, kn_scr, vv_scr, gc_scr, bt_scr, s_scr,
                       wq_scr, uc_scr, qk_scr, kdt_scr, sd_scr):
    rows = q_ref.shape[0]
    hg = q_ref.shape[1] // DK
    qn, kn, vv = _qkv_prep(_conv_zero_past(q_ref, cwq_ref[...]), _conv_zero_past(k_ref, cwk_ref[...]),
                           _conv_zero_past(v_ref, cwv_ref[...]), hg)
    qn_scr[...] = qn
    kn_scr[...] = kn
    vv_scr[...] = vv
    ab = ab_ref[...]
    g = -jnp.exp(alog_ref[...]) * jax.nn.softplus(ab + dtb_ref[...])
    gc_scr[...] = _seg_cumsum(g, _row_iota(ab.shape, CHUNK), CHUNK)
    bt_scr[...] = jax.nn.sigmoid(ab)
    s_scr[...] = jnp.zeros_like(s_scr)
    head0 = pl.program_id(1) * hg
    onorm = on_ref[...]

    def factor(step, carry):
        lane = lax.broadcasted_iota(jnp.int32, (CHUNK, LANES), 1)
        ri = lax.broadcasted_iota(jnp.int32, (CHUNK, CHUNK), 0)
        ci = lax.broadcasted_iota(jnp.int32, (CHUNK, CHUNK), 1)
        causal = ri >= ci
        strict = ri > ci
        blk = jnp.bitwise_xor(ri, ci)
        chains = []
        for cc in range(FACTOR_CHUNKS):
            c = step * FACTOR_CHUNKS + cc
            rs = pl.ds(pl.multiple_of(c * CHUNK, CHUNK), CHUNK)
            gcc = gc_scr[rs, :]
            btc = bt_scr[rs, :]
            for hh in range(hg):
                sl = slice(hh * DK, (hh + 1) * DK)
                gcol = jnp.sum(jnp.where(lane == head0 + hh, gcc, 0.0), axis=1, keepdims=True)
                bcol = jnp.sum(jnp.where(lane == head0 + hh + H_A, btc, 0.0), axis=1, keepdims=True)
                gmat = jnp.broadcast_to(gcol, (CHUNK, CHUNK))
                decay = jnp.where(causal, jnp.exp(jnp.where(causal, gmat - gmat.T, 0.0)), 0.0)
                q = qn_scr[rs, sl]
                k = kn_scr[rs, sl]
                kq = _dot_nt(jnp.concatenate([k, q], axis=0).astype(BF16), k.astype(BF16))
                m = jnp.where(strict, bcol * kq[:CHUNK] * decay, 0.0)
                eg = jnp.exp(gcol)
                glast = jnp.broadcast_to(gcol[CHUNK - 1:CHUNK, :], (CHUNK, 1))
                wq_scr[hh, c, CHUNK:, :] = (q * eg).astype(BF16)
                qk_scr[hh, rs, :] = (kq[CHUNK:] * decay).astype(BF16)
                kdt_scr[hh, rs, :] = (k * jnp.exp(glast - gcol)).T.astype(BF16)
                sd_scr[hh, pl.ds(pl.multiple_of(c * SUBLANES, SUBLANES), SUBLANES), :] = jnp.broadcast_to(
                    jnp.broadcast_to(eg, (CHUNK, DV))[CHUNK - 1:CHUNK, :], (SUBLANES, DV))
                chains.append((hh, c, rs, sl, m, bcol, eg))
        es = [-jnp.where(lax.shift_right_logical(blk, 1) == 0, ch[4], 0.0) for ch in chains]
        lg = 1
        while (1 << lg) < CHUNK:
            cross = lax.shift_right_logical(blk, lg) == 1
            offs = [jnp.where(cross, ch[4], 0.0) for ch in chains]
            ebs = [e.astype(BF16) for e in es]
            xs = [off + _dot(eb, off.astype(BF16)) for off, eb in zip(offs, ebs)]
            es = [e - (x + _dot(x.astype(BF16), eb)) for e, x, eb in zip(es, xs, ebs)]
            lg += 1
        for (hh, c, rs, sl, m, bcol, eg), e in zip(chains, es):
            rhs = jnp.concatenate([kn_scr[rs, sl] * (bcol * eg), vv_scr[rs, sl] * bcol], axis=1)
            sol = rhs + _dot(e.astype(BF16), rhs.astype(BF16))
            wq_scr[hh, c, :CHUNK, :] = sol[:, :DK].astype(BF16)
            uc_scr[hh, rs, :] = sol[:, DK:]
        return carry

    lax.fori_loop(0, rows // (CHUNK * FACTOR_CHUNKS), factor, 0)

    def recur(c, carry):
        rs = pl.ds(pl.multiple_of(c * CHUNK, CHUNK), CHUNK)
        s_old = [s_scr[hh] for hh in range(hg)]
        ps = [_dot(wq_scr[hh, c], s_old[hh].astype(BF16)) for hh in range(hg)]
        us = [(uc_scr[hh, rs, :] - ps[hh][:CHUNK]).astype(BF16) for hh in range(hg)]
        os_ = [ps[hh][CHUNK:] + _dot(qk_scr[hh, rs, :], us[hh]) for hh in range(hg)]
        for hh in range(hg):
            sdec = sd_scr[hh, pl.ds(pl.multiple_of(c * SUBLANES, SUBLANES), SUBLANES), :]
            s_scr[hh] = s_old[hh] * sdec[:1, :] + _dot(kdt_scr[hh, rs, :], us[hh])
        for hh in range(hg):
            sl = slice(hh * DK, (hh + 1) * DK)
            oa_ref[rs, sl] = _gate_norm(os_[hh], onorm, gt_ref[rs, sl]).astype(BF16)
        return carry

    lax.fori_loop(0, rows // CHUNK, recur, 0)
    s_ref[...] = s_scr[...]


def _delta_prompt(proj, batch, seq_len, layer, dconv_w, alog_row, dtb_row, onorm):
    m = proj.shape[0]
    hg = HG_PROMPT
    wcols = hg * DK
    nchunks = seq_len // CHUNK
    assert nchunks % FACTOR_CHUNKS == 0
    nq, nk, nv, ng = (c // wcols for c in (COL_Q, COL_K, COL_V, COL_G))
    colspec = lambda off: pl.BlockSpec((seq_len, wcols), lambda b, j: (b, off + j))
    cwspec = lambda off: pl.BlockSpec((None, CONV_W, wcols), lambda b, j: (layer, 0, off + j))
    row = pl.BlockSpec((None, 1, LANES), lambda b, j: (layer, 0, 0))
    return pl.pallas_call(
        _delta_prompt_body,
        grid=(batch, H_A // hg),
        in_specs=[colspec(nq), colspec(nk), colspec(nv), colspec(ng),
                  pl.BlockSpec((seq_len, LANES), lambda b, j: (b, COL_AB // LANES)),
                  cwspec(nq), cwspec(nk), cwspec(nv), row, row, row],
        out_specs=[pl.BlockSpec((seq_len, wcols), lambda b, j: (b, j)),
                   pl.BlockSpec((None, hg, DK, DV), lambda b, j: (b, j, 0, 0))],
        out_shape=[jax.ShapeDtypeStruct((m, W_A), BF16), jax.ShapeDtypeStruct((batch, H_A, DK, DV), F32)],
        scratch_shapes=[pltpu.VMEM((seq_len, wcols), F32)] * 3
        + [pltpu.VMEM((seq_len, LANES), F32)] * 2 + [pltpu.VMEM((hg, DK, DV), F32)]
        + [pltpu.VMEM((hg, nchunks, 2 * CHUNK, DK), BF16), pltpu.VMEM((hg, seq_len, DV), F32),
           pltpu.VMEM((hg, seq_len, CHUNK), BF16), pltpu.VMEM((hg, seq_len, CHUNK), BF16),
           pltpu.VMEM((hg, nchunks * SUBLANES, DV), F32)],
        compiler_params=_cparams("parallel", "parallel"),
        name="delta_prompt",
    )(proj, proj, proj, proj, proj, dconv_w, dconv_w, dconv_w, alog_row, dtb_row, onorm)


def _delta_sample_body(q_ref, k_ref, v_ref, gt_ref, ab_ref, pe_ref, cw_ref, alog_ref, dtb_ref, on_ref, s0_ref,
                       *rest, seq_len):
    oa_ref, s_ref, qn_scr, kn_scr, vv_scr, gc_scr, bt_scr = rest[-7:]
    rows = q_ref.shape[0]
    t = _row_iota(q_ref.shape, seq_len)
    pe = pe_ref[...]
    cw = cw_ref[...]
    qn, kn, vv = _qkv_prep(_conv_rows(q_ref[...], pe[:, COL_Q:COL_K], cw[:, COL_Q:COL_K], t),
                           _conv_rows(k_ref[...], pe[:, COL_K:COL_V], cw[:, COL_K:COL_V], t),
                           _conv_rows(v_ref[...], pe[:, COL_V:COL_G], cw[:, COL_V:COL_G], t), H_A)
    qn_scr[...] = qn
    kn_scr[...] = kn
    vv_scr[...] = vv
    ab = ab_ref[...]
    g = -jnp.exp(alog_ref[...]) * jax.nn.softplus(ab + dtb_ref[...])
    gc_scr[...] = _seg_cumsum(g, _row_iota(ab.shape, seq_len), seq_len)
    bt_scr[...] = jax.nn.sigmoid(ab)
    onorm = on_ref[...]
    per_tile = SUBLANES // seq_len

    def tile(p, carry):
        r0 = pl.multiple_of(p * SUBLANES, SUBLANES)
        rs = pl.ds(r0, SUBLANES)
        gcc = gc_scr[rs, :]
        btc = bt_scr[rs, :]
        tt = _row_iota((SUBLANES, LANES), seq_len)
        tt2 = _row_iota((SUBLANES, DK + DV), seq_len)
        ri = lax.broadcasted_iota(jnp.int32, (SUBLANES, LANES), 0)
        ri2 = _row_iota((2 * SUBLANES, LANES), SUBLANES)
        lane = lax.broadcasted_iota(jnp.int32, (LANES, LANES), 1)
        zpad = jnp.zeros((LANES - SUBLANES, DK), F32)
        staged = []
        for h in range(H_A):
            sl = slice(h * DK, (h + 1) * DK)
            gc = jnp.broadcast_to(gcc[:, h:h + 1], (SUBLANES, LANES))
            beta = jnp.broadcast_to(btc[:, H_A + h:H_A + h + 1], (SUBLANES, LANES))
            q = qn_scr[rs, sl]
            k = kn_scr[rs, sl]
            v = vv_scr[rs, sl]
            eg = jnp.exp(gc)
            decs, ms = [None], [None]
            for d in range(1, seq_len):
                ok = tt >= d
                dec = jnp.where(ok, jnp.exp(jnp.where(ok, gc - pltpu.roll(gc, d, axis=0), 0.0)), 0.0)
                kk = jnp.sum(k * pltpu.roll(k, d, axis=0), axis=-1, keepdims=True)
                decs.append(dec)
                md = beta * kk * dec
                ms.append(jnp.concatenate([md, md], axis=1))
            rhs = jnp.concatenate([k * (beta * eg), v * beta], axis=1)
            sol = rhs
            for step in range(1, seq_len):
                acc = ms[1] * pltpu.roll(sol, 1, axis=0)
                for d in range(2, seq_len):
                    acc = acc + ms[d] * pltpu.roll(sol, d, axis=0)
                sol = jnp.where(tt2 == step, rhs - acc, sol)
            lhs = jnp.concatenate([sol[:, :DK], q * eg], axis=0).astype(BF16)
            staged.append((sl, gc, q, k, eg, decs, sol[:, DK:], lhs))
        pps = []
        for h in range(H_A):
            lhs = staged[h][-1]
            pp = _dot(lhs, s0_ref[p * per_tile, h].astype(BF16))
            for bb in range(1, per_tile):
                pp = jnp.where(ri2 >= bb * seq_len, _dot(lhs, s0_ref[p * per_tile + bb, h].astype(BF16)), pp)
            pps.append(pp)
        updates = []
        for h in range(H_A):
            sl, gc, q, k, eg, decs, uc, _ = staged[h]
            pp = pps[h]
            u = uc - pp[:SUBLANES]
            o = pp[SUBLANES:] + jnp.sum(q * k, axis=-1, keepdims=True) * u
            for d in range(1, seq_len):
                qk = jnp.sum(q * pltpu.roll(k, d, axis=0), axis=-1, keepdims=True)
                o = o + (qk * decs[d]) * pltpu.roll(u, d, axis=0)
            oa_ref[rs, sl] = _gate_norm(o, onorm, gt_ref[rs, sl]).astype(BF16)
            glast = jnp.broadcast_to(gc[seq_len - 1:seq_len, :], (SUBLANES, LANES))
            for bb in range(1, per_tile):
                last = bb * seq_len + seq_len - 1
                glast = jnp.where(ri >= bb * seq_len, jnp.broadcast_to(gc[last:last + 1, :], (SUBLANES, LANES)), glast)
            kd = k * jnp.exp(glast - gc)
            kdt = jnp.concatenate([kd, zpad], axis=0).T
            updates.append((kdt, jnp.concatenate([u, zpad], axis=0).astype(BF16)))
        for h in range(H_A):
            kdt, upad = updates[h]
            eg = staged[h][4]
            for bb in range(per_tile):
                last = bb * seq_len + seq_len - 1
                cols = (lane >= bb * seq_len) & (lane < (bb + 1) * seq_len)
                kd_b = jnp.where(cols, kdt, 0.0).astype(BF16)
                s_ref[p * per_tile + bb, h] = (s0_ref[p * per_tile + bb, h] * eg[last:last + 1, :]
                                               + _dot(kd_b, upad))
        return carry

    lax.fori_loop(0, rows // SUBLANES, tile, 0)


def _delta_sample(proj, seq_len, layer, pe, dconv_w, alog_row, dtb_row, onorm, s0, s_stack):
    m = proj.shape[0]
    tb = TB_SAMPLE
    rows = tb * seq_len
    colspec = lambda off: pl.BlockSpec((rows, W_A), lambda i: (i, off // W_A))
    row = pl.BlockSpec((None, 1, LANES), lambda i: (layer, 0, 0))
    sspec = pl.BlockSpec((None, tb, H_A, DK, DV), lambda i: (layer, i, 0, 0, 0))
    in_specs = [colspec(COL_Q), colspec(COL_K), colspec(COL_V), colspec(COL_G),
                pl.BlockSpec((rows, LANES), lambda i: (i, COL_AB // LANES)),
                pl.BlockSpec((rows, N_DCONV), lambda i: (i, 0)),
                pl.BlockSpec((None, CONV_W, N_DCONV), lambda i: (layer, 0, 0)),
                row, row, row, sspec]
    args = [proj, proj, proj, proj, proj, pe, dconv_w, alog_row, dtb_row, onorm, s0]
    aliases = {}
    if s_stack is not None:
        in_specs.append(pl.BlockSpec(memory_space=pl.ANY))
        aliases = {len(args): 1}
        args.append(s_stack)
    return pl.pallas_call(
        functools.partial(_delta_sample_body, seq_len=seq_len),
        grid=(m // rows,),
        in_specs=in_specs,
        out_specs=[pl.BlockSpec((rows, W_A), lambda i: (i, 0)), sspec],
        out_shape=[jax.ShapeDtypeStruct((m, W_A), BF16), jax.ShapeDtypeStruct(s0.shape, F32)],
        input_output_aliases=aliases,
        scratch_shapes=[pltpu.VMEM((rows, W_A), F32)] * 3 + [pltpu.VMEM((rows, LANES), F32)] * 2,
        compiler_params=_cparams("parallel"),
        name="delta_sample",
    )(*args)


def _pad_rows(state, seq_len):
    b, r, c = state.shape
    return jnp.pad(state, ((0, 0), (0, seq_len - r), (0, 0))).reshape(b * seq_len, c)


def _trunk(x, mod, batch, seq_len, wts, states, ffn_cast):
    (w_in, w_out, norm_g3, ffn_f32, dconv_w, alog_row, dtb_row, onorm, lconv_w, lconv_b,
     wa, wx, ba, bx, lam, final_g) = wts
    new_d, new_dc, new_l, new_lc = [], [], [], []
    s_stack = None

    def ffn(x, layer, s, fg=None):
        if (layer, s) in ffn_cast:
            return _ffn(x, mod, norm_g3, *ffn_cast[(layer, s)], layer, s, fg)
        y, w13, w2 = _ffn(x, mod, norm_g3, ffn_f32[:2], ffn_f32[2], layer, s, fg, emit=True)
        ffn_cast[(layer, s)] = (w13, w2)
        return y

    for layer in range(DEPTH):
        x = ffn(x, layer, 0)
        proj = _inproj(x, mod, norm_g3, w_in, layer)
        proj3 = proj.reshape(batch, seq_len, N_PROJ)
        if states is None:
            oa, sd = _delta_prompt(proj, batch, seq_len, layer, dconv_w, alog_row, dtb_row, onorm)
            new_d.append(sd)
            ob, hl = _lru(proj, seq_len, layer, lconv_w, lconv_b, wa, wx, ba, bx, lam)
            sl = hl.reshape(batch, SUBLANES, W_B)[:, SUBLANES - 1]
        else:
            s_delta, s_dconv, s_lru, s_lconv = states
            oa, s_stack = _delta_sample(proj, seq_len, layer, _pad_rows(s_dconv[layer], seq_len),
                                        dconv_w, alog_row, dtb_row, onorm, s_delta, s_stack)
            ob, hl = _lru(proj, seq_len, layer, lconv_w, lconv_b, wa, wx, ba, bx, lam,
                          pe=_pad_rows(s_lconv[layer], seq_len),
                          h0e=_pad_rows(s_lru[layer][:, None, :], seq_len))
            sl = hl.reshape(batch, seq_len, W_B)[:, seq_len - 1]
        x = _outproj(x, oa, ob, mod, w_out, layer)
        x = ffn(x, layer, 1, final_g if layer == DEPTH - 1 else None)
        new_dc.append(proj3[:, seq_len - (CONV_W - 1):, COL_Q:COL_G])
        new_l.append(sl)
        new_lc.append(proj3[:, seq_len - (CONV_W - 1):, COL_X:COL_Y])
    new_delta = jnp.stack(new_d) if states is None else s_stack
    return x, new_delta, jnp.stack(new_dc), jnp.stack(new_l), jnp.stack(new_lc)


def kernel(x_prompt, x_sample, c_prompt, c_sample, state_delta, state_delta_conv, state_lru, state_lru_conv, w_in, w_out, norm_g, w_ada, b_ada, ffn_w1, ffn_w3, ffn_w2, dconv_w, d_alog, d_dtbias, d_onorm, lconv_w, lconv_b, lru_wa, lru_ba, lru_wx, lru_bx, lru_lam, final_g):
    bp, lp, _ = x_prompt.shape
    bs, ls, _ = x_sample.shape
    assert lp % CHUNK == 0 and SUBLANES % ls == 0 and bp <= SUBLANES

    w_in_r = jnp.concatenate(
        [w_in[..., :OFF_A], w_in[..., OFF_X:], w_in[..., OFF_A:OFF_X],
         jnp.zeros((DEPTH, D_MODEL, N_PROJ - N_IN), w_in.dtype)], axis=-1).astype(BF16)
    w_out_b = w_out.astype(BF16)
    wa = lru_wa.astype(BF16)
    wx = lru_wx.astype(BF16)
    norm_g3 = norm_g.reshape(DEPTH * 3, 1, D_MODEL)
    lane_row = lambda v: jnp.pad(v, ((0, 0), (0, LANES - v.shape[1]))).reshape(DEPTH, 1, LANES)
    vec = lambda v: v.reshape(DEPTH, 1, W_B)
    wts = (w_in_r, w_out_b, norm_g3, (ffn_w1, ffn_w3, ffn_w2), dconv_w, lane_row(d_alog), lane_row(d_dtbias),
           d_onorm.reshape(DEPTH, 1, DV), lconv_w, vec(lconv_b), wa, wx, vec(lru_ba), vec(lru_bx),
           vec(lru_lam), final_g)

    ms = bs * ls
    c_all = jnp.concatenate([c_sample, c_prompt, jnp.zeros((SUBLANES - bp, D_MODEL), F32)], axis=0)
    mod_all = _ada(c_all, w_ada, b_ada, bs, ls)
    mod_p = _Mod(mod_all[:, ms:ms + bp].reshape(DEPTH, bp, 1, 9 * D_MODEL), False, lp)
    mod_s = _Mod(mod_all, True, ls)

    ffn_cast = {}
    ys, sd, sdc, sl_, slc = _trunk(x_sample.reshape(bs * ls, D_MODEL), mod_s, bs, ls, wts,
                                   (state_delta, state_delta_conv, state_lru, state_lru_conv), ffn_cast)
    yp, pd, pdc, pl_, plc = _trunk(x_prompt.reshape(bp * lp, D_MODEL), mod_p, bp, lp, wts, None, ffn_cast)
    return (yp.reshape(bp, lp, D_MODEL), ys.reshape(bs, ls, D_MODEL), pd, pdc, pl_, plc, sd, sdc, sl_, slc)
```

```python
import functools

import jax
import jax.numpy as jnp
from jax import lax
from jax.experimental import pallas as pl
from jax.experimental.pallas import tpu as pltpu

F32 = jnp.float32
BF16 = jnp.bfloat16

D_MODEL = 2048
DEPTH = 4
H_A = 8
DK = 128
DV = 128
W_QK = H_A * DK
W_A = H_A * DV
W_B = D_MODEL - W_A
LRU_BLOCKS = 8
LRU_BW = W_B // LRU_BLOCKS
LRU_C = 8.0
CONV_W = 4
D_FF = ((8 * D_MODEL // 3 + 127) // 128) * 128
HALF = 0.5
EPS = 1e-6
N_DCONV = 2 * W_QK + W_A
OFF_G = 2 * W_QK + W_A
OFF_A = OFF_G + W_A
OFF_X = OFF_A + 2 * H_A
N_IN = OFF_X + 2 * W_B

LANES = 128
SUBLANES = 8
VMEM_LIMIT = 56 * 1024 * 1024

COL_Q, COL_K, COL_V, COL_G = 0, W_QK, 2 * W_QK, 2 * W_QK + W_A
COL_X = COL_G + W_A
COL_Y = COL_X + W_B
COL_AB = COL_Y + W_B
N_PROJ = COL_AB + 2 * LANES

TM_ROWS = 512
TM_PROJ = 1024
TF_FFN = 512
TF_EMIT = 256
TN_PROJ = 1280
TN_ADA = 1024
CHUNK = 128
HG_PROMPT = 2
FACTOR_CHUNKS = 8
TB_SAMPLE = 16

assert N_PROJ % TN_PROJ == 0 and (9 * D_MODEL) % TN_ADA == 0 and D_FF % LANES == 0


def _cparams(*sem):
    return pltpu.CompilerParams(dimension_semantics=sem, vmem_limit_bytes=VMEM_LIMIT)


def _dot(a, b):
    return jnp.dot(a, b, preferred_element_type=F32)


def _dot_nt(a, b):
    return lax.dot_general(a, b, (((1,), (1,)), ((), ())), preferred_element_type=F32)


def _silu(x):
    return x * jax.nn.sigmoid(x)


def _expm1(x):
    u = jnp.exp(x)
    um1 = u - 1.0
    safe = um1 * x / jnp.log(jnp.where(um1 == 0.0, 2.0, u))
    return jnp.where(um1 == 0.0, x, jnp.where(x < -30.0, um1, safe))


def _norm_mod(x, g, shift, scale):
    var = jnp.mean(x * x, axis=-1, keepdims=True)
    return x * lax.rsqrt(var + EPS) * (g * (1.0 + scale)) + shift


def _row_iota(shape, mod):
    return jnp.bitwise_and(lax.broadcasted_iota(jnp.int32, shape, 0), mod - 1)


def _conv_zero_past(x_ref, w):
    rows = x_ref.shape[0]
    head = x_ref[:SUBLANES, :]
    head_y = _conv_rows(head, None, w, lax.broadcasted_iota(jnp.int32, head.shape, 0))
    y = x_ref[SUBLANES:, :] * w[CONV_W - 1:CONV_W, :]
    for j in range(1, CONV_W):
        y = y + x_ref[SUBLANES - j:rows - j, :] * w[CONV_W - 1 - j:CONV_W - j, :]
    return jnp.concatenate([head_y, y], axis=0)


def _conv_rows(x, pe, w, t):
    rows = x.shape[0]
    y = x * w[CONV_W - 1:CONV_W, :]
    for j in range(1, CONV_W):
        term = pltpu.roll(x, j, axis=0)
        if pe is None:
            term = jnp.where(t >= j, term, 0.0)
        else:
            back = CONV_W - 1 - j
            prev = pe if back == 0 else pltpu.roll(pe, rows - back, axis=0)
            term = jnp.where(t >= j, term, prev)
        y = y + term * w[CONV_W - 1 - j:CONV_W - j, :]
    return y


def _ada_body(c_ref, w_ref, b_ref, o_ref, *, nseq, reps):
    cs = _silu(c_ref[...]).astype(BF16)
    m = _dot(cs, w_ref[...].astype(BF16)) + b_ref[...]
    hi = m[:nseq].astype(BF16)
    r1 = m[:nseq] - hi.astype(F32)
    mid = r1.astype(BF16)
    lo = (r1 - mid.astype(F32)).astype(BF16)
    rows = nseq * reps
    src = lax.broadcasted_iota(jnp.int32, (rows, 3 * nseq), 1)
    dst = lax.div(lax.broadcasted_iota(jnp.int32, (rows, 3 * nseq), 0), reps)
    hit = (src == dst) | (src == dst + nseq) | (src == dst + 2 * nseq)
    pick = jnp.where(hit, 1.0, 0.0).astype(BF16)
    o_ref[:rows, :] = _dot(pick, jnp.concatenate([hi, mid, lo], axis=0))
    o_ref[rows:, :] = m[nseq:]


def _ada(c_all, w_ada, b_ada, nseq, reps):
    rows = c_all.shape[0]
    out_rows = nseq * reps + rows - nseq
    n9 = 9 * D_MODEL
    return pl.pallas_call(
        functools.partial(_ada_body, nseq=nseq, reps=reps),
        grid=(DEPTH, n9 // TN_ADA),
        in_specs=[
            pl.BlockSpec((rows, D_MODEL), lambda l, n: (0, 0)),
            pl.BlockSpec((None, D_MODEL, TN_ADA), lambda l, n: (l, 0, n)),
            pl.BlockSpec((None, 1, TN_ADA), lambda l, n: (l, 0, n)),
        ],
        out_specs=pl.BlockSpec((None, out_rows, TN_ADA), lambda l, n: (l, 0, n)),
        out_shape=jax.ShapeDtypeStruct((DEPTH, out_rows, n9), F32),
        compiler_params=_cparams("parallel", "parallel"),
        name="ada",
    )(c_all, w_ada, b_ada.reshape(DEPTH, 1, n9))


class _Mod:
    def __init__(self, arr, per_token, seq_len):
        self.arr = arr
        self.per_token = per_token
        self.seq_len = seq_len

    def spec(self, layer, tm, k, **kw):
        if self.per_token:
            return pl.BlockSpec((None, tm, D_MODEL), lambda i, *_: (layer, i, k), **kw)
        seq_len = self.seq_len
        return pl.BlockSpec((None, None, 1, D_MODEL), lambda i, *_: (layer, (i * tm) // seq_len, 0, k), **kw)


def _ffn_body(x_ref, g_ref, sh_ref, sc_ref, gt_ref, w1_ref, w3_ref, w2_ref, *rest, final, emit, tf):
    rest = list(rest)
    fg_ref = rest.pop(0) if final else None
    o_ref = rest.pop(0)
    if emit:
        w1b_ref, w3b_ref, w2b_ref = rest[:3]
        rest = rest[3:]
        w1b_ref[...] = w1_ref[...].astype(BF16)
        w3b_ref[...] = w3_ref[...].astype(BF16)
        w2b_ref[...] = w2_ref[...].astype(BF16)
        w1_ref, w3_ref, w2_ref = w1b_ref, w3b_ref, w2b_ref
    h_scr, acc_scr = rest
    f = pl.program_id(1)
    last = pl.num_programs(1) - 1
    tail = D_FF - (-(-D_FF // tf) - 1) * tf

    def swiglu(width):
        h = h_scr[...]
        a = _dot(h, w1_ref[:, :width])
        b = _dot(h, w3_ref[:, :width])
        return _dot((_silu(a) * b).astype(BF16), w2_ref[:width, :])

    @pl.when(f == 0)
    def _():
        h_scr[...] = _norm_mod(x_ref[...], g_ref[...], sh_ref[...], sc_ref[...]).astype(BF16)
        acc_scr[...] = swiglu(tf)

    @pl.when((f > 0) & (f < last))
    def _():
        acc_scr[...] += swiglu(tf)

    @pl.when(f == last)
    def _():
        y = x_ref[...] + HALF * gt_ref[...] * (acc_scr[...] + swiglu(tail))
        if final:
            var = jnp.mean(y * y, axis=-1, keepdims=True)
            y = y * lax.rsqrt(var + EPS) * fg_ref[...]
        o_ref[...] = y


def _ffn(x, mod, norm_g3, w13, w2, layer, s, final_g=None, emit=False):
    m = x.shape[0]
    tm = min(TM_ROWS, m)
    tf = TF_EMIT if emit else TF_FFN
    steps = -(-D_FF // tf)
    assert steps >= 2
    sub = 2 * s
    final = final_g is not None
    w1, w3 = w13
    if emit:
        assert m == tm
        once = dict(pipeline_mode=pl.Buffered(1))
        up = pl.BlockSpec((None, None, D_MODEL, tf), lambda i, f: (layer, s, 0, f))
        down = pl.BlockSpec((None, None, tf, D_MODEL), lambda i, f: (layer, s, f, 0))
    else:
        once = {}
        up = pl.BlockSpec((D_MODEL, tf), lambda i, f: (0, f))
        down = pl.BlockSpec((tf, D_MODEL), lambda i, f: (f, 0))
    in_specs = [
        pl.BlockSpec((tm, D_MODEL), lambda i, f: (i, 0), **once),
        pl.BlockSpec((None, 1, D_MODEL), lambda i, f: (layer * 3 + sub, 0, 0)),
        mod.spec(layer, tm, 3 * sub, **once), mod.spec(layer, tm, 3 * sub + 1, **once),
        mod.spec(layer, tm, 3 * sub + 2, **once),
        up, up, down,
    ]
    args = [x, norm_g3, mod.arr, mod.arr, mod.arr, w1, w3, w2]
    if final:
        in_specs.append(pl.BlockSpec((1, D_MODEL), lambda i, f: (0, 0)))
        args.append(final_g.reshape(1, D_MODEL))
    out_specs = [pl.BlockSpec((tm, D_MODEL), lambda i, f: (i, 0))]
    out_shape = [jax.ShapeDtypeStruct((m, D_MODEL), F32)]
    if emit:
        out_specs += [pl.BlockSpec((D_MODEL, tf), lambda i, f: (0, f))] * 2 + [pl.BlockSpec((tf, D_MODEL), lambda i, f: (f, 0))]
        out_shape += [jax.ShapeDtypeStruct((D_MODEL, D_FF), BF16)] * 2 + [jax.ShapeDtypeStruct((D_FF, D_MODEL), BF16)]
    outs = pl.pallas_call(
        functools.partial(_ffn_body, final=final, emit=emit, tf=tf),
        grid=(m // tm, steps),
        in_specs=in_specs,
        out_specs=out_specs,
        out_shape=out_shape,
        scratch_shapes=[pltpu.VMEM((tm, D_MODEL), BF16), pltpu.VMEM((tm, D_MODEL), F32)],
        compiler_params=_cparams("parallel", "arbitrary"),
        name="ffn_cast" if emit else "ffn",
    )(*args)
    return (outs[0], (outs[1], outs[2]), outs[3]) if emit else outs[0]


def _inproj_body(x_ref, g_ref, sh_ref, sc_ref, w_ref, o_ref, h_scr):
    @pl.when(pl.program_id(1) == 0)
    def _():
        h_scr[...] = _norm_mod(x_ref[...], g_ref[...], sh_ref[...], sc_ref[...]).astype(BF16)

    o_ref[...] = _dot(h_scr[...], w_ref[...])


def _inproj(x, mod, norm_g3, w_in, layer):
    m = x.shape[0]
    tm = min(TM_PROJ, m)
    return pl.pallas_call(
        _inproj_body,
        grid=(m // tm, N_PROJ // TN_PROJ),
        in_specs=[
            pl.BlockSpec((tm, D_MODEL), lambda i, n: (i, 0)),
            pl.BlockSpec((None, 1, D_MODEL), lambda i, n: (layer * 3 + 1, 0, 0)),
            mod.spec(layer, tm, 3), mod.spec(layer, tm, 4),
            pl.BlockSpec((None, D_MODEL, TN_PROJ), lambda i, n: (layer, 0, n)),
        ],
        out_specs=pl.BlockSpec((tm, TN_PROJ), lambda i, n: (i, n)),
        out_shape=jax.ShapeDtypeStruct((m, N_PROJ), F32),
        scratch_shapes=[pltpu.VMEM((tm, D_MODEL), BF16)],
        compiler_params=_cparams("parallel", "arbitrary"),
        name="inproj",
    )(x, norm_g3, mod.arr, mod.arr, w_in)


def _outproj_body(x_ref, oa_ref, ob_ref, gt_ref, wa_ref, wb_ref, o_ref):
    acc = _dot(oa_ref[...], wa_ref[...]) + _dot(ob_ref[...], wb_ref[...])
    o_ref[...] = x_ref[...] + gt_ref[...] * acc


def _outproj(x, oa, ob, mod, w_out, layer):
    m = x.shape[0]
    tm = min(TM_ROWS, m)
    return pl.pallas_call(
        _outproj_body,
        grid=(m // tm,),
        in_specs=[
            pl.BlockSpec((tm, D_MODEL), lambda i: (i, 0)),
            pl.BlockSpec((tm, W_A), lambda i: (i, 0)),
            pl.BlockSpec((tm, W_B), lambda i: (i, 0)),
            mod.spec(layer, tm, 5),
            pl.BlockSpec((None, W_A, D_MODEL), lambda i: (layer, 0, 0)),
            pl.BlockSpec((None, W_B, D_MODEL), lambda i: (layer, 1, 0)),
        ],
        out_specs=pl.BlockSpec((tm, D_MODEL), lambda i: (i, 0)),
        out_shape=jax.ShapeDtypeStruct((m, D_MODEL), F32),
        compiler_params=_cparams("parallel"),
        name="outproj",
    )(x, oa, ob, mod.arr, w_out, w_out)


def _lru_body(*refs, seq_len, has_state):
    if has_state:
        (x_ref, y_ref, pe_ref, h0_ref, cw_ref, cb_ref, wa_ref, wx_ref, ba_ref, bx_ref, lam_ref,
         ob_ref, hl_ref, a_scr, b_scr) = refs
    else:
        (x_ref, y_ref, cw_ref, cb_ref, wa_ref, wx_ref, ba_ref, bx_ref, lam_ref,
         ob_ref, hl_ref, a_scr, b_scr) = refs
    rows = x_ref.shape[0]
    if has_state:
        xc = _conv_rows(x_ref[...], pe_ref[...], cw_ref[...], _row_iota(x_ref.shape, seq_len))
    else:
        xc = _conv_zero_past(x_ref, cw_ref[...])
    xc = xc + cb_ref[...]
    xcb = xc.astype(BF16)
    r = jax.nn.sigmoid(_dot(xcb, wa_ref[...]) + ba_ref[...])
    i = jax.nn.sigmoid(_dot(xcb, wx_ref[...]) + bx_ref[...])
    log_a = -LRU_C * r * jax.nn.softplus(-lam_ref[...])
    a = jnp.exp(log_a)
    b = jnp.sqrt(-_expm1(2.0 * log_a)) * (i * xc)
    if has_state:
        b = b + a * h0_ref[...]
    group = min(seq_len, SUBLANES)
    tg = _row_iota(x_ref.shape, group)
    d = 1
    while d < group:
        keep = tg >= d
        b = jnp.where(keep, a * pltpu.roll(b, d, axis=0) + b, b)
        a = jnp.where(keep, a * pltpu.roll(a, d, axis=0), a)
        d *= 2
    if seq_len > SUBLANES:
        a_scr[...] = a
        b_scr[...] = b

        def step(k, carry):
            r0 = pl.multiple_of(k * SUBLANES, SUBLANES)
            hk = b_scr[pl.ds(r0, SUBLANES), :] + a_scr[pl.ds(r0, SUBLANES), :] * carry
            b_scr[pl.ds(r0, SUBLANES), :] = hk
            return hk[SUBLANES - 1:SUBLANES, :]

        lax.fori_loop(0, rows // SUBLANES, step, jnp.zeros((1, LRU_BW), F32), unroll=8)
        h = b_scr[...]
    else:
        h = b
    ob_ref[...] = (h * jax.nn.gelu(y_ref[...])).astype(BF16)
    hl_ref[...] = h[rows - hl_ref.shape[0]:, :]


def _lru(proj, seq_len, layer, lconv_w, lconv_b, wa, wx, ba, bx, lam, pe=None, h0e=None):
    m = proj.shape[0]
    has_state = pe is not None
    rows = seq_len if not has_state else m
    nseq = m // rows
    hl_rows = rows if has_state else SUBLANES
    bx0, by0 = COL_X // LRU_BW, COL_Y // LRU_BW
    col = lambda b, n: (b, n)
    vec = pl.BlockSpec((None, 1, LRU_BW), lambda b, n: (layer, 0, n))
    blk = pl.BlockSpec((None, None, LRU_BW, LRU_BW), lambda b, n: (layer, n, 0, 0))
    in_specs = [pl.BlockSpec((rows, LRU_BW), lambda b, n: (b, bx0 + n)),
                pl.BlockSpec((rows, LRU_BW), lambda b, n: (b, by0 + n))]
    args = [proj, proj]
    if has_state:
        in_specs += [pl.BlockSpec((rows, LRU_BW), col), pl.BlockSpec((rows, LRU_BW), col)]
        args += [pe, h0e]
    in_specs += [pl.BlockSpec((None, CONV_W, LRU_BW), lambda b, n: (layer, 0, n)), vec, blk, blk, vec, vec, vec]
    args += [lconv_w, lconv_b, wa, wx, ba, bx, lam]
    return pl.pallas_call(
        functools.partial(_lru_body, seq_len=seq_len, has_state=has_state),
        grid=(nseq, LRU_BLOCKS),
        in_specs=in_specs,
        out_specs=[pl.BlockSpec((rows, LRU_BW), col), pl.BlockSpec((hl_rows, LRU_BW), col)],
        out_shape=[jax.ShapeDtypeStruct((m, W_B), BF16), jax.ShapeDtypeStruct((nseq * hl_rows, W_B), F32)],
        scratch_shapes=[pltpu.VMEM((rows, LRU_BW), F32), pltpu.VMEM((rows, LRU_BW), F32)],
        compiler_params=_cparams("parallel", "parallel"),
        name="lru",
    )(*args)


def _qkv_prep(q_conv, k_conv, v_conv, heads):
    q, k, v = _silu(q_conv), _silu(k_conv), _silu(v_conv)
    qs, ks = [], []
    for h in range(heads):
        sl = slice(h * DK, (h + 1) * DK)
        qh, kh = q[:, sl], k[:, sl]
        qs.append(qh * (lax.rsqrt(jnp.sum(qh * qh, axis=-1, keepdims=True) + EPS) * DK ** -0.5))
        ks.append(kh * lax.rsqrt(jnp.sum(kh * kh, axis=-1, keepdims=True) + EPS))
    return jnp.concatenate(qs, axis=1), jnp.concatenate(ks, axis=1), v


def _gate_norm(o, onorm, gate_pre):
    var = jnp.mean(o * o, axis=-1, keepdims=True)
    return o * lax.rsqrt(var + EPS) * onorm * _silu(gate_pre)


def _seg_cumsum(g, tg, group):
    d = 1
    while d < group:
        g = jnp.where(tg >= d, g + pltpu.roll(g, d, axis=0), g)
        d *= 2
    return g


def _delta_prompt_body(q_ref, k_ref, v_ref, gt_ref, ab_ref, cwq_ref, cwk_ref, cwv_ref, alog_ref, dtb_ref,
                       on_ref, oa_ref, s_ref, qn_scr, kn_scr, vv_scr, gc_scr, bt_scr, s_scr,
                       wq_scr, uc_scr, qk_scr, kdt_scr, sd_scr):
    rows = q_ref.shape[0]
    hg = q_ref.shape[1] // DK
    qn, kn, vv = _qkv_prep(_conv_zero_past(q_ref, cwq_ref[...]), _conv_zero_past(k_ref, cwk_ref[...]),
                           _conv_zero_past(v_ref, cwv_ref[...]), hg)
    qn_scr[...] = qn
    kn_scr[...] = kn
    vv_scr[...] = vv
    ab = ab_ref[...]
    g = -jnp.exp(alog_ref[...]) * jax.nn.softplus(ab + dtb_ref[...])
    gc_scr[...] = _seg_cumsum(g, _row_iota(ab.shape, CHUNK), CHUNK)
    bt_scr[...] = jax.nn.sigmoid(ab)
    s_scr[...] = jnp.zeros_like(s_scr)
    head0 = pl.program_id(1) * hg
    onorm = on_ref[...]

    def factor(step, carry):
        lane = lax.broadcasted_iota(jnp.int32, (CHUNK, LANES), 1)
        ri = lax.broadcasted_iota(jnp.int32, (CHUNK, CHUNK), 0)
        ci = lax.broadcasted_iota(jnp.int32, (CHUNK, CHUNK), 1)
        causal = ri >= ci
        strict = ri > ci
        blk = jnp.bitwise_xor(ri, ci)
        chains = []
        for cc in range(FACTOR_CHUNKS):
            c = step * FACTOR_CHUNKS + cc
            rs = pl.ds(pl.multiple_of(c * CHUNK, CHUNK), CHUNK)
            gcc = gc_scr[rs, :]
            btc = bt_scr[rs, :]
            for hh in range(hg):
                sl = slice(hh * DK, (hh + 1) * DK)
                gcol = jnp.sum(jnp.where(lane == head0 + hh, gcc, 0.0), axis=1, keepdims=True)
                bcol = jnp.sum(jnp.where(lane == head0 + hh + H_A, btc, 0.0), axis=1, keepdims=True)
                gmat = jnp.broadcast_to(gcol, (CHUNK, CHUNK))
                decay = jnp.where(causal, jnp.exp(jnp.where(causal, gmat - gmat.T, 0.0)), 0.0)
                q = qn_scr[rs, sl]
                k = kn_scr[rs, sl]
                kq = _dot_nt(jnp.concatenate([k, q], axis=0).astype(BF16), k.astype(BF16))
                m = jnp.where(strict, bcol * kq[:CHUNK] * decay, 0.0)
                eg = jnp.exp(gcol)
                glast = jnp.broadcast_to(gcol[CHUNK - 1:CHUNK, :], (CHUNK, 1))
                wq_scr[hh, c, CHUNK:, :] = (q * eg).astype(BF16)
                qk_scr[hh, rs, :] = (kq[CHUNK:] * decay).astype(BF16)
                kdt_scr[hh, rs, :] = (k * jnp.exp(glast - gcol)).T.astype(BF16)
                sd_scr[hh, pl.ds(pl.multiple_of(c * SUBLANES, SUBLANES), SUBLANES), :] = jnp.broadcast_to(
                    jnp.broadcast_to(eg, (CHUNK, DV))[CHUNK - 1:CHUNK, :], (SUBLANES, DV))
                chains.append((hh, c, rs, sl, m, bcol, eg))
        es = [-jnp.where(lax.shift_right_logical(blk, 1) == 0, ch[4], 0.0) for ch in chains]
        lg = 1
        while (1 << lg) < CHUNK:
            cross = lax.shift_right_logical(blk, lg) == 1
            offs = [jnp.where(cross, ch[4], 0.0) for ch in chains]
            ebs = [e.astype(BF16) for e in es]
            xs = [off + _dot(eb, off.astype(BF16)) for off, eb in zip(offs, ebs)]
            es = [e - (x + _dot(x.astype(BF16), eb)) for e, x, eb in zip(es, xs, ebs)]
            lg += 1
        for (hh, c, rs, sl, m, bcol, eg), e in zip(chains, es):
            rhs = jnp.concatenate([kn_scr[rs, sl] * (bcol * eg), vv_scr[rs, sl] * bcol], axis=1)
            sol = rhs + _dot(e.astype(BF16), rhs.astype(BF16))
            wq_scr[hh, c, :CHUNK, :] = sol[:, :DK].astype(BF16)
            uc_scr[hh, rs, :] = sol[:, DK:]
        return carry

    lax.fori_loop(0, rows // (CHUNK * FACTOR_CHUNKS), factor, 0)

    def recur(c, carry):
        rs = pl.ds(pl.multiple_of(c * CHUNK, CHUNK), CHUNK)
        s_old = [s_scr[hh] for hh in range(hg)]
        ps = [_dot(wq_scr[hh, c], s_old[hh].astype(BF16)) for hh in range(hg)]
        us = [(uc_scr[hh, rs, :] - ps[hh][:CHUNK]).astype(BF16) for hh in range(hg)]
        os_ = [ps[hh][CHUNK:] + _dot(qk_scr[hh, rs, :], us[hh]) for hh in range(hg)]
        for hh in range(hg):
            sdec = sd_scr[hh, pl.ds(pl.multiple_of(c * SUBLANES, SUBLANES), SUBLANES), :]
            s_scr[hh] = s_old[hh] * sdec[:1, :] + _dot(kdt_scr[hh, rs, :], us[hh])
        for hh in range(hg):
            sl = slice(hh * DK, (hh + 1) * DK)
            oa_ref[rs, sl] = _gate_norm(os_[hh], onorm, gt_ref[rs, sl]).astype(BF16)
        return carry

    lax.fori_loop(0, rows // CHUNK, recur, 0)
    s_ref[...] = s_scr[...]


def _delta_prompt(proj, batch, seq_len, layer, dconv_w, alog_row, dtb_row, onorm):
    m = proj.shape[0]
    hg = HG_PROMPT
    wcols = hg * DK
    nchunks = seq_len // CHUNK
    assert nchunks % FACTOR_CHUNKS == 0
    nq, nk, nv, ng = (c // wcols for c in (COL_Q, COL_K, COL_V, COL_G))
    colspec = lambda off: pl.BlockSpec((seq_len, wcols), lambda b, j: (b, off + j))
    cwspec = lambda off: pl.BlockSpec((None, CONV_W, wcols), lambda b, j: (layer, 0, off + j))
    row = pl.BlockSpec((None, 1, LANES), lambda b, j: (layer, 0, 0))
    return pl.pallas_call(
        _delta_prompt_body,
        grid=(batch, H_A // hg),
        in_specs=[colspec(nq), colspec(nk), colspec(nv), colspec(ng),
                  pl.BlockSpec((seq_len, LANES), lambda b, j: (b, COL_AB // LANES)),
                  cwspec(nq), cwspec(nk), cwspec(nv), row, row, row],
        out_specs=[pl.BlockSpec((seq_len, wcols), lambda b, j: (b, j)),
                   pl.BlockSpec((None, hg, DK, DV), lambda b, j: (b, j, 0, 0))],
        out_shape=[jax.ShapeDtypeStruct((m, W_A), BF16), jax.ShapeDtypeStruct((batch, H_A, DK, DV), F32)],
        scratch_shapes=[pltpu.VMEM((seq_len, wcols), F32)] * 3
        + [pltpu.VMEM((seq_len, LANES), F32)] * 2 + [pltpu.VMEM((hg, DK, DV), F32)]
        + [pltpu.VMEM((hg, nchunks, 2 * CHUNK, DK), BF16), pltpu.VMEM((hg, seq_len, DV), F32),
           pltpu.VMEM((hg, seq_len, CHUNK), BF16), pltpu.VMEM((hg, seq_len, CHUNK), BF16),
           pltpu.VMEM((hg, nchunks * SUBLANES, DV), F32)],
        compiler_params=_cparams("parallel", "parallel"),
        name="delta_prompt",
    )(proj, proj, proj, proj, proj, dconv_w, dconv_w, dconv_w, alog_row, dtb_row, onorm)


def _delta_sample_body(q_ref, k_ref, v_ref, gt_ref, ab_ref, pe_ref, cw_ref, alog_ref, dtb_ref, on_ref, s0_ref,
                       *rest, seq_len):
    oa_ref, s_ref, qn_scr, kn_scr, vv_scr, gc_scr, bt_scr = rest[-7:]
    rows = q_ref.shape[0]
    t = _row_iota(q_ref.shape, seq_len)
    pe = pe_ref[...]
    cw = cw_ref[...]
    qn, kn, vv = _qkv_prep(_conv_rows(q_ref[...], pe[:, COL_Q:COL_K], cw[:, COL_Q:COL_K], t),
                           _conv_rows(k_ref[...], pe[:, COL_K:COL_V], cw[:, COL_K:COL_V], t),
                           _conv_rows(v_ref[...], pe[:, COL_V:COL_G], cw[:, COL_V:COL_G], t), H_A)
    qn_scr[...] = qn
    kn_scr[...] = kn
    vv_scr[...] = vv
    ab = ab_ref[...]
    g = -jnp.exp(alog_ref[...]) * jax.nn.softplus(ab + dtb_ref[...])
    gc_scr[...] = _seg_cumsum(g, _row_iota(ab.shape, seq_len), seq_len)
    bt_scr[...] = jax.nn.sigmoid(ab)
    onorm = on_ref[...]
    per_tile = SUBLANES // seq_len

    def tile(p, carry):
        r0 = pl.multiple_of(p * SUBLANES, SUBLANES)
        rs = pl.ds(r0, SUBLANES)
        gcc = gc_scr[rs, :]
        btc = bt_scr[rs, :]
        tt = _row_iota((SUBLANES, LANES), seq_len)
        tt2 = _row_iota((SUBLANES, DK + DV), seq_len)
        ri = lax.broadcasted_iota(jnp.int32, (SUBLANES, LANES), 0)
        ri2 = _row_iota((2 * SUBLANES, LANES), SUBLANES)
        lane = lax.broadcasted_iota(jnp.int32, (LANES, LANES), 1)
        zpad = jnp.zeros((LANES - SUBLANES, DK), F32)
        staged = []
        for h in range(H_A):
            sl = slice(h * DK, (h + 1) * DK)
            gc = jnp.broadcast_to(gcc[:, h:h + 1], (SUBLANES, LANES))
            beta = jnp.broadcast_to(btc[:, H_A + h:H_A + h + 1], (SUBLANES, LANES))
            q = qn_scr[rs, sl]
            k = kn_scr[rs, sl]
            v = vv_scr[rs, sl]
            eg = jnp.exp(gc)
            decs, ms = [None], [None]
            for d in range(1, seq_len):
                ok = tt >= d
                dec = jnp.where(ok, jnp.exp(jnp.where(ok, gc - pltpu.roll(gc, d, axis=0), 0.0)), 0.0)
                kk = jnp.sum(k * pltpu.roll(k, d, axis=0), axis=-1, keepdims=True)
                decs.append(dec)
                md = beta * kk * dec
                ms.append(jnp.concatenate([md, md], axis=1))
            rhs = jnp.concatenate([k * (beta * eg), v * beta], axis=1)
            sol = rhs
            for step in range(1, seq_len):
                acc = ms[1] * pltpu.roll(sol, 1, axis=0)
                for d in range(2, seq_len):
                    acc = acc + ms[d] * pltpu.roll(sol, d, axis=0)
                sol = jnp.where(tt2 == step, rhs - acc, sol)
            lhs = jnp.concatenate([sol[:, :DK], q * eg], axis=0).astype(BF16)
            staged.append((sl, gc, q, k, eg, decs, sol[:, DK:], lhs))
        pps = []
        for h in range(H_A):
            lhs = staged[h][-1]
            pp = _dot(lhs, s0_ref[p * per_tile, h].astype(BF16))
            for bb in range(1, per_tile):
                pp = jnp.where(ri2 >= bb * seq_len, _dot(lhs, s0_ref[p * per_tile + bb, h].astype(BF16)), pp)
            pps.append(pp)
        updates = []
        for h in range(H_A):
            sl, gc, q, k, eg, decs, uc, _ = staged[h]
            pp = pps[h]
            u = uc - pp[:SUBLANES]
            o = pp[SUBLANES:] + jnp.sum(q * k, axis=-1, keepdims=True) * u
            for d in range(1, seq_len):
                qk = jnp.sum(q * pltpu.roll(k, d, axis=0), axis=-1, keepdims=True)
                o = o + (qk * decs[d]) * pltpu.roll(u, d, axis=0)
            oa_ref[rs, sl] = _gate_norm(o, onorm, gt_ref[rs, sl]).astype(BF16)
            glast = jnp.broadcast_to(gc[seq_len - 1:seq_len, :], (SUBLANES, LANES))
            for bb in range(1, per_tile):
                last = bb * seq_len + seq_len - 1
                glast = jnp.where(ri >= bb * seq_len, jnp.broadcast_to(gc[last:last + 1, :], (SUBLANES, LANES)), glast)
            kd = k * jnp.exp(glast - gc)
            kdt = jnp.concatenate([kd, zpad], axis=0).T
            updates.append((kdt, jnp.concatenate([u, zpad], axis=0).astype(BF16)))
        for h in range(H_A):
            kdt, upad = updates[h]
            eg = staged[h][4]
            for bb in range(per_tile):
                last = bb * seq_len + seq_len - 1
                cols = (lane >= bb * seq_len) & (lane < (bb + 1) * seq_len)
                kd_b = jnp.where(cols, kdt, 0.0).astype(BF16)
                s_ref[p * per_tile + bb, h] = (s0_ref[p * per_tile + bb, h] * eg[last:last + 1, :]
                                               + _dot(kd_b, upad))
        return carry

    lax.fori_loop(0, rows // SUBLANES, tile, 0)


def _delta_sample(proj, seq_len, layer, pe, dconv_w, alog_row, dtb_row, onorm, s0, s_stack):
    m = proj.shape[0]
    tb = TB_SAMPLE
    rows = tb * seq_len
    colspec = lambda off: pl.BlockSpec((rows, W_A), lambda i: (i, off // W_A))
    row = pl.BlockSpec((None, 1, LANES), lambda i: (layer, 0, 0))
    sspec = pl.BlockSpec((None, tb, H_A, DK, DV), lambda i: (layer, i, 0, 0, 0))
    in_specs = [colspec(COL_Q), colspec(COL_K), colspec(COL_V), colspec(COL_G),
                pl.BlockSpec((rows, LANES), lambda i: (i, COL_AB // LANES)),
                pl.BlockSpec((rows, N_DCONV), lambda i: (i, 0)),
                pl.BlockSpec((None, CONV_W, N_DCONV), lambda i: (layer, 0, 0)),
                row, row, row, sspec]
    args = [proj, proj, proj, proj, proj, pe, dconv_w, alog_row, dtb_row, onorm, s0]
    aliases = {}
    if s_stack is not None:
        in_specs.append(pl.BlockSpec(memory_space=pl.ANY))
        aliases = {len(args): 1}
        args.append(s_stack)
    return pl.pallas_call(
        functools.partial(_delta_sample_body, seq_len=seq_len),
        grid=(m // rows,),
        in_specs=in_specs,
        out_specs=[pl.BlockSpec((rows, W_A), lambda i: (i, 0)), sspec],
        out_shape=[jax.ShapeDtypeStruct((m, W_A), BF16), jax.ShapeDtypeStruct(s0.shape, F32)],
        input_output_aliases=aliases,
        scratch_shapes=[pltpu.VMEM((rows, W_A), F32)] * 3 + [pltpu.VMEM((rows, LANES), F32)] * 2,
        compiler_params=_cparams("parallel"),
        name="delta_sample",
    )(*args)


def _pad_rows(state, seq_len):
    b, r, c = state.shape
    return jnp.pad(state, ((0, 0), (0, seq_len - r), (0, 0))).reshape(b * seq_len, c)


def _trunk(x, mod, batch, seq_len, wts, states, ffn_cast):
    (w_in, w_out, norm_g3, ffn_f32, dconv_w, alog_row, dtb_row, onorm, lconv_w, lconv_b,
     wa, wx, ba, bx, lam, final_g) = wts
    new_d, new_dc, new_l, new_lc = [], [], [], []
    s_stack = None

    def ffn(x, layer, s, fg=None):
        if (layer, s) in ffn_cast:
            return _ffn(x, mod, norm_g3, *ffn_cast[(layer, s)], layer, s, fg)
        y, w13, w2 = _ffn(x, mod, norm_g3, ffn_f32[:2], ffn_f32[2], layer, s, fg, emit=True)
        ffn_cast[(layer, s)] = (w13, w2)
        return y

    for layer in range(DEPTH):
        x = ffn(x, layer, 0)
        proj = _inproj(x, mod, norm_g3, w_in, layer)
        if states is None:
            oa, sd = _delta_prompt(proj, batch, seq_len, layer, dconv_w, alog_row, dtb_row, onorm)
            new_d.append(sd)
            ob, hl = _lru(proj, seq_len, layer, lconv_w, lconv_b, wa, wx, ba, bx, lam)
            sl = hl.reshape(batch, SUBLANES, W_B)[:, SUBLANES - 1]
        else:
            s_delta, s_dconv, s_lru, s_lconv = states
            oa, s_stack = _delta_sample(proj, seq_len, layer, _pad_rows(s_dconv[layer], seq_len),
                                        dconv_w, alog_row, dtb_row, onorm, s_delta, s_stack)
            ob, hl = _lru(proj, seq_len, layer, lconv_w, lconv_b, wa, wx, ba, bx, lam,
                          pe=_pad_rows(s_lconv[layer], seq_len),
                          h0e=_pad_rows(s_lru[layer][:, None, :], seq_len))
            sl = hl.reshape(batch, seq_len, W_B)[:, seq_len - 1]
        x = _outproj(x, oa, ob, mod, w_out, layer)
        x = ffn(x, layer, 1, final_g if layer == DEPTH - 1 else None)
        last_rows = lambda lo, hi: proj[:, lo:hi].reshape(batch, seq_len, hi - lo)[:, seq_len - (CONV_W - 1):]
        new_dc.append(last_rows(COL_Q, COL_G))
        new_l.append(sl)
        new_lc.append(last_rows(COL_X, COL_Y))
    new_delta = jnp.stack(new_d) if states is None else s_stack
    return x, new_delta, jnp.stack(new_dc), jnp.stack(new_l), jnp.stack(new_lc)


def kernel(x_prompt, x_sample, c_prompt, c_sample, state_delta, state_delta_conv, state_lru, state_lru_conv, w_in, w_out, norm_g, w_ada, b_ada, ffn_w1, ffn_w3, ffn_w2, dconv_w, d_alog, d_dtbias, d_onorm, lconv_w, lconv_b, lru_wa, lru_ba, lru_wx, lru_bx, lru_lam, final_g):
    bp, lp, _ = x_prompt.shape
    bs, ls, _ = x_sample.shape
    assert lp % CHUNK == 0 and SUBLANES % ls == 0 and bp <= SUBLANES

    w_in_r = jnp.concatenate(
        [w_in[..., :OFF_A], w_in[..., OFF_X:], w_in[..., OFF_A:OFF_X],
         jnp.zeros((DEPTH, D_MODEL, N_PROJ - N_IN), w_in.dtype)], axis=-1).astype(BF16)
    w_out_b = w_out.astype(BF16)
    wa = lru_wa.astype(BF16)
    wx = lru_wx.astype(BF16)
    norm_g3 = norm_g.reshape(DEPTH * 3, 1, D_MODEL)
    lane_row = lambda v: jnp.pad(v, ((0, 0), (0, LANES - v.shape[1]))).reshape(DEPTH, 1, LANES)
    vec = lambda v: v.reshape(DEPTH, 1, W_B)
    wts = (w_in_r, w_out_b, norm_g3, (ffn_w1, ffn_w3, ffn_w2), dconv_w, lane_row(d_alog), lane_row(d_dtbias),
           d_onorm.reshape(DEPTH, 1, DV), lconv_w, vec(lconv_b), wa, wx, vec(lru_ba), vec(lru_bx),
           vec(lru_lam), final_g)

    ms = bs * ls
    c_all = jnp.concatenate([c_sample, c_prompt, jnp.zeros((SUBLANES - bp, D_MODEL), F32)], axis=0)
    mod_all = _ada(c_all, w_ada, b_ada, bs, ls)
    mod_p = _Mod(mod_all[:, ms:ms + bp].reshape(DEPTH, bp, 1, 9 * D_MODEL), False, lp)
    mod_s = _Mod(mod_all, True, ls)

    ffn_cast = {}
    ys, sd, sdc, sl_, slc = _trunk(x_sample.reshape(bs * ls, D_MODEL), mod_s, bs, ls, wts,
                                   (state_delta, state_delta_conv, state_lru, state_lru_conv), ffn_cast)
    yp, pd, pdc, pl_, plc = _trunk(x_prompt.reshape(bp * lp, D_MODEL), mod_p, bp, lp, wts, None, ffn_cast)
    return (yp.reshape(bp, lp, D_MODEL), ys.reshape(bs, ls, D_MODEL), pd, pdc, pl_, plc, sd, sdc, sl_, slc)
```

```python
import functools

import jax
import jax.numpy as jnp
from jax import lax
from jax.experimental import pallas as pl
from jax.experimental.pallas import tpu as pltpu

F32 = jnp.float32
BF16 = jnp.bfloat16

D_MODEL = 2048
DEPTH = 4
H_A = 8
DK = 128
DV = 128
W_QK = H_A * DK
W_A = H_A * DV
W_B = D_MODEL - W_A
LRU_BLOCKS = 8
LRU_BW = W_B // LRU_BLOCKS
LRU_C = 8.0
CONV_W = 4
D_FF = ((8 * D_MODEL // 3 + 127) // 128) * 128
HALF = 0.5
EPS = 1e-6
N_DCONV = 2 * W_QK + W_A
OFF_G = 2 * W_QK + W_A
OFF_A = OFF_G + W_A
OFF_X = OFF_A + 2 * H_A
N_IN = OFF_X + 2 * W_B

LANES = 128
SUBLANES = 8
VMEM_LIMIT = 56 * 1024 * 1024

COL_Q, COL_K, COL_V, COL_G = 0, W_QK, 2 * W_QK, 2 * W_QK + W_A
COL_X = COL_G + W_A
COL_Y = COL_X + W_B
COL_AB = COL_Y + W_B
N_PROJ = COL_AB + 2 * LANES

TM_ROWS = 512
TM_PROJ = 1024
TF_FFN = 512
TF_EMIT = 256
TN_PROJ = 1280
TN_ADA = 1024
CHUNK = 128
HG_PROMPT = 2
FACTOR_CHUNKS = 8
TB_SAMPLE = 16

assert N_PROJ % TN_PROJ == 0 and (9 * D_MODEL) % TN_ADA == 0 and D_FF % LANES == 0


def _cparams(*sem):
    return pltpu.CompilerParams(dimension_semantics=sem, vmem_limit_bytes=VMEM_LIMIT)


def _dot(a, b):
    return jnp.dot(a, b, preferred_element_type=F32)


def _dot_nt(a, b):
    return lax.dot_general(a, b, (((1,), (1,)), ((), ())), preferred_element_type=F32)


def _silu(x):
    return x * jax.nn.sigmoid(x)


def _expm1(x):
    u = jnp.exp(x)
    um1 = u - 1.0
    safe = um1 * x / jnp.log(jnp.where(um1 == 0.0, 2.0, u))
    return jnp.where(um1 == 0.0, x, jnp.where(x < -30.0, um1, safe))


def _norm_mod(x, g, shift, scale):
    var = jnp.mean(x * x, axis=-1, keepdims=True)
    return x * lax.rsqrt(var + EPS) * (g * (1.0 + scale)) + shift


def _row_iota(shape, mod):
    return jnp.bitwise_and(lax.broadcasted_iota(jnp.int32, shape, 0), mod - 1)


def _conv_zero_past(x_ref, w):
    rows = x_ref.shape[0]
    head = x_ref[:SUBLANES, :]
    head_y = _conv_rows(head, None, w, lax.broadcasted_iota(jnp.int32, head.shape, 0))
    y = x_ref[SUBLANES:, :] * w[CONV_W - 1:CONV_W, :]
    for j in range(1, CONV_W):
        y = y + x_ref[SUBLANES - j:rows - j, :] * w[CONV_W - 1 - j:CONV_W - j, :]
    return jnp.concatenate([head_y, y], axis=0)


def _conv_rows(x, pe, w, t):
    rows = x.shape[0]
    y = x * w[CONV_W - 1:CONV_W, :]
    for j in range(1, CONV_W):
        term = pltpu.roll(x, j, axis=0)
        if pe is None:
            term = jnp.where(t >= j, term, 0.0)
        else:
            back = CONV_W - 1 - j
            prev = pe if back == 0 else pltpu.roll(pe, rows - back, axis=0)
            term = jnp.where(t >= j, term, prev)
        y = y + term * w[CONV_W - 1 - j:CONV_W - j, :]
    return y


def _ada_body(c_ref, w_ref, b_ref, o_ref, *, nseq, reps):
    cs = _silu(c_ref[...]).astype(BF16)
    m = _dot(cs, w_ref[...].astype(BF16)) + b_ref[...]
    hi = m[:nseq].astype(BF16)
    r1 = m[:nseq] - hi.astype(F32)
    mid = r1.astype(BF16)
    lo = (r1 - mid.astype(F32)).astype(BF16)
    rows = nseq * reps
    src = lax.broadcasted_iota(jnp.int32, (rows, 3 * nseq), 1)
    dst = lax.div(lax.broadcasted_iota(jnp.int32, (rows, 3 * nseq), 0), reps)
    hit = (src == dst) | (src == dst + nseq) | (src == dst + 2 * nseq)
    pick = jnp.where(hit, 1.0, 0.0).astype(BF16)
    o_ref[:rows, :] = _dot(pick, jnp.concatenate([hi, mid, lo], axis=0))
    o_ref[rows:, :] = m[nseq:]


def _ada(c_all, w_ada, b_ada, nseq, reps):
    rows = c_all.shape[0]
    out_rows = nseq * reps + rows - nseq
    n9 = 9 * D_MODEL
    return pl.pallas_call(
        functools.partial(_ada_body, nseq=nseq, reps=reps),
        grid=(DEPTH, n9 // TN_ADA),
        in_specs=[
            pl.BlockSpec((rows, D_MODEL), lambda l, n: (0, 0)),
            pl.BlockSpec((None, D_MODEL, TN_ADA), lambda l, n: (l, 0, n)),
            pl.BlockSpec((None, 1, TN_ADA), lambda l, n: (l, 0, n)),
        ],
        out_specs=pl.BlockSpec((None, out_rows, TN_ADA), lambda l, n: (l, 0, n)),
        out_shape=jax.ShapeDtypeStruct((DEPTH, out_rows, n9), F32),
        compiler_params=_cparams("parallel", "parallel"),
        name="ada",
    )(c_all, w_ada, b_ada.reshape(DEPTH, 1, n9))


class _Mod:
    def __init__(self, arr, per_token, seq_len):
        self.arr = arr
        self.per_token = per_token
        self.seq_len = seq_len

    def spec(self, layer, tm, k, **kw):
        if self.per_token:
            return pl.BlockSpec((None, tm, D_MODEL), lambda i, *_: (layer, i, k), **kw)
        seq_len = self.seq_len
        return pl.BlockSpec((None, None, 1, D_MODEL), lambda i, *_: (layer, (i * tm) // seq_len, 0, k), **kw)


def _ffn_body(x_ref, g_ref, sh_ref, sc_ref, gt_ref, w1_ref, w3_ref, w2_ref, *rest, final, emit, tf):
    rest = list(rest)
    fg_ref = rest.pop(0) if final else None
    o_ref = rest.pop(0)
    if emit:
        w1b_ref, w3b_ref, w2b_ref = rest[:3]
        rest = rest[3:]
        w1b_ref[...] = w1_ref[...].astype(BF16)
        w3b_ref[...] = w3_ref[...].astype(BF16)
        w2b_ref[...] = w2_ref[...].astype(BF16)
        w1_ref, w3_ref, w2_ref = w1b_ref, w3b_ref, w2b_ref
    h_scr, acc_scr = rest
    f = pl.program_id(1)
    last = pl.num_programs(1) - 1
    tail = D_FF - (-(-D_FF // tf) - 1) * tf

    def swiglu(width):
        h = h_scr[...]
        a = _dot(h, w1_ref[:, :width])
        b = _dot(h, w3_ref[:, :width])
        return _dot((_silu(a) * b).astype(BF16), w2_ref[:width, :])

    @pl.when(f == 0)
    def _():
        h_scr[...] = _norm_mod(x_ref[...], g_ref[...], sh_ref[...], sc_ref[...]).astype(BF16)
        acc_scr[...] = swiglu(tf)

    @pl.when((f > 0) & (f < last))
    def _():
        acc_scr[...] += swiglu(tf)

    @pl.when(f == last)
    def _():
        y = x_ref[...] + HALF * gt_ref[...] * (acc_scr[...] + swiglu(tail))
        if final:
            var = jnp.mean(y * y, axis=-1, keepdims=True)
            y = y * lax.rsqrt(var + EPS) * fg_ref[...]
        o_ref[...] = y


def _ffn(x, mod, norm_g3, w13, w2, layer, s, final_g=None, emit=False):
    m = x.shape[0]
    tm = min(TM_ROWS, m)
    tf = TF_EMIT if emit else TF_FFN
    steps = -(-D_FF // tf)
    assert steps >= 2
    sub = 2 * s
    final = final_g is not None
    w1, w3 = w13
    if emit:
        assert m == tm
        once = dict(pipeline_mode=pl.Buffered(1))
        up = pl.BlockSpec((None, None, D_MODEL, tf), lambda i, f: (layer, s, 0, f))
        down = pl.BlockSpec((None, None, tf, D_MODEL), lambda i, f: (layer, s, f, 0))
    else:
        once = {}
        up = pl.BlockSpec((D_MODEL, tf), lambda i, f: (0, f))
        down = pl.BlockSpec((tf, D_MODEL), lambda i, f: (f, 0))
    in_specs = [
        pl.BlockSpec((tm, D_MODEL), lambda i, f: (i, 0), **once),
        pl.BlockSpec((None, 1, D_MODEL), lambda i, f: (layer * 3 + sub, 0, 0)),
        mod.spec(layer, tm, 3 * sub, **once), mod.spec(layer, tm, 3 * sub + 1, **once),
        mod.spec(layer, tm, 3 * sub + 2, **once),
        up, up, down,
    ]
    args = [x, norm_g3, mod.arr, mod.arr, mod.arr, w1, w3, w2]
    if final:
        in_specs.append(pl.BlockSpec((1, D_MODEL), lambda i, f: (0, 0)))
        args.append(final_g.reshape(1, D_MODEL))
    out_specs = [pl.BlockSpec((tm, D_MODEL), lambda i, f: (i, 0))]
    out_shape = [jax.ShapeDtypeStruct((m, D_MODEL), F32)]
    if emit:
        out_specs += [pl.BlockSpec((D_MODEL, tf), lambda i, f: (0, f))] * 2 + [pl.BlockSpec((tf, D_MODEL), lambda i, f: (f, 0))]
        out_shape += [jax.ShapeDtypeStruct((D_MODEL, D_FF), BF16)] * 2 + [jax.ShapeDtypeStruct((D_FF, D_MODEL), BF16)]
    outs = pl.pallas_call(
        functools.partial(_ffn_body, final=final, emit=emit, tf=tf),
        grid=(m // tm, steps),
        in_specs=in_specs,
        out_specs=out_specs,
        out_shape=out_shape,
        scratch_shapes=[pltpu.VMEM((tm, D_MODEL), BF16), pltpu.VMEM((tm, D_MODEL), F32)],
        compiler_params=_cparams("parallel", "arbitrary"),
        name="ffn_cast" if emit else "ffn",
    )(*args)
    return (outs[0], (outs[1], outs[2]), outs[3]) if emit else outs[0]


def _inproj_body(x_ref, g_ref, sh_ref, sc_ref, w_ref, o_ref, h_scr):
    n = pl.program_id(1)

    @pl.when(n == 0)
    def _():
        h_scr[...] = _norm_mod(x_ref[...], g_ref[...], sh_ref[...], sc_ref[...]).astype(BF16)
        o_ref[...] = _dot(h_scr[...], w_ref[...])

    @pl.when(n > 0)
    def _():
        o_ref[...] = _dot(h_scr[...], w_ref[...])


def _inproj(x, mod, norm_g3, w_in, layer):
    m = x.shape[0]
    tm = min(TM_PROJ, m)
    return pl.pallas_call(
        _inproj_body,
        grid=(m // tm, N_PROJ // TN_PROJ),
        in_specs=[
            pl.BlockSpec((tm, D_MODEL), lambda i, n: (i, 0)),
            pl.BlockSpec((None, 1, D_MODEL), lambda i, n: (layer * 3 + 1, 0, 0)),
            mod.spec(layer, tm, 3), mod.spec(layer, tm, 4),
            pl.BlockSpec((None, D_MODEL, TN_PROJ), lambda i, n: (layer, 0, n)),
        ],
        out_specs=pl.BlockSpec((tm, TN_PROJ), lambda i, n: (i, n)),
        out_shape=jax.ShapeDtypeStruct((m, N_PROJ), F32),
        scratch_shapes=[pltpu.VMEM((tm, D_MODEL), BF16)],
        compiler_params=_cparams("parallel", "arbitrary"),
        name="inproj",
    )(x, norm_g3, mod.arr, mod.arr, w_in)


def _outproj_body(x_ref, oa_ref, ob_ref, gt_ref, wa_ref, wb_ref, o_ref):
    acc = _dot(oa_ref[...], wa_ref[...]) + _dot(ob_ref[...], wb_ref[...])
    o_ref[...] = x_ref[...] + gt_ref[...] * acc


def _outproj(x, oa, ob, mod, w_out, layer):
    m = x.shape[0]
    tm = min(TM_ROWS, m)
    return pl.pallas_call(
        _outproj_body,
        grid=(m // tm,),
        in_specs=[
            pl.BlockSpec((tm, D_MODEL), lambda i: (i, 0)),
            pl.BlockSpec((tm, W_A), lambda i: (i, 0)),
            pl.BlockSpec((tm, W_B), lambda i: (i, 0)),
            mod.spec(layer, tm, 5),
            pl.BlockSpec((None, W_A, D_MODEL), lambda i: (layer, 0, 0)),
            pl.BlockSpec((None, W_B, D_MODEL), lambda i: (layer, 1, 0)),
        ],
        out_specs=pl.BlockSpec((tm, D_MODEL), lambda i: (i, 0)),
        out_shape=jax.ShapeDtypeStruct((m, D_MODEL), F32),
        compiler_params=_cparams("parallel"),
        name="outproj",
    )(x, oa, ob, mod.arr, w_out, w_out)


def _lru_body(*refs, seq_len, has_state):
    if has_state:
        (x_ref, y_ref, pe_ref, h0_ref, cw_ref, cb_ref, wa_ref, wx_ref, ba_ref, bx_ref, lam_ref,
         ob_ref, hl_ref, a_scr, b_scr) = refs
    else:
        (x_ref, y_ref, cw_ref, cb_ref, wa_ref, wx_ref, ba_ref, bx_ref, lam_ref,
         ob_ref, hl_ref, a_scr, b_scr) = refs
    rows = x_ref.shape[0]
    if has_state:
        xc = _conv_rows(x_ref[...], pe_ref[...], cw_ref[...], _row_iota(x_ref.shape, seq_len))
    else:
        xc = _conv_zero_past(x_ref, cw_ref[...])
    xc = xc + cb_ref[...]
    xcb = xc.astype(BF16)
    r = jax.nn.sigmoid(_dot(xcb, wa_ref[...]) + ba_ref[...])
    i = jax.nn.sigmoid(_dot(xcb, wx_ref[...]) + bx_ref[...])
    log_a = -LRU_C * r * jax.nn.softplus(-lam_ref[...])
    a = jnp.exp(log_a)
    b = jnp.sqrt(-_expm1(2.0 * log_a)) * (i * xc)
    if has_state:
        b = b + a * h0_ref[...]
    group = min(seq_len, SUBLANES)
    tg = _row_iota(x_ref.shape, group)
    d = 1
    while d < group:
        keep = tg >= d
        b = jnp.where(keep, a * pltpu.roll(b, d, axis=0) + b, b)
        a = jnp.where(keep, a * pltpu.roll(a, d, axis=0), a)
        d *= 2
    if seq_len > SUBLANES:
        a_scr[...] = a
        b_scr[...] = b

        def step(k, carry):
            r0 = pl.multiple_of(k * SUBLANES, SUBLANES)
            hk = b_scr[pl.ds(r0, SUBLANES), :] + a_scr[pl.ds(r0, SUBLANES), :] * carry
            b_scr[pl.ds(r0, SUBLANES), :] = hk
            return hk[SUBLANES - 1:SUBLANES, :]

        lax.fori_loop(0, rows // SUBLANES, step, jnp.zeros((1, LRU_BW), F32), unroll=8)
        h = b_scr[...]
    else:
        h = b
    ob_ref[...] = (h * jax.nn.gelu(y_ref[...])).astype(BF16)
    hl_ref[...] = h[rows - hl_ref.shape[0]:, :]


def _lru(proj, seq_len, layer, lconv_w, lconv_b, wa, wx, ba, bx, lam, pe=None, h0e=None):
    m = proj.shape[0]
    has_state = pe is not None
    rows = seq_len if not has_state else m
    nseq = m // rows
    hl_rows = rows if has_state else SUBLANES
    bx0, by0 = COL_X // LRU_BW, COL_Y // LRU_BW
    col = lambda b, n: (b, n)
    vec = pl.BlockSpec((None, 1, LRU_BW), lambda b, n: (layer, 0, n))
    blk = pl.BlockSpec((None, None, LRU_BW, LRU_BW), lambda b, n: (layer, n, 0, 0))
    in_specs = [pl.BlockSpec((rows, LRU_BW), lambda b, n: (b, bx0 + n)),
                pl.BlockSpec((rows, LRU_BW), lambda b, n: (b, by0 + n))]
    args = [proj, proj]
    if has_state:
        in_specs += [pl.BlockSpec((rows, LRU_BW), col), pl.BlockSpec((rows, LRU_BW), col)]
        args += [pe, h0e]
    in_specs += [pl.BlockSpec((None, CONV_W, LRU_BW), lambda b, n: (layer, 0, n)), vec, blk, blk, vec, vec, vec]
    args += [lconv_w, lconv_b, wa, wx, ba, bx, lam]
    return pl.pallas_call(
        functools.partial(_lru_body, seq_len=seq_len, has_state=has_state),
        grid=(nseq, LRU_BLOCKS),
        in_specs=in_specs,
        out_specs=[pl.BlockSpec((rows, LRU_BW), col), pl.BlockSpec((hl_rows, LRU_BW), col)],
        out_shape=[jax.ShapeDtypeStruct((m, W_B), BF16), jax.ShapeDtypeStruct((nseq * hl_rows, W_B), F32)],
        scratch_shapes=[pltpu.VMEM((rows, LRU_BW), F32), pltpu.VMEM((rows, LRU_BW), F32)],
        compiler_params=_cparams("parallel", "parallel"),
        name="lru",
    )(*args)


def _qkv_prep(q_conv, k_conv, v_conv, heads):
    q, k, v = _silu(q_conv), _silu(k_conv), _silu(v_conv)
    qs, ks = [], []
    for h in range(heads):
        sl = slice(h * DK, (h + 1) * DK)
        qh, kh = q[:, sl], k[:, sl]
        qs.append(qh * (lax.rsqrt(jnp.sum(qh * qh, axis=-1, keepdims=True) + EPS) * DK ** -0.5))
        ks.append(kh * lax.rsqrt(jnp.sum(kh * kh, axis=-1, keepdims=True) + EPS))
    return jnp.concatenate(qs, axis=1), jnp.concatenate(ks, axis=1), v


def _gate_norm(o, onorm, gate_pre):
    var = jnp.mean(o * o, axis=-1, keepdims=True)
    return o * lax.rsqrt(var + EPS) * onorm * _silu(gate_pre)


def _seg_cumsum(g, tg, group):
    d = 1
    while d < group:
        g = jnp.where(tg >= d, g + pltpu.roll(g, d, axis=0), g)
        d *= 2
    return g


def _delta_prompt_body(q_ref, k_ref, v_ref, gt_ref, ab_ref, cwq_ref, cwk_ref, cwv_ref, alog_ref, dtb_ref,
                       on_ref, oa_ref, s_ref, qn_scr, kn_scr, vv_scr, gc_scr, bt_scr, s_scr,
                       wq_scr, uc_scr, qk_scr, kdt_scr, sd_scr):
    rows = q_ref.shape[0]
    hg = q_ref.shape[1] // DK
    qn, kn, vv = _qkv_prep(_conv_zero_past(q_ref, cwq_ref[...]), _conv_zero_past(k_ref, cwk_ref[...]),
                           _conv_zero_past(v_ref, cwv_ref[...]), hg)
    qn_scr[...] = qn
    kn_scr[...] = kn
    vv_scr[...] = vv
    ab = ab_ref[...]
    g = -jnp.exp(alog_ref[...]) * jax.nn.softplus(ab + dtb_ref[...])
    gc_scr[...] = _seg_cumsum(g, _row_iota(ab.shape, CHUNK), CHUNK)
    bt_scr[...] = jax.nn.sigmoid(ab)
    s_scr[...] = jnp.zeros_like(s_scr)
    head0 = pl.program_id(1) * hg
    onorm = on_ref[...]

    def factor(step, carry):
        lane = lax.broadcasted_iota(jnp.int32, (CHUNK, LANES), 1)
        ri = lax.broadcasted_iota(jnp.int32, (CHUNK, CHUNK), 0)
        ci = lax.broadcasted_iota(jnp.int32, (CHUNK, CHUNK), 1)
        causal = ri >= ci
        strict = ri > ci
        blk = jnp.bitwise_xor(ri, ci)
        chains = []
        for cc in range(FACTOR_CHUNKS):
            c = step * FACTOR_CHUNKS + cc
            rs = pl.ds(pl.multiple_of(c * CHUNK, CHUNK), CHUNK)
            gcc = gc_scr[rs, :]
            btc = bt_scr[rs, :]
            for hh in range(hg):
                sl = slice(hh * DK, (hh + 1) * DK)
                gcol = jnp.sum(jnp.where(lane == head0 + hh, gcc, 0.0), axis=1, keepdims=True)
                bcol = jnp.sum(jnp.where(lane == head0 + hh + H_A, btc, 0.0), axis=1, keepdims=True)
                gmat = jnp.broadcast_to(gcol, (CHUNK, CHUNK))
                decay = jnp.where(causal, jnp.exp(jnp.where(causal, gmat - gmat.T, 0.0)), 0.0)
                q = qn_scr[rs, sl]
                k = kn_scr[rs, sl]
                kq = _dot_nt(jnp.concatenate([k, q], axis=0).astype(BF16), k.astype(BF16))
                m = jnp.where(strict, bcol * kq[:CHUNK] * decay, 0.0)
                eg = jnp.exp(gcol)
                glast = jnp.broadcast_to(gcol[CHUNK - 1:CHUNK, :], (CHUNK, 1))
                wq_scr[hh, c, CHUNK:, :] = (q * eg).astype(BF16)
                qk_scr[hh, rs, :] = (kq[CHUNK:] * decay).astype(BF16)
                kdt_scr[hh, rs, :] = (k * jnp.exp(glast - gcol)).T.astype(BF16)
                sd_scr[hh, pl.ds(pl.multiple_of(c * SUBLANES, SUBLANES), SUBLANES), :] = jnp.broadcast_to(
                    jnp.broadcast_to(eg, (CHUNK, DV))[CHUNK - 1:CHUNK, :], (SUBLANES, DV))
                chains.append((hh, c, rs, sl, m, bcol, eg))
        es = [-jnp.where(lax.shift_right_logical(blk, 1) == 0, ch[4], 0.0) for ch in chains]
        lg = 1
        while (1 << lg) < CHUNK:
            cross = lax.shift_right_logical(blk, lg) == 1
            offs = [jnp.where(cross, ch[4], 0.0) for ch in chains]
            ebs = [e.astype(BF16) for e in es]
            xs = [off + _dot(eb, off.astype(BF16)) for off, eb in zip(offs, ebs)]
            es = [e - (x + _dot(x.astype(BF16), eb)) for e, x, eb in zip(es, xs, ebs)]
            lg += 1
        for (hh, c, rs, sl, m, bcol, eg), e in zip(chains, es):
            rhs = jnp.concatenate([kn_scr[rs, sl] * (bcol * eg), vv_scr[rs, sl] * bcol], axis=1)
            sol = rhs + _dot(e.astype(BF16), rhs.astype(BF16))
            wq_scr[hh, c, :CHUNK, :] = sol[:, :DK].astype(BF16)
            uc_scr[hh, rs, :] = sol[:, DK:]
        return carry

    lax.fori_loop(0, rows // (CHUNK * FACTOR_CHUNKS), factor, 0)

    def recur(c, carry):
        rs = pl.ds(pl.multiple_of(c * CHUNK, CHUNK), CHUNK)
        s_old = [s_scr[hh] for hh in range(hg)]
        ps = [_dot(wq_scr[hh, c], s_old[hh].astype(BF16)) for hh in range(hg)]
        us = [(uc_scr[hh, rs, :] - ps[hh][:CHUNK]).astype(BF16) for hh in range(hg)]
        os_ = [ps[hh][CHUNK:] + _dot(qk_scr[hh, rs, :], us[hh]) for hh in range(hg)]
        for hh in range(hg):
            sdec = sd_scr[hh, pl.ds(pl.multiple_of(c * SUBLANES, SUBLANES), SUBLANES), :]
            s_scr[hh] = s_old[hh] * sdec[:1, :] + _dot(kdt_scr[hh, rs, :], us[hh])
        for hh in range(hg):
            sl = slice(hh * DK, (hh + 1) * DK)
            oa_ref[rs, sl] = _gate_norm(os_[hh], onorm, gt_ref[rs, sl]).astype(BF16)
        return carry

    lax.fori_loop(0, rows // CHUNK, recur, 0)
    s_ref[...] = s_scr[...]


def _delta_prompt(proj, batch, seq_len, layer, dconv_w, alog_row, dtb_row, onorm):
    m = proj.shape[0]
    hg = HG_PROMPT
    wcols = hg * DK
    nchunks = seq_len // CHUNK
    assert nchunks % FACTOR_CHUNKS == 0
    nq, nk, nv, ng = (c // wcols for c in (COL_Q, COL_K, COL_V, COL_G))
    colspec = lambda off: pl.BlockSpec((seq_len, wcols), lambda b, j: (b, off + j))
    cwspec = lambda off: pl.BlockSpec((None, CONV_W, wcols), lambda b, j: (layer, 0, off + j))
    row = pl.BlockSpec((None, 1, LANES), lambda b, j: (layer, 0, 0))
    return pl.pallas_call(
        _delta_prompt_body,
        grid=(batch, H_A // hg),
        in_specs=[colspec(nq), colspec(nk), colspec(nv), colspec(ng),
                  pl.BlockSpec((seq_len, LANES), lambda b, j: (b, COL_AB // LANES)),
                  cwspec(nq), cwspec(nk), cwspec(nv), row, row, row],
        out_specs=[pl.BlockSpec((seq_len, wcols), lambda b, j: (b, j)),
                   pl.BlockSpec((None, hg, DK, DV), lambda b, j: (b, j, 0, 0))],
        out_shape=[jax.ShapeDtypeStruct((m, W_A), BF16), jax.ShapeDtypeStruct((batch, H_A, DK, DV), F32)],
        scratch_shapes=[pltpu.VMEM((seq_len, wcols), F32)] * 3
        + [pltpu.VMEM((seq_len, LANES), F32)] * 2 + [pltpu.VMEM((hg, DK, DV), F32)]
        + [pltpu.VMEM((hg, nchunks, 2 * CHUNK, DK), BF16), pltpu.VMEM((hg, seq_len, DV), F32),
           pltpu.VMEM((hg, seq_len, CHUNK), BF16), pltpu.VMEM((hg, seq_len, CHUNK), BF16),
           pltpu.VMEM((hg, nchunks * SUBLANES, DV), F32)],
        compiler_params=_cparams("parallel", "parallel"),
        name="delta_prompt",
    )(proj, proj, proj, proj, proj, dconv_w, dconv_w, dconv_w, alog_row, dtb_row, onorm)


def _delta_sample_body(q_ref, k_ref, v_ref, gt_ref, ab_ref, pe_ref, cw_ref, alog_ref, dtb_ref, on_ref, s0_ref,
                       *rest, seq_len):
    oa_ref, s_ref, qn_scr, kn_scr, vv_scr, gc_scr, bt_scr = rest[-7:]
    rows = q_ref.shape[0]
    t = _row_iota(q_ref.shape, seq_len)
    pe = pe_ref[...]
    cw = cw_ref[...]
    qn, kn, vv = _qkv_prep(_conv_rows(q_ref[...], pe[:, COL_Q:COL_K], cw[:, COL_Q:COL_K], t),
                           _conv_rows(k_ref[...], pe[:, COL_K:COL_V], cw[:, COL_K:COL_V], t),
                           _conv_rows(v_ref[...], pe[:, COL_V:COL_G], cw[:, COL_V:COL_G], t), H_A)
    qn_scr[...] = qn
    kn_scr[...] = kn
    vv_scr[...] = vv
    ab = ab_ref[...]
    g = -jnp.exp(alog_ref[...]) * jax.nn.softplus(ab + dtb_ref[...])
    gc_scr[...] = _seg_cumsum(g, _row_iota(ab.shape, seq_len), seq_len)
    bt_scr[...] = jax.nn.sigmoid(ab)
    onorm = on_ref[...]
    per_tile = SUBLANES // seq_len

    def tile(p, carry):
        r0 = pl.multiple_of(p * SUBLANES, SUBLANES)
        rs = pl.ds(r0, SUBLANES)
        gcc = gc_scr[rs, :]
        btc = bt_scr[rs, :]
        tt = _row_iota((SUBLANES, LANES), seq_len)
        tt2 = _row_iota((SUBLANES, DK + DV), seq_len)
        ri = lax.broadcasted_iota(jnp.int32, (SUBLANES, LANES), 0)
        ri2 = _row_iota((2 * SUBLANES, LANES), SUBLANES)
        lane = lax.broadcasted_iota(jnp.int32, (LANES, LANES), 1)
        zpad = jnp.zeros((LANES - SUBLANES, DK), F32)
        staged = []
        for h in range(H_A):
            sl = slice(h * DK, (h + 1) * DK)
            gc = jnp.broadcast_to(gcc[:, h:h + 1], (SUBLANES, LANES))
            beta = jnp.broadcast_to(btc[:, H_A + h:H_A + h + 1], (SUBLANES, LANES))
            q = qn_scr[rs, sl]
            k = kn_scr[rs, sl]
            v = vv_scr[rs, sl]
            eg = jnp.exp(gc)
            decs, ms = [None], [None]
            for d in range(1, seq_len):
                ok = tt >= d
                dec = jnp.where(ok, jnp.exp(jnp.where(ok, gc - pltpu.roll(gc, d, axis=0), 0.0)), 0.0)
                kk = jnp.sum(k * pltpu.roll(k, d, axis=0), axis=-1, keepdims=True)
                decs.append(dec)
                md = beta * kk * dec
                ms.append(jnp.concatenate([md, md], axis=1))
            rhs = jnp.concatenate([k * (beta * eg), v * beta], axis=1)
            sol = rhs
            for step in range(1, seq_len):
                acc = ms[1] * pltpu.roll(sol, 1, axis=0)
                for d in range(2, seq_len):
                    acc = acc + ms[d] * pltpu.roll(sol, d, axis=0)
                sol = jnp.where(tt2 == step, rhs - acc, sol)
            lhs = jnp.concatenate([sol[:, :DK], q * eg], axis=0).astype(BF16)
            staged.append((sl, gc, q, k, eg, decs, sol[:, DK:], lhs))
        pps = []
        for h in range(H_A):
            lhs = staged[h][-1]
            pp = _dot(lhs, s0_ref[p * per_tile, h].astype(BF16))
            for bb in range(1, per_tile):
                pp = jnp.where(ri2 >= bb * seq_len, _dot(lhs, s0_ref[p * per_tile + bb, h].astype(BF16)), pp)
            pps.append(pp)
        updates = []
        for h in range(H_A):
            sl, gc, q, k, eg, decs, uc, _ = staged[h]
            pp = pps[h]
            u = uc - pp[:SUBLANES]
            o = pp[SUBLANES:] + jnp.sum(q * k, axis=-1, keepdims=True) * u
            for d in range(1, seq_len):
                qk = jnp.sum(q * pltpu.roll(k, d, axis=0), axis=-1, keepdims=True)
                o = o + (qk * decs[d]) * pltpu.roll(u, d, axis=0)
            oa_ref[rs, sl] = _gate_norm(o, onorm, gt_ref[rs, sl]).astype(BF16)
            glast = jnp.broadcast_to(gc[seq_len - 1:seq_len, :], (SUBLANES, LANES))
            for bb in range(1, per_tile):
                last = bb * seq_len + seq_len - 1
                glast = jnp.where(ri >= bb * seq_len, jnp.broadcast_to(gc[last:last + 1, :], (SUBLANES, LANES)), glast)
            kd = k * jnp.exp(glast - gc)
            kdt = jnp.concatenate([kd, zpad], axis=0).T
            updates.append((kdt, jnp.concatenate([u, zpad], axis=0).astype(BF16)))
        for h in range(H_A):
            kdt, upad = updates[h]
            eg = staged[h][4]
            for bb in range(per_tile):
                last = bb * seq_len + seq_len - 1
                cols = (lane >= bb * seq_len) & (lane < (bb + 1) * seq_len)
                kd_b = jnp.where(cols, kdt, 0.0).astype(BF16)
                s_ref[p * per_tile + bb, h] = (s0_ref[p * per_tile + bb, h] * eg[last:last + 1, :]
                                               + _dot(kd_b, upad))
        return carry

    lax.fori_loop(0, rows // SUBLANES, tile, 0)


def _delta_sample(proj, seq_len, layer, pe, dconv_w, alog_row, dtb_row, onorm, s0, s_stack):
    m = proj.shape[0]
    tb = TB_SAMPLE
    rows = tb * seq_len
    colspec = lambda off: pl.BlockSpec((rows, W_A), lambda i: (i, off // W_A))
    row = pl.BlockSpec((None, 1, LANES), lambda i: (layer, 0, 0))
    sspec = pl.BlockSpec((None, tb, H_A, DK, DV), lambda i: (layer, i, 0, 0, 0))
    in_specs = [colspec(COL_Q), colspec(COL_K), colspec(COL_V), colspec(COL_G),
                pl.BlockSpec((rows, LANES), lambda i: (i, COL_AB // LANES)),
                pl.BlockSpec((rows, N_DCONV), lambda i: (i, 0)),
                pl.BlockSpec((None, CONV_W, N_DCONV), lambda i: (layer, 0, 0)),
                row, row, row, sspec]
    args = [proj, proj, proj, proj, proj, pe, dconv_w, alog_row, dtb_row, onorm, s0]
    aliases = {}
    if s_stack is not None:
        in_specs.append(pl.BlockSpec(memory_space=pl.ANY))
        aliases = {len(args): 1}
        args.append(s_stack)
    return pl.pallas_call(
        functools.partial(_delta_sample_body, seq_len=seq_len),
        grid=(m // rows,),
        in_specs=in_specs,
        out_specs=[pl.BlockSpec((rows, W_A), lambda i: (i, 0)), sspec],
        out_shape=[jax.ShapeDtypeStruct((m, W_A), BF16), jax.ShapeDtypeStruct(s0.shape, F32)],
        input_output_aliases=aliases,
        scratch_shapes=[pltpu.VMEM((rows, W_A), F32)] * 3 + [pltpu.VMEM((rows, LANES), F32)] * 2,
        compiler_params=_cparams("parallel"),
        name="delta_sample",
    )(*args)


def _pad_rows(state, seq_len):
    b, r, c = state.shape
    return jnp.pad(state, ((0, 0), (0, seq_len - r), (0, 0))).reshape(b * seq_len, c)


def _trunk(x, mod, batch, seq_len, wts, states, ffn_cast):
    (w_in, w_out, norm_g3, ffn_f32, dconv_w, alog_row, dtb_row, onorm, lconv_w, lconv_b,
     wa, wx, ba, bx, lam, final_g) = wts
    new_d, new_dc, new_l, new_lc = [], [], [], []
    s_stack = None

    def ffn(x, layer, s, fg=None):
        if (layer, s) in ffn_cast:
            return _ffn(x, mod, norm_g3, *ffn_cast[(layer, s)], layer, s, fg)
        y, w13, w2 = _ffn(x, mod, norm_g3, ffn_f32[:2], ffn_f32[2], layer, s, fg, emit=True)
        ffn_cast[(layer, s)] = (w13, w2)
        return y

    for layer in range(DEPTH):
        x = ffn(x, layer, 0)
        proj = _inproj(x, mod, norm_g3, w_in, layer)
        if states is None:
            oa, sd = _delta_prompt(proj, batch, seq_len, layer, dconv_w, alog_row, dtb_row, onorm)
            new_d.append(sd)
            ob, hl = _lru(proj, seq_len, layer, lconv_w, lconv_b, wa, wx, ba, bx, lam)
            sl = hl.reshape(batch, SUBLANES, W_B)[:, SUBLANES - 1]
        else:
            s_delta, s_dconv, s_lru, s_lconv = states
            oa, s_stack = _delta_sample(proj, seq_len, layer, _pad_rows(s_dconv[layer], seq_len),
                                        dconv_w, alog_row, dtb_row, onorm, s_delta, s_stack)
            ob, hl = _lru(proj, seq_len, layer, lconv_w, lconv_b, wa, wx, ba, bx, lam,
                          pe=_pad_rows(s_lconv[layer], seq_len),
                          h0e=_pad_rows(s_lru[layer][:, None, :], seq_len))
            sl = hl.reshape(batch, seq_len, W_B)[:, seq_len - 1]
        x = _outproj(x, oa, ob, mod, w_out, layer)
        x = ffn(x, layer, 1, final_g if layer == DEPTH - 1 else None)
        first_row = seq_len - (CONV_W - 1)
        if seq_len % SUBLANES == 0:
            ends = proj.reshape(batch, seq_len, N_PROJ)[:, first_row:]
            last_rows = lambda lo, hi: ends[:, :, lo:hi]
        else:
            last_rows = lambda lo, hi: proj[:, lo:hi].reshape(batch, seq_len, hi - lo)[:, first_row:]
        new_dc.append(last_rows(COL_Q, COL_G))
        new_l.append(sl)
        new_lc.append(last_rows(COL_X, COL_Y))
    new_delta = jnp.stack(new_d) if states is None else s_stack
    return x, new_delta, jnp.stack(new_dc), jnp.stack(new_l), jnp.stack(new_lc)


def kernel(x_prompt, x_sample, c_prompt, c_sample, state_delta, state_delta_conv, state_lru, state_lru_conv, w_in, w_out, norm_g, w_ada, b_ada, ffn_w1, ffn_w3, ffn_w2, dconv_w, d_alog, d_dtbias, d_onorm, lconv_w, lconv_b, lru_wa, lru_ba, lru_wx, lru_bx, lru_lam, final_g):
    bp, lp, _ = x_prompt.shape
    bs, ls, _ = x_sample.shape
    assert lp % CHUNK == 0 and SUBLANES % ls == 0 and bp <= SUBLANES

    w_in_r = jnp.concatenate(
        [w_in[..., :OFF_A], w_in[..., OFF_X:], w_in[..., OFF_A:OFF_X],
         jnp.zeros((DEPTH, D_MODEL, N_PROJ - N_IN), w_in.dtype)], axis=-1).astype(BF16)
    w_out_b = w_out.astype(BF16)
    wa = lru_wa.astype(BF16)
    wx = lru_wx.astype(BF16)
    norm_g3 = norm_g.reshape(DEPTH * 3, 1, D_MODEL)
    lane_row = lambda v: jnp.pad(v, ((0, 0), (0, LANES - v.shape[1]))).reshape(DEPTH, 1, LANES)
    vec = lambda v: v.reshape(DEPTH, 1, W_B)
    wts = (w_in_r, w_out_b, norm_g3, (ffn_w1, ffn_w3, ffn_w2), dconv_w, lane_row(d_alog), lane_row(d_dtbias),
           d_onorm.reshape(DEPTH, 1, DV), lconv_w, vec(lconv_b), wa, wx, vec(lru_ba), vec(lru_bx),
           vec(lru_lam), final_g)

    ms = bs * ls
    c_all = jnp.concatenate([c_sample, c_prompt, jnp.zeros((SUBLANES - bp, D_MODEL), F32)], axis=0)
    mod_all = _ada(c_all, w_ada, b_ada, bs, ls)
    mod_p = _Mod(mod_all[:, ms:ms + bp].reshape(DEPTH, bp, 1, 9 * D_MODEL), False, lp)
    mod_s = _Mod(mod_all, True, ls)

    ffn_cast = {}
    ys, sd, sdc, sl_, slc = _trunk(x_sample.reshape(bs * ls, D_MODEL), mod_s, bs, ls, wts,
                                   (state_delta, state_delta_conv, state_lru, state_lru_conv), ffn_cast)
    yp, pd, pdc, pl_, plc = _trunk(x_prompt.reshape(bp * lp, D_MODEL), mod_p, bp, lp, wts, None, ffn_cast)
    return (yp.reshape(bp, lp, D_MODEL), ys.reshape(bs, ls, D_MODEL), pd, pdc, pl_, plc, sd, sdc, sl_, slc)
```

```python
import functools

import jax
import jax.numpy as jnp
from jax import lax
from jax.experimental import pallas as pl
from jax.experimental.pallas import tpu as pltpu

F32 = jnp.float32
BF16 = jnp.bfloat16

D_MODEL = 2048
DEPTH = 4
H_A = 8
DK = 128
DV = 128
W_QK = H_A * DK
W_A = H_A * DV
W_B = D_MODEL - W_A
LRU_BLOCKS = 8
LRU_BW = W_B // LRU_BLOCKS
LRU_C = 8.0
CONV_W = 4
D_FF = ((8 * D_MODEL // 3 + 127) // 128) * 128
HALF = 0.5
EPS = 1e-6
N_DCONV = 2 * W_QK + W_A
OFF_G = 2 * W_QK + W_A
OFF_A = OFF_G + W_A
OFF_X = OFF_A + 2 * H_A
N_IN = OFF_X + 2 * W_B

LANES = 128
SUBLANES = 8
VMEM_LIMIT = 56 * 1024 * 1024

COL_Q, COL_K, COL_V, COL_G = 0, W_QK, 2 * W_QK, 2 * W_QK + W_A
COL_X = COL_G + W_A
COL_Y = COL_X + W_B
COL_AB = COL_Y + W_B
N_PROJ = COL_AB + 2 * LANES

TM_ROWS = 512
TM_PROJ = 1024
TF_FFN = 512
TF_EMIT = 256
TN_PROJ = 1280
TN_ADA = 1024
CHUNK = 128
HG_PROMPT = 2
FACTOR_CHUNKS = 8
TB_SAMPLE = 16

assert N_PROJ % TN_PROJ == 0 and (9 * D_MODEL) % TN_ADA == 0 and D_FF % LANES == 0


def _cparams(*sem):
    return pltpu.CompilerParams(dimension_semantics=sem, vmem_limit_bytes=VMEM_LIMIT)


def _dot(a, b):
    return jnp.dot(a, b, preferred_element_type=F32)


def _dot_nt(a, b):
    return lax.dot_general(a, b, (((1,), (1,)), ((), ())), preferred_element_type=F32)


def _silu(x):
    return x * jax.nn.sigmoid(x)


def _expm1(x):
    u = jnp.exp(x)
    um1 = u - 1.0
    safe = um1 * x / jnp.log(jnp.where(um1 == 0.0, 2.0, u))
    return jnp.where(um1 == 0.0, x, jnp.where(x < -30.0, um1, safe))


def _norm_mod(x, g, shift, scale):
    var = jnp.mean(x * x, axis=-1, keepdims=True)
    return x * lax.rsqrt(var + EPS) * (g * (1.0 + scale)) + shift


def _row_iota(shape, mod):
    return jnp.bitwise_and(lax.broadcasted_iota(jnp.int32, shape, 0), mod - 1)


def _conv_zero_past(x_ref, w):
    rows = x_ref.shape[0]
    head = x_ref[:SUBLANES, :]
    head_y = _conv_rows(head, None, w, lax.broadcasted_iota(jnp.int32, head.shape, 0))
    y = x_ref[SUBLANES:, :] * w[CONV_W - 1:CONV_W, :]
    for j in range(1, CONV_W):
        y = y + x_ref[SUBLANES - j:rows - j, :] * w[CONV_W - 1 - j:CONV_W - j, :]
    return jnp.concatenate([head_y, y], axis=0)


def _conv_rows(x, pe, w, t):
    rows = x.shape[0]
    y = x * w[CONV_W - 1:CONV_W, :]
    for j in range(1, CONV_W):
        term = pltpu.roll(x, j, axis=0)
        if pe is None:
            term = jnp.where(t >= j, term, 0.0)
        else:
            back = CONV_W - 1 - j
            prev = pe if back == 0 else pltpu.roll(pe, rows - back, axis=0)
            term = jnp.where(t >= j, term, prev)
        y = y + term * w[CONV_W - 1 - j:CONV_W - j, :]
    return y


def _ada_body(c_ref, w_ref, b_ref, o_ref, *, nseq, reps):
    cs = _silu(c_ref[...]).astype(BF16)
    m = _dot(cs, w_ref[...].astype(BF16)) + b_ref[...]
    hi = m[:nseq].astype(BF16)
    r1 = m[:nseq] - hi.astype(F32)
    mid = r1.astype(BF16)
    lo = (r1 - mid.astype(F32)).astype(BF16)
    rows = nseq * reps
    src = lax.broadcasted_iota(jnp.int32, (rows, 3 * nseq), 1)
    dst = lax.div(lax.broadcasted_iota(jnp.int32, (rows, 3 * nseq), 0), reps)
    hit = (src == dst) | (src == dst + nseq) | (src == dst + 2 * nseq)
    pick = jnp.where(hit, 1.0, 0.0).astype(BF16)
    o_ref[:rows, :] = _dot(pick, jnp.concatenate([hi, mid, lo], axis=0))
    o_ref[rows:, :] = m[nseq:]


def _ada(c_all, w_ada, b_ada, nseq, reps):
    rows = c_all.shape[0]
    out_rows = nseq * reps + rows - nseq
    n9 = 9 * D_MODEL
    return pl.pallas_call(
        functools.partial(_ada_body, nseq=nseq, reps=reps),
        grid=(DEPTH, n9 // TN_ADA),
        in_specs=[
            pl.BlockSpec((rows, D_MODEL), lambda l, n: (0, 0)),
            pl.BlockSpec((None, D_MODEL, TN_ADA), lambda l, n: (l, 0, n)),
            pl.BlockSpec((None, 1, TN_ADA), lambda l, n: (l, 0, n)),
        ],
        out_specs=pl.BlockSpec((None, out_rows, TN_ADA), lambda l, n: (l, 0, n)),
        out_shape=jax.ShapeDtypeStruct((DEPTH, out_rows, n9), F32),
        compiler_params=_cparams("parallel", "parallel"),
        name="ada",
    )(c_all, w_ada, b_ada.reshape(DEPTH, 1, n9))


class _Mod:
    def __init__(self, arr, per_token, seq_len):
        self.arr = arr
        self.per_token = per_token
        self.seq_len = seq_len

    def spec(self, layer, tm, k, **kw):
        if self.per_token:
            return pl.BlockSpec((None, tm, D_MODEL), lambda i, *_: (layer, i, k), **kw)
        seq_len = self.seq_len
        return pl.BlockSpec((None, None, 1, D_MODEL), lambda i, *_: (layer, (i * tm) // seq_len, 0, k), **kw)


def _ffn_body(x_ref, g_ref, sh_ref, sc_ref, gt_ref, w1_ref, w3_ref, w2_ref, *rest, final, emit, tf):
    rest = list(rest)
    fg_ref = rest.pop(0) if final else None
    o_ref = rest.pop(0)
    if emit:
        w1b_ref, w3b_ref, w2b_ref = rest[:3]
        rest = rest[3:]
        w1b_ref[...] = w1_ref[...].astype(BF16)
        w3b_ref[...] = w3_ref[...].astype(BF16)
        w2b_ref[...] = w2_ref[...].astype(BF16)
        w1_ref, w3_ref, w2_ref = w1b_ref, w3b_ref, w2b_ref
    h_scr, acc_scr = rest
    f = pl.program_id(1)
    last = pl.num_programs(1) - 1
    tail = D_FF - (-(-D_FF // tf) - 1) * tf

    def swiglu(width):
        h = h_scr[...]
        a = _dot(h, w1_ref[:, :width])
        b = _dot(h, w3_ref[:, :width])
        return _dot((_silu(a) * b).astype(BF16), w2_ref[:width, :])

    @pl.when(f == 0)
    def _():
        h_scr[...] = _norm_mod(x_ref[...], g_ref[...], sh_ref[...], sc_ref[...]).astype(BF16)
        acc_scr[...] = swiglu(tf)

    @pl.when((f > 0) & (f < last))
    def _():
        acc_scr[...] += swiglu(tf)

    @pl.when(f == last)
    def _():
        y = x_ref[...] + HALF * gt_ref[...] * (acc_scr[...] + swiglu(tail))
        if final:
            var = jnp.mean(y * y, axis=-1, keepdims=True)
            y = y * lax.rsqrt(var + EPS) * fg_ref[...]
        o_ref[...] = y


def _ffn(x, mod, norm_g3, w13, w2, layer, s, final_g=None, emit=False):
    m = x.shape[0]
    tm = min(TM_ROWS, m)
    tf = TF_EMIT if emit else TF_FFN
    steps = -(-D_FF // tf)
    assert steps >= 2
    sub = 2 * s
    final = final_g is not None
    w1, w3 = w13
    if emit:
        assert m == tm
        once = dict(pipeline_mode=pl.Buffered(1))
        up = pl.BlockSpec((None, None, D_MODEL, tf), lambda i, f: (layer, s, 0, f))
        down = pl.BlockSpec((None, None, tf, D_MODEL), lambda i, f: (layer, s, f, 0))
    else:
        once = {}
        up = pl.BlockSpec((D_MODEL, tf), lambda i, f: (0, f))
        down = pl.BlockSpec((tf, D_MODEL), lambda i, f: (f, 0))
    in_specs = [
        pl.BlockSpec((tm, D_MODEL), lambda i, f: (i, 0), **once),
        pl.BlockSpec((None, 1, D_MODEL), lambda i, f: (layer * 3 + sub, 0, 0)),
        mod.spec(layer, tm, 3 * sub, **once), mod.spec(layer, tm, 3 * sub + 1, **once),
        mod.spec(layer, tm, 3 * sub + 2, **once),
        up, up, down,
    ]
    args = [x, norm_g3, mod.arr, mod.arr, mod.arr, w1, w3, w2]
    if final:
        in_specs.append(pl.BlockSpec((1, D_MODEL), lambda i, f: (0, 0)))
        args.append(final_g.reshape(1, D_MODEL))
    out_specs = [pl.BlockSpec((tm, D_MODEL), lambda i, f: (i, 0))]
    out_shape = [jax.ShapeDtypeStruct((m, D_MODEL), F32)]
    if emit:
        out_specs += [pl.BlockSpec((D_MODEL, tf), lambda i, f: (0, f))] * 2 + [pl.BlockSpec((tf, D_MODEL), lambda i, f: (f, 0))]
        out_shape += [jax.ShapeDtypeStruct((D_MODEL, D_FF), BF16)] * 2 + [jax.ShapeDtypeStruct((D_FF, D_MODEL), BF16)]
    outs = pl.pallas_call(
        functools.partial(_ffn_body, final=final, emit=emit, tf=tf),
        grid=(m // tm, steps),
        in_specs=in_specs,
        out_specs=out_specs,
        out_shape=out_shape,
        scratch_shapes=[pltpu.VMEM((tm, D_MODEL), BF16), pltpu.VMEM((tm, D_MODEL), F32)],
        compiler_params=_cparams("parallel", "arbitrary"),
        name="ffn_cast" if emit else "ffn",
    )(*args)
    return (outs[0], (outs[1], outs[2]), outs[3]) if emit else outs[0]


def _inproj_body(x_ref, g_ref, sh_ref, sc_ref, w_ref, o_ref, h_scr):
    n = pl.program_id(1)

    @pl.when(n == 0)
    def _():
        h_scr[...] = _norm_mod(x_ref[...], g_ref[...], sh_ref[...], sc_ref[...]).astype(BF16)
        o_ref[...] = _dot(h_scr[...], w_ref[...])

    @pl.when(n > 0)
    def _():
        o_ref[...] = _dot(h_scr[...], w_ref[...])


def _inproj(x, mod, norm_g3, w_in, layer):
    m = x.shape[0]
    tm = min(TM_PROJ, m)
    return pl.pallas_call(
        _inproj_body,
        grid=(m // tm, N_PROJ // TN_PROJ),
        in_specs=[
            pl.BlockSpec((tm, D_MODEL), lambda i, n: (i, 0)),
            pl.BlockSpec((None, 1, D_MODEL), lambda i, n: (layer * 3 + 1, 0, 0)),
            mod.spec(layer, tm, 3), mod.spec(layer, tm, 4),
            pl.BlockSpec((None, D_MODEL, TN_PROJ), lambda i, n: (layer, 0, n)),
        ],
        out_specs=pl.BlockSpec((tm, TN_PROJ), lambda i, n: (i, n)),
        out_shape=jax.ShapeDtypeStruct((m, N_PROJ), F32),
        scratch_shapes=[pltpu.VMEM((tm, D_MODEL), BF16)],
        compiler_params=_cparams("parallel", "arbitrary"),
        name="inproj",
    )(x, norm_g3, mod.arr, mod.arr, w_in)


def _outproj_body(x_ref, oa_ref, ob_ref, gt_ref, wa_ref, wb_ref, o_ref):
    acc = _dot(oa_ref[...], wa_ref[...]) + _dot(ob_ref[...], wb_ref[...])
    o_ref[...] = x_ref[...] + gt_ref[...] * acc


def _outproj(x, oa, ob, mod, w_out, layer):
    m = x.shape[0]
    tm = min(TM_ROWS, m)
    return pl.pallas_call(
        _outproj_body,
        grid=(m // tm,),
        in_specs=[
            pl.BlockSpec((tm, D_MODEL), lambda i: (i, 0)),
            pl.BlockSpec((tm, W_A), lambda i: (i, 0)),
            pl.BlockSpec((tm, W_B), lambda i: (i, 0)),
            mod.spec(layer, tm, 5),
            pl.BlockSpec((None, W_A, D_MODEL), lambda i: (layer, 0, 0)),
            pl.BlockSpec((None, W_B, D_MODEL), lambda i: (layer, 1, 0)),
        ],
        out_specs=pl.BlockSpec((tm, D_MODEL), lambda i: (i, 0)),
        out_shape=jax.ShapeDtypeStruct((m, D_MODEL), F32),
        compiler_params=_cparams("parallel"),
        name="outproj",
    )(x, oa, ob, mod.arr, w_out, w_out)


def _lru_body(*refs, seq_len, has_state):
    if has_state:
        (x_ref, y_ref, pe_ref, h0_ref, cw_ref, cb_ref, wa_ref, wx_ref, ba_ref, bx_ref, lam_ref,
         ob_ref, hl_ref, a_scr, b_scr) = refs
    else:
        (x_ref, y_ref, cw_ref, cb_ref, wa_ref, wx_ref, ba_ref, bx_ref, lam_ref,
         ob_ref, hl_ref, a_scr, b_scr) = refs
    rows = x_ref.shape[0]
    if has_state:
        xc = _conv_rows(x_ref[...], pe_ref[...], cw_ref[...], _row_iota(x_ref.shape, seq_len))
    else:
        xc = _conv_zero_past(x_ref, cw_ref[...])
    xc = xc + cb_ref[...]
    xcb = xc.astype(BF16)
    r = jax.nn.sigmoid(_dot(xcb, wa_ref[...]) + ba_ref[...])
    i = jax.nn.sigmoid(_dot(xcb, wx_ref[...]) + bx_ref[...])
    log_a = -LRU_C * r * jax.nn.softplus(-lam_ref[...])
    a = jnp.exp(log_a)
    b = jnp.sqrt(-_expm1(2.0 * log_a)) * (i * xc)
    if has_state:
        b = b + a * h0_ref[...]
    group = min(seq_len, SUBLANES)
    tg = _row_iota(x_ref.shape, group)
    d = 1
    while d < group:
        keep = tg >= d
        b = jnp.where(keep, a * pltpu.roll(b, d, axis=0) + b, b)
        a = jnp.where(keep, a * pltpu.roll(a, d, axis=0), a)
        d *= 2
    if seq_len > SUBLANES:
        a_scr[...] = a
        b_scr[...] = b

        def step(k, carry):
            r0 = pl.multiple_of(k * SUBLANES, SUBLANES)
            hk = b_scr[pl.ds(r0, SUBLANES), :] + a_scr[pl.ds(r0, SUBLANES), :] * carry
            b_scr[pl.ds(r0, SUBLANES), :] = hk
            return hk[SUBLANES - 1:SUBLANES, :]

        lax.fori_loop(0, rows // SUBLANES, step, jnp.zeros((1, LRU_BW), F32), unroll=8)
        h = b_scr[...]
    else:
        h = b
    ob_ref[...] = (h * jax.nn.gelu(y_ref[...])).astype(BF16)
    hl_ref[...] = h[rows - hl_ref.shape[0]:, :]


def _lru(proj, seq_len, layer, lconv_w, lconv_b, wa, wx, ba, bx, lam, pe=None, h0e=None):
    m = proj.shape[0]
    has_state = pe is not None
    rows = seq_len if not has_state else m
    nseq = m // rows
    hl_rows = rows if has_state else SUBLANES
    bx0, by0 = COL_X // LRU_BW, COL_Y // LRU_BW
    col = lambda b, n: (b, n)
    vec = pl.BlockSpec((None, 1, LRU_BW), lambda b, n: (layer, 0, n))
    blk = pl.BlockSpec((None, None, LRU_BW, LRU_BW), lambda b, n: (layer, n, 0, 0))
    in_specs = [pl.BlockSpec((rows, LRU_BW), lambda b, n: (b, bx0 + n)),
                pl.BlockSpec((rows, LRU_BW), lambda b, n: (b, by0 + n))]
    args = [proj, proj]
    if has_state:
        in_specs += [pl.BlockSpec((rows, LRU_BW), col), pl.BlockSpec((rows, LRU_BW), col)]
        args += [pe, h0e]
    in_specs += [pl.BlockSpec((None, CONV_W, LRU_BW), lambda b, n: (layer, 0, n)), vec, blk, blk, vec, vec, vec]
    args += [lconv_w, lconv_b, wa, wx, ba, bx, lam]
    return pl.pallas_call(
        functools.partial(_lru_body, seq_len=seq_len, has_state=has_state),
        grid=(nseq, LRU_BLOCKS),
        in_specs=in_specs,
        out_specs=[pl.BlockSpec((rows, LRU_BW), col), pl.BlockSpec((hl_rows, LRU_BW), col)],
        out_shape=[jax.ShapeDtypeStruct((m, W_B), BF16), jax.ShapeDtypeStruct((nseq * hl_rows, W_B), F32)],
        scratch_shapes=[pltpu.VMEM((rows, LRU_BW), F32), pltpu.VMEM((rows, LRU_BW), F32)],
        compiler_params=_cparams("parallel", "parallel"),
        name="lru",
    )(*args)


def _qkv_prep(q_conv, k_conv, v_conv, heads):
    q, k, v = _silu(q_conv), _silu(k_conv), _silu(v_conv)
    qs, ks = [], []
    for h in range(heads):
        sl = slice(h * DK, (h + 1) * DK)
        qh, kh = q[:, sl], k[:, sl]
        qs.append(qh * (lax.rsqrt(jnp.sum(qh * qh, axis=-1, keepdims=True) + EPS) * DK ** -0.5))
        ks.append(kh * lax.rsqrt(jnp.sum(kh * kh, axis=-1, keepdims=True) + EPS))
    return jnp.concatenate(qs, axis=1), jnp.concatenate(ks, axis=1), v


def _gate_norm(o, onorm, gate_pre):
    var = jnp.mean(o * o, axis=-1, keepdims=True)
    return o * lax.rsqrt(var + EPS) * onorm * _silu(gate_pre)


def _seg_cumsum(g, tg, group):
    d = 1
    while d < group:
        g = jnp.where(tg >= d, g + pltpu.roll(g, d, axis=0), g)
        d *= 2
    return g


def _delta_prompt_body(q_ref, k_ref, v_ref, gt_ref, ab_ref, cwq_ref, cwk_ref, cwv_ref, alog_ref, dtb_ref,
                       on_ref, oa_ref, s_ref, qn_scr, kn_scr, vv_scr, gc_scr, bt_scr, s_scr,
                       wq_scr, uc_scr, qk_scr, kdt_scr, sd_scr):
    rows = q_ref.shape[0]
    hg = q_ref.shape[1] // DK
    qn, kn, vv = _qkv_prep(_conv_zero_past(q_ref, cwq_ref[...]), _conv_zero_past(k_ref, cwk_ref[...]),
                           _conv_zero_past(v_ref, cwv_ref[...]), hg)
    qn_scr[...] = qn
    kn_scr[...] = kn
    vv_scr[...] = vv
    ab = ab_ref[...]
    g = -jnp.exp(alog_ref[...]) * jax.nn.softplus(ab + dtb_ref[...])
    gc_scr[...] = _seg_cumsum(g, _row_iota(ab.shape, CHUNK), CHUNK)
    bt_scr[...] = jax.nn.sigmoid(ab)
    s_scr[...] = jnp.zeros_like(s_scr)
    head0 = pl.program_id(1) * hg
    onorm = on_ref[...]

    def factor(step, carry):
        lane = lax.broadcasted_iota(jnp.int32, (CHUNK, LANES), 1)
        ri = lax.broadcasted_iota(jnp.int32, (CHUNK, CHUNK), 0)
        ci = lax.broadcasted_iota(jnp.int32, (CHUNK, CHUNK), 1)
        causal = ri >= ci
        strict = ri > ci
        blk = jnp.bitwise_xor(ri, ci)
        chains = []
        for cc in range(FACTOR_CHUNKS):
            c = step * FACTOR_CHUNKS + cc
            rs = pl.ds(pl.multiple_of(c * CHUNK, CHUNK), CHUNK)
            gcc = gc_scr[rs, :]
            btc = bt_scr[rs, :]
            for hh in range(hg):
                sl = slice(hh * DK, (hh + 1) * DK)
                gcol = jnp.sum(jnp.where(lane == head0 + hh, gcc, 0.0), axis=1, keepdims=True)
                bcol = jnp.sum(jnp.where(lane == head0 + hh + H_A, btc, 0.0), axis=1, keepdims=True)
                gmat = jnp.broadcast_to(gcol, (CHUNK, CHUNK))
                decay = jnp.where(causal, jnp.exp(jnp.where(causal, gmat - gmat.T, 0.0)), 0.0)
                q = qn_scr[rs, sl]
                k = kn_scr[rs, sl]
                kq = _dot_nt(jnp.concatenate([k, q], axis=0).astype(BF16), k.astype(BF16))
                m = jnp.where(strict, bcol * kq[:CHUNK] * decay, 0.0)
                eg = jnp.exp(gcol)
                glast = jnp.broadcast_to(gcol[CHUNK - 1:CHUNK, :], (CHUNK, 1))
                wq_scr[hh, c, CHUNK:, :] = (q * eg).astype(BF16)
                qk_scr[hh, rs, :] = (kq[CHUNK:] * decay).astype(BF16)
                kdt_scr[hh, rs, :] = (k * jnp.exp(glast - gcol)).T.astype(BF16)
                sd_scr[hh, pl.ds(pl.multiple_of(c * SUBLANES, SUBLANES), SUBLANES), :] = jnp.broadcast_to(
                    jnp.broadcast_to(eg, (CHUNK, DV))[CHUNK - 1:CHUNK, :], (SUBLANES, DV))
                chains.append((hh, c, rs, sl, m, bcol, eg))
        es = [-jnp.where(lax.shift_right_logical(blk, 1) == 0, ch[4], 0.0) for ch in chains]
        lg = 1
        while (1 << lg) < CHUNK:
            cross = lax.shift_right_logical(blk, lg) == 1
            offs = [jnp.where(cross, ch[4], 0.0) for ch in chains]
            ebs = [e.astype(BF16) for e in es]
            xs = [off + _dot(eb, off.astype(BF16)) for off, eb in zip(offs, ebs)]
            es = [e - (x + _dot(x.astype(BF16), eb)) for e, x, eb in zip(es, xs, ebs)]
            lg += 1
        for (hh, c, rs, sl, m, bcol, eg), e in zip(chains, es):
            rhs = jnp.concatenate([kn_scr[rs, sl] * (bcol * eg), vv_scr[rs, sl] * bcol], axis=1)
            sol = rhs + _dot(e.astype(BF16), rhs.astype(BF16))
            wq_scr[hh, c, :CHUNK, :] = sol[:, :DK].astype(BF16)
            uc_scr[hh, rs, :] = sol[:, DK:]
        return carry

    lax.fori_loop(0, rows // (CHUNK * FACTOR_CHUNKS), factor, 0)

    def recur(c, carry):
        rs = pl.ds(pl.multiple_of(c * CHUNK, CHUNK), CHUNK)
        s_old = [s_scr[hh] for hh in range(hg)]
        ps = [_dot(wq_scr[hh, c], s_old[hh].astype(BF16)) for hh in range(hg)]
        us = [(uc_scr[hh, rs, :] - ps[hh][:CHUNK]).astype(BF16) for hh in range(hg)]
        os_ = [ps[hh][CHUNK:] + _dot(qk_scr[hh, rs, :], us[hh]) for hh in range(hg)]
        for hh in range(hg):
            sdec = sd_scr[hh, pl.ds(pl.multiple_of(c * SUBLANES, SUBLANES), SUBLANES), :]
            s_scr[hh] = s_old[hh] * sdec[:1, :] + _dot(kdt_scr[hh, rs, :], us[hh])
        for hh in range(hg):
            sl = slice(hh * DK, (hh + 1) * DK)
            oa_ref[rs, sl] = _gate_norm(os_[hh], onorm, gt_ref[rs, sl]).astype(BF16)
        return carry

    lax.fori_loop(0, rows // CHUNK, recur, 0)
    s_ref[...] = s_scr[...]


def _delta_prompt(proj, batch, seq_len, layer, dconv_w, alog_row, dtb_row, onorm):
    m = proj.shape[0]
    hg = HG_PROMPT
    wcols = hg * DK
    nchunks = seq_len // CHUNK
    assert nchunks % FACTOR_CHUNKS == 0
    nq, nk, nv, ng = (c // wcols for c in (COL_Q, COL_K, COL_V, COL_G))
    colspec = lambda off: pl.BlockSpec((seq_len, wcols), lambda b, j: (b, off + j))
    cwspec = lambda off: pl.BlockSpec((None, CONV_W, wcols), lambda b, j: (layer, 0, off + j))
    row = pl.BlockSpec((None, 1, LANES), lambda b, j: (layer, 0, 0))
    return pl.pallas_call(
        _delta_prompt_body,
        grid=(batch, H_A // hg),
        in_specs=[colspec(nq), colspec(nk), colspec(nv), colspec(ng),
                  pl.BlockSpec((seq_len, LANES), lambda b, j: (b, COL_AB // LANES)),
                  cwspec(nq), cwspec(nk), cwspec(nv), row, row, row],
        out_specs=[pl.BlockSpec((seq_len, wcols), lambda b, j: (b, j)),
                   pl.BlockSpec((None, hg, DK, DV), lambda b, j: (b, j, 0, 0))],
        out_shape=[jax.ShapeDtypeStruct((m, W_A), BF16), jax.ShapeDtypeStruct((batch, H_A, DK, DV), F32)],
        scratch_shapes=[pltpu.VMEM((seq_len, wcols), F32)] * 3
        + [pltpu.VMEM((seq_len, LANES), F32)] * 2 + [pltpu.VMEM((hg, DK, DV), F32)]
        + [pltpu.VMEM((hg, nchunks, 2 * CHUNK, DK), BF16), pltpu.VMEM((hg, seq_len, DV), F32),
           pltpu.VMEM((hg, seq_len, CHUNK), BF16), pltpu.VMEM((hg, seq_len, CHUNK), BF16),
           pltpu.VMEM((hg, nchunks * SUBLANES, DV), F32)],
        compiler_params=_cparams("parallel", "parallel"),
        name="delta_prompt",
    )(proj, proj, proj, proj, proj, dconv_w, dconv_w, dconv_w, alog_row, dtb_row, onorm)


def _delta_sample_body(q_ref, k_ref, v_ref, gt_ref, ab_ref, pe_ref, cw_ref, alog_ref, dtb_ref, on_ref, s0_ref,
                       *rest, seq_len):
    oa_ref, s_ref, qn_scr, kn_scr, vv_scr, gc_scr, bt_scr = rest[-7:]
    rows = q_ref.shape[0]
    t = _row_iota(q_ref.shape, seq_len)
    pe = pe_ref[...]
    cw = cw_ref[...]
    qn, kn, vv = _qkv_prep(_conv_rows(q_ref[...], pe[:, COL_Q:COL_K], cw[:, COL_Q:COL_K], t),
                           _conv_rows(k_ref[...], pe[:, COL_K:COL_V], cw[:, COL_K:COL_V], t),
                           _conv_rows(v_ref[...], pe[:, COL_V:COL_G], cw[:, COL_V:COL_G], t), H_A)
    qn_scr[...] = qn
    kn_scr[...] = kn
    vv_scr[...] = vv
    ab = ab_ref[...]
    g = -jnp.exp(alog_ref[...]) * jax.nn.softplus(ab + dtb_ref[...])
    gc_scr[...] = _seg_cumsum(g, _row_iota(ab.shape, seq_len), seq_len)
    bt_scr[...] = jax.nn.sigmoid(ab)
    onorm = on_ref[...]
    per_tile = SUBLANES // seq_len

    def tile(p, carry):
        tt = _row_iota((SUBLANES, LANES), seq_len)
        tt2 = _row_iota((SUBLANES, DK + DV), seq_len)
        ri = lax.broadcasted_iota(jnp.int32, (SUBLANES, LANES), 0)
        ri2 = _row_iota((2 * SUBLANES, LANES), SUBLANES)
        lane = lax.broadcasted_iota(jnp.int32, (LANES, LANES), 1)
        zpad = jnp.zeros((LANES - SUBLANES, DK), F32)
        rs = pl.ds(pl.multiple_of(p * SUBLANES, SUBLANES), SUBLANES)
        gcc, btc = gc_scr[rs, :], bt_scr[rs, :]
        work = [(p, rs, h, gcc, btc) for h in range(H_A)]
        staged = []
        for p, rs, h, gcc, btc in work:
            sl = slice(h * DK, (h + 1) * DK)
            gc = jnp.broadcast_to(gcc[:, h:h + 1], (SUBLANES, LANES))
            beta = jnp.broadcast_to(btc[:, H_A + h:H_A + h + 1], (SUBLANES, LANES))
            q = qn_scr[rs, sl]
            k = kn_scr[rs, sl]
            v = vv_scr[rs, sl]
            eg = jnp.exp(gc)
            decs, ms = [None], [None]
            for d in range(1, seq_len):
                ok = tt >= d
                dec = jnp.where(ok, jnp.exp(jnp.where(ok, gc - pltpu.roll(gc, d, axis=0), 0.0)), 0.0)
                kk = jnp.sum(k * pltpu.roll(k, d, axis=0), axis=-1, keepdims=True)
                decs.append(dec)
                md = beta * kk * dec
                ms.append(jnp.concatenate([md, md], axis=1))
            rhs = jnp.concatenate([k * (beta * eg), v * beta], axis=1)
            sol = rhs
            for step in range(1, seq_len):
                acc = ms[1] * pltpu.roll(sol, 1, axis=0)
                for d in range(2, seq_len):
                    acc = acc + ms[d] * pltpu.roll(sol, d, axis=0)
                sol = jnp.where(tt2 == step, rhs - acc, sol)
            lhs = jnp.concatenate([sol[:, :DK], q * eg], axis=0).astype(BF16)
            staged.append((sl, gc, q, k, eg, decs, sol[:, DK:], lhs))
        pps = []
        for (p, rs, h, _, _), st in zip(work, staged):
            lhs = st[-1]
            pp = _dot(lhs, s0_ref[p * per_tile, h].astype(BF16))
            for bb in range(1, per_tile):
                pp = jnp.where(ri2 >= bb * seq_len, _dot(lhs, s0_ref[p * per_tile + bb, h].astype(BF16)), pp)
            pps.append(pp)
        updates = []
        for (p, rs, h, _, _), st, pp in zip(work, staged, pps):
            sl, gc, q, k, eg, decs, uc, _ = st
            u = uc - pp[:SUBLANES]
            o = pp[SUBLANES:] + jnp.sum(q * k, axis=-1, keepdims=True) * u
            for d in range(1, seq_len):
                qk = jnp.sum(q * pltpu.roll(k, d, axis=0), axis=-1, keepdims=True)
                o = o + (qk * decs[d]) * pltpu.roll(u, d, axis=0)
            oa_ref[rs, sl] = _gate_norm(o, onorm, gt_ref[rs, sl]).astype(BF16)
            glast = jnp.broadcast_to(gc[seq_len - 1:seq_len, :], (SUBLANES, LANES))
            for bb in range(1, per_tile):
                last = bb * seq_len + seq_len - 1
                glast = jnp.where(ri >= bb * seq_len, jnp.broadcast_to(gc[last:last + 1, :], (SUBLANES, LANES)), glast)
            kd = k * jnp.exp(glast - gc)
            kdt = jnp.concatenate([kd, zpad], axis=0).T
            updates.append((kdt, jnp.concatenate([u, zpad], axis=0).astype(BF16)))
        for (p, rs, h, _, _), st, (kdt, upad) in zip(work, staged, updates):
            eg = st[4]
            for bb in range(per_tile):
                last = bb * seq_len + seq_len - 1
                cols = (lane >= bb * seq_len) & (lane < (bb + 1) * seq_len)
                kd_b = jnp.where(cols, kdt, 0.0).astype(BF16)
                s_ref[p * per_tile + bb, h] = (s0_ref[p * per_tile + bb, h] * eg[last:last + 1, :]
                                               + _dot(kd_b, upad))
        return carry

    lax.fori_loop(0, rows // SUBLANES, tile, 0)


def _delta_sample(proj, seq_len, layer, pe, dconv_w, alog_row, dtb_row, onorm, s0, s_stack):
    m = proj.shape[0]
    tb = TB_SAMPLE
    rows = tb * seq_len
    colspec = lambda off: pl.BlockSpec((rows, W_A), lambda i: (i, off // W_A))
    row = pl.BlockSpec((None, 1, LANES), lambda i: (layer, 0, 0))
    sspec = pl.BlockSpec((None, tb, H_A, DK, DV), lambda i: (layer, i, 0, 0, 0))
    in_specs = [colspec(COL_Q), colspec(COL_K), colspec(COL_V), colspec(COL_G),
                pl.BlockSpec((rows, LANES), lambda i: (i, COL_AB // LANES)),
                pl.BlockSpec((rows, N_DCONV), lambda i: (i, 0)),
                pl.BlockSpec((None, CONV_W, N_DCONV), lambda i: (layer, 0, 0)),
                row, row, row, sspec]
    args = [proj, proj, proj, proj, proj, pe, dconv_w, alog_row, dtb_row, onorm, s0]
    aliases = {}
    if s_stack is not None:
        in_specs.append(pl.BlockSpec(memory_space=pl.ANY))
        aliases = {len(args): 1}
        args.append(s_stack)
    return pl.pallas_call(
        functools.partial(_delta_sample_body, seq_len=seq_len),
        grid=(m // rows,),
        in_specs=in_specs,
        out_specs=[pl.BlockSpec((rows, W_A), lambda i: (i, 0)), sspec],
        out_shape=[jax.ShapeDtypeStruct((m, W_A), BF16), jax.ShapeDtypeStruct(s0.shape, F32)],
        input_output_aliases=aliases,
        scratch_shapes=[pltpu.VMEM((rows, W_A), F32)] * 3 + [pltpu.VMEM((rows, LANES), F32)] * 2,
        compiler_params=_cparams("parallel"),
        name="delta_sample",
    )(*args)


def _pad_rows(state, seq_len):
    b, r, c = state.shape
    return jnp.pad(state, ((0, 0), (0, seq_len - r), (0, 0))).reshape(b * seq_len, c)


def _trunk(x, mod, batch, seq_len, wts, states, ffn_cast):
    (w_in, w_out, norm_g3, ffn_f32, dconv_w, alog_row, dtb_row, onorm, lconv_w, lconv_b,
     wa, wx, ba, bx, lam, final_g) = wts
    new_d, new_dc, new_l, new_lc = [], [], [], []
    s_stack = None

    def ffn(x, layer, s, fg=None):
        if (layer, s) in ffn_cast:
            return _ffn(x, mod, norm_g3, *ffn_cast[(layer, s)], layer, s, fg)
        y, w13, w2 = _ffn(x, mod, norm_g3, ffn_f32[:2], ffn_f32[2], layer, s, fg, emit=True)
        ffn_cast[(layer, s)] = (w13, w2)
        return y

    for layer in range(DEPTH):
        x = ffn(x, layer, 0)
        proj = _inproj(x, mod, norm_g3, w_in, layer)
        if states is None:
            oa, sd = _delta_prompt(proj, batch, seq_len, layer, dconv_w, alog_row, dtb_row, onorm)
            new_d.append(sd)
            ob, hl = _lru(proj, seq_len, layer, lconv_w, lconv_b, wa, wx, ba, bx, lam)
            sl = hl.reshape(batch, SUBLANES, W_B)[:, SUBLANES - 1]
        else:
            s_delta, s_dconv, s_lru, s_lconv = states
            oa, s_stack = _delta_sample(proj, seq_len, layer, _pad_rows(s_dconv[layer], seq_len),
                                        dconv_w, alog_row, dtb_row, onorm, s_delta, s_stack)
            ob, hl = _lru(proj, seq_len, layer, lconv_w, lconv_b, wa, wx, ba, bx, lam,
                          pe=_pad_rows(s_lconv[layer], seq_len),
                          h0e=_pad_rows(s_lru[layer][:, None, :], seq_len))
            sl = hl.reshape(batch, seq_len, W_B)[:, seq_len - 1]
        x = _outproj(x, oa, ob, mod, w_out, layer)
        x = ffn(x, layer, 1, final_g if layer == DEPTH - 1 else None)
        first_row = seq_len - (CONV_W - 1)
        if seq_len % SUBLANES == 0:
            ends = proj.reshape(batch, seq_len, N_PROJ)[:, first_row:]
            last_rows = lambda lo, hi: ends[:, :, lo:hi]
        else:
            last_rows = lambda lo, hi: proj[:, lo:hi].reshape(batch, seq_len, hi - lo)[:, first_row:]
        new_dc.append(last_rows(COL_Q, COL_G))
        new_l.append(sl)
        new_lc.append(last_rows(COL_X, COL_Y))
    new_delta = jnp.stack(new_d) if states is None else s_stack
    return x, new_delta, jnp.stack(new_dc), jnp.stack(new_l), jnp.stack(new_lc)


def kernel(x_prompt, x_sample, c_prompt, c_sample, state_delta, state_delta_conv, state_lru, state_lru_conv, w_in, w_out, norm_g, w_ada, b_ada, ffn_w1, ffn_w3, ffn_w2, dconv_w, d_alog, d_dtbias, d_onorm, lconv_w, lconv_b, lru_wa, lru_ba, lru_wx, lru_bx, lru_lam, final_g):
    bp, lp, _ = x_prompt.shape
    bs, ls, _ = x_sample.shape
    assert lp % CHUNK == 0 and SUBLANES % ls == 0 and bp <= SUBLANES

    w_in_r = jnp.zeros((DEPTH, D_MODEL, N_PROJ), BF16)
    for dst, lo, hi in ((COL_Q, 0, OFF_A), (COL_X, OFF_X, N_IN), (COL_AB, OFF_A, OFF_X)):
        w_in_r = lax.dynamic_update_slice(w_in_r, w_in[..., lo:hi].astype(BF16), (0, 0, dst))
    w_out_b = w_out.astype(BF16)
    wa = lru_wa.astype(BF16)
    wx = lru_wx.astype(BF16)
    norm_g3 = norm_g.reshape(DEPTH * 3, 1, D_MODEL)
    lane_row = lambda v: jnp.pad(v, ((0, 0), (0, LANES - v.shape[1]))).reshape(DEPTH, 1, LANES)
    vec = lambda v: v.reshape(DEPTH, 1, W_B)
    wts = (w_in_r, w_out_b, norm_g3, (ffn_w1, ffn_w3, ffn_w2), dconv_w, lane_row(d_alog), lane_row(d_dtbias),
           d_onorm.reshape(DEPTH, 1, DV), lconv_w, vec(lconv_b), wa, wx, vec(lru_ba), vec(lru_bx),
           vec(lru_lam), final_g)

    ms = bs * ls
    c_all = jnp.concatenate([c_sample, c_prompt, jnp.zeros((SUBLANES - bp, D_MODEL), F32)], axis=0)
    mod_all = _ada(c_all, w_ada, b_ada, bs, ls)
    mod_p = _Mod(mod_all[:, ms:ms + bp].reshape(DEPTH, bp, 1, 9 * D_MODEL), False, lp)
    mod_s = _Mod(mod_all, True, ls)

    ffn_cast = {}
    ys, sd, sdc, sl_, slc = _trunk(x_sample.reshape(bs * ls, D_MODEL), mod_s, bs, ls, wts,
                                   (state_delta, state_delta_conv, state_lru, state_lru_conv), ffn_cast)
    yp, pd, pdc, pl_, plc = _trunk(x_prompt.reshape(bp * lp, D_MODEL), mod_p, bp, lp, wts, None, ffn_cast)
    return (yp.reshape(bp, lp, D_MODEL), ys.reshape(bs, ls, D_MODEL), pd, pdc, pl_, plc, sd, sdc, sl_, slc)
```

```python
import functools

import jax
import jax.numpy as jnp
from jax import lax
from jax.experimental import pallas as pl
from jax.experimental.pallas import tpu as pltpu

F32 = jnp.float32
BF16 = jnp.bfloat16

D_MODEL = 2048
DEPTH = 4
H_A = 8
DK = 128
DV = 128
W_QK = H_A * DK
W_A = H_A * DV
W_B = D_MODEL - W_A
LRU_BLOCKS = 8
LRU_BW = W_B // LRU_BLOCKS
LRU_C = 8.0
CONV_W = 4
D_FF = ((8 * D_MODEL // 3 + 127) // 128) * 128
HALF = 0.5
EPS = 1e-6
N_DCONV = 2 * W_QK + W_A
OFF_G = 2 * W_QK + W_A
OFF_A = OFF_G + W_A
OFF_X = OFF_A + 2 * H_A
N_IN = OFF_X + 2 * W_B

LANES = 128
SUBLANES = 8
VMEM_LIMIT = 56 * 1024 * 1024

COL_Q, COL_K, COL_V, COL_G = 0, W_QK, 2 * W_QK, 2 * W_QK + W_A
COL_X = COL_G + W_A
COL_Y = COL_X + W_B
COL_AB = COL_Y + W_B
N_PROJ = COL_AB + 2 * LANES

TM_ROWS = 512
TM_PROJ = 1024
TF_FFN = 1024
TF_EMIT = 256
TN_PROJ = 1280
TN_ADA = 1024
CHUNK = 128
HG_PROMPT = 2
FACTOR_CHUNKS = 16
TB_SAMPLE = 16

assert N_PROJ % TN_PROJ == 0 and (9 * D_MODEL) % TN_ADA == 0 and D_FF % LANES == 0


def _cparams(*sem):
    return pltpu.CompilerParams(dimension_semantics=sem, vmem_limit_bytes=VMEM_LIMIT)


def _dot(a, b):
    return jnp.dot(a, b, preferred_element_type=F32)


def _dot_nt(a, b):
    return lax.dot_general(a, b, (((1,), (1,)), ((), ())), preferred_element_type=F32)


def _silu(x):
    return x * jax.nn.sigmoid(x)


def _expm1(x):
    u = jnp.exp(x)
    um1 = u - 1.0
    safe = um1 * x / jnp.log(jnp.where(um1 == 0.0, 2.0, u))
    return jnp.where(um1 == 0.0, x, jnp.where(x < -30.0, um1, safe))


def _norm_mod(x, g, shift, scale):
    var = jnp.mean(x * x, axis=-1, keepdims=True)
    return x * lax.rsqrt(var + EPS) * (g * (1.0 + scale)) + shift


def _row_iota(shape, mod):
    return jnp.bitwise_and(lax.broadcasted_iota(jnp.int32, shape, 0), mod - 1)


def _conv_zero_past(x_ref, w):
    rows = x_ref.shape[0]
    head = x_ref[:SUBLANES, :]
    head_y = _conv_rows(head, None, w, lax.broadcasted_iota(jnp.int32, head.shape, 0))
    y = x_ref[SUBLANES:, :] * w[CONV_W - 1:CONV_W, :]
    for j in range(1, CONV_W):
        y = y + x_ref[SUBLANES - j:rows - j, :] * w[CONV_W - 1 - j:CONV_W - j, :]
    return jnp.concatenate([head_y, y], axis=0)


def _conv_rows(x, pe, w, t):
    rows = x.shape[0]
    y = x * w[CONV_W - 1:CONV_W, :]
    for j in range(1, CONV_W):
        term = pltpu.roll(x, j, axis=0)
        if pe is None:
            term = jnp.where(t >= j, term, 0.0)
        else:
            back = CONV_W - 1 - j
            prev = pe if back == 0 else pltpu.roll(pe, rows - back, axis=0)
            term = jnp.where(t >= j, term, prev)
        y = y + term * w[CONV_W - 1 - j:CONV_W - j, :]
    return y


def _ada_body(c_ref, w_ref, b_ref, o_ref, *, nseq, reps):
    cs = _silu(c_ref[...]).astype(BF16)
    m = _dot(cs, w_ref[...].astype(BF16)) + b_ref[...]
    hi = m[:nseq].astype(BF16)
    r1 = m[:nseq] - hi.astype(F32)
    mid = r1.astype(BF16)
    lo = (r1 - mid.astype(F32)).astype(BF16)
    rows = nseq * reps
    src = lax.broadcasted_iota(jnp.int32, (rows, 3 * nseq), 1)
    dst = lax.div(lax.broadcasted_iota(jnp.int32, (rows, 3 * nseq), 0), reps)
    hit = (src == dst) | (src == dst + nseq) | (src == dst + 2 * nseq)
    pick = jnp.where(hit, 1.0, 0.0).astype(BF16)
    o_ref[:rows, :] = _dot(pick, jnp.concatenate([hi, mid, lo], axis=0))
    o_ref[rows:, :] = m[nseq:]


def _ada(c_all, w_ada, b_ada, nseq, reps):
    rows = c_all.shape[0]
    out_rows = nseq * reps + rows - nseq
    n9 = 9 * D_MODEL
    return pl.pallas_call(
        functools.partial(_ada_body, nseq=nseq, reps=reps),
        grid=(DEPTH, n9 // TN_ADA),
        in_specs=[
            pl.BlockSpec((rows, D_MODEL), lambda l, n: (0, 0)),
            pl.BlockSpec((None, D_MODEL, TN_ADA), lambda l, n: (l, 0, n)),
            pl.BlockSpec((None, 1, TN_ADA), lambda l, n: (l, 0, n)),
        ],
        out_specs=pl.BlockSpec((None, out_rows, TN_ADA), lambda l, n: (l, 0, n)),
        out_shape=jax.ShapeDtypeStruct((DEPTH, out_rows, n9), F32),
        compiler_params=_cparams("parallel", "parallel"),
        name="ada",
    )(c_all, w_ada, b_ada.reshape(DEPTH, 1, n9))


class _Mod:
    def __init__(self, arr, per_token, seq_len):
        self.arr = arr
        self.per_token = per_token
        self.seq_len = seq_len

    def spec(self, layer, tm, k, **kw):
        if self.per_token:
            return pl.BlockSpec((None, tm, D_MODEL), lambda i, *_: (layer, i, k), **kw)
        seq_len = self.seq_len
        return pl.BlockSpec((None, None, 1, D_MODEL), lambda i, *_: (layer, (i * tm) // seq_len, 0, k), **kw)


def _ffn_body(x_ref, g_ref, sh_ref, sc_ref, gt_ref, w1_ref, w3_ref, w2_ref, *rest, final, emit, tf):
    rest = list(rest)
    fg_ref = rest.pop(0) if final else None
    o_ref = rest.pop(0)
    if emit:
        w1b_ref, w3b_ref, w2b_ref = rest[:3]
        rest = rest[3:]
        w1b_ref[...] = w1_ref[...].astype(BF16)
        w3b_ref[...] = w3_ref[...].astype(BF16)
        w2b_ref[...] = w2_ref[...].astype(BF16)
        w1_ref, w3_ref, w2_ref = w1b_ref, w3b_ref, w2b_ref
    f = pl.program_id(1)
    last = pl.num_programs(1) - 1
    steps = -(-D_FF // tf)
    tail = D_FF - (steps - 1) * tf

    def epilogue(acc):
        y = x_ref[...] + HALF * gt_ref[...] * acc
        if final:
            var = jnp.mean(y * y, axis=-1, keepdims=True)
            y = y * lax.rsqrt(var + EPS) * fg_ref[...]
        o_ref[...] = y

    h_scr, acc_scr = rest

    def swiglu(width):
        h = h_scr[...]
        a = _dot(h, w1_ref[:, :width])
        b = _dot(h, w3_ref[:, :width])
        return _dot((_silu(a) * b).astype(BF16), w2_ref[:width, :])

    @pl.when(f == 0)
    def _():
        h_scr[...] = _norm_mod(x_ref[...], g_ref[...], sh_ref[...], sc_ref[...]).astype(BF16)
        acc_scr[...] = swiglu(tf)

    @pl.when((f > 0) & (f < last))
    def _():
        acc_scr[...] += swiglu(tf)

    @pl.when(f == last)
    def _():
        epilogue(acc_scr[...] + swiglu(tail))


def _ffn(x, mod, norm_g3, w13, w2, layer, s, final_g=None, emit=False):
    m = x.shape[0]
    tm = min(TM_ROWS, m)
    tf = TF_EMIT if emit else TF_FFN
    steps = -(-D_FF // tf)
    assert steps >= 2
    sub = 2 * s
    final = final_g is not None
    w1, w3 = w13
    if emit:
        assert m == tm
        once = dict(pipeline_mode=pl.Buffered(1))
        up = pl.BlockSpec((None, None, D_MODEL, tf), lambda i, f: (layer, s, 0, f))
        down = pl.BlockSpec((None, None, tf, D_MODEL), lambda i, f: (layer, s, f, 0))
    else:
        once = {}
        up = pl.BlockSpec((D_MODEL, tf), lambda i, f: (0, f))
        down = pl.BlockSpec((tf, D_MODEL), lambda i, f: (f, 0))
    in_specs = [
        pl.BlockSpec((tm, D_MODEL), lambda i, f: (i, 0), **once),
        pl.BlockSpec((None, 1, D_MODEL), lambda i, f: (layer * 3 + sub, 0, 0)),
        mod.spec(layer, tm, 3 * sub, **once), mod.spec(layer, tm, 3 * sub + 1, **once),
        mod.spec(layer, tm, 3 * sub + 2, **once),
        up, up, down,
    ]
    args = [x, norm_g3, mod.arr, mod.arr, mod.arr, w1, w3, w2]
    if final:
        in_specs.append(pl.BlockSpec((1, D_MODEL), lambda i, f: (0, 0)))
        args.append(final_g.reshape(1, D_MODEL))
    out_specs = [pl.BlockSpec((tm, D_MODEL), lambda i, f: (i, 0))]
    out_shape = [jax.ShapeDtypeStruct((m, D_MODEL), F32)]
    if emit:
        out_specs += [pl.BlockSpec((D_MODEL, tf), lambda i, f: (0, f))] * 2 + [pl.BlockSpec((tf, D_MODEL), lambda i, f: (f, 0))]
        out_shape += [jax.ShapeDtypeStruct((D_MODEL, D_FF), BF16)] * 2 + [jax.ShapeDtypeStruct((D_FF, D_MODEL), BF16)]
    outs = pl.pallas_call(
        functools.partial(_ffn_body, final=final, emit=emit, tf=tf),
        grid=(m // tm, steps),
        in_specs=in_specs,
        out_specs=out_specs,
        out_shape=out_shape,
        scratch_shapes=[pltpu.VMEM((tm, D_MODEL), BF16), pltpu.VMEM((tm, D_MODEL), F32)],
        compiler_params=_cparams("parallel", "arbitrary"),
        name="ffn_cast" if emit else "ffn",
    )(*args)
    return (outs[0], (outs[1], outs[2]), outs[3]) if emit else outs[0]


def _inproj_body(x_ref, g_ref, sh_ref, sc_ref, w_ref, o_ref, h_scr):
    n = pl.program_id(1)

    @pl.when(n == 0)
    def _():
        h_scr[...] = _norm_mod(x_ref[...], g_ref[...], sh_ref[...], sc_ref[...]).astype(BF16)
        o_ref[...] = _dot(h_scr[...], w_ref[...])

    @pl.when(n > 0)
    def _():
        o_ref[...] = _dot(h_scr[...], w_ref[...])


def _inproj(x, mod, norm_g3, w_in, layer):
    m = x.shape[0]
    tm = min(TM_PROJ, m)
    return pl.pallas_call(
        _inproj_body,
        grid=(m // tm, N_PROJ // TN_PROJ),
        in_specs=[
            pl.BlockSpec((tm, D_MODEL), lambda i, n: (i, 0)),
            pl.BlockSpec((None, 1, D_MODEL), lambda i, n: (layer * 3 + 1, 0, 0)),
            mod.spec(layer, tm, 3), mod.spec(layer, tm, 4),
            pl.BlockSpec((None, D_MODEL, TN_PROJ), lambda i, n: (layer, 0, n)),
        ],
        out_specs=pl.BlockSpec((tm, TN_PROJ), lambda i, n: (i, n)),
        out_shape=jax.ShapeDtypeStruct((m, N_PROJ), F32),
        scratch_shapes=[pltpu.VMEM((tm, D_MODEL), BF16)],
        compiler_params=_cparams("parallel", "arbitrary"),
        name="inproj",
    )(x, norm_g3, mod.arr, mod.arr, w_in)


def _outproj_body(x_ref, oa_ref, ob_ref, gt_ref, wa_ref, wb_ref, o_ref):
    acc = _dot(oa_ref[...], wa_ref[...]) + _dot(ob_ref[...], wb_ref[...])
    o_ref[...] = x_ref[...] + gt_ref[...] * acc


def _outproj(x, oa, ob, mod, w_out, layer):
    m = x.shape[0]
    tm = min(TM_ROWS, m)
    return pl.pallas_call(
        _outproj_body,
        grid=(m // tm,),
        in_specs=[
            pl.BlockSpec((tm, D_MODEL), lambda i: (i, 0)),
            pl.BlockSpec((tm, W_A), lambda i: (i, 0)),
            pl.BlockSpec((tm, W_B), lambda i: (i, 0)),
            mod.spec(layer, tm, 5),
            pl.BlockSpec((None, W_A, D_MODEL), lambda i: (layer, 0, 0)),
            pl.BlockSpec((None, W_B, D_MODEL), lambda i: (layer, 1, 0)),
        ],
        out_specs=pl.BlockSpec((tm, D_MODEL), lambda i: (i, 0)),
        out_shape=jax.ShapeDtypeStruct((m, D_MODEL), F32),
        compiler_params=_cparams("parallel"),
        name="outproj",
    )(x, oa, ob, mod.arr, w_out, w_out)


def _lru_body(*refs, seq_len, has_state):
    if has_state:
        (x_ref, y_ref, pe_ref, h0_ref, cw_ref, cb_ref, wa_ref, wx_ref, ba_ref, bx_ref, lam_ref,
         ob_ref, hl_ref, a_scr, b_scr) = refs
    else:
        (x_ref, y_ref, cw_ref, cb_ref, wa_ref, wx_ref, ba_ref, bx_ref, lam_ref,
         ob_ref, hl_ref, a_scr, b_scr) = refs
    rows = x_ref.shape[0]
    if has_state:
        xc = _conv_rows(x_ref[...], pe_ref[...], cw_ref[...], _row_iota(x_ref.shape, seq_len))
    else:
        xc = _conv_zero_past(x_ref, cw_ref[...])
    xc = xc + cb_ref[...]
    xcb = xc.astype(BF16)
    r = jax.nn.sigmoid(_dot(xcb, wa_ref[...]) + ba_ref[...])
    i = jax.nn.sigmoid(_dot(xcb, wx_ref[...]) + bx_ref[...])
    log_a = -LRU_C * r * jax.nn.softplus(-lam_ref[...])
    a = jnp.exp(log_a)
    b = jnp.sqrt(-_expm1(2.0 * log_a)) * (i * xc)
    if has_state:
        b = b + a * h0_ref[...]
    group = min(seq_len, SUBLANES)
    tg = _row_iota(x_ref.shape, group)
    d = 1
    while d < group:
        keep = tg >= d
        b = jnp.where(keep, a * pltpu.roll(b, d, axis=0) + b, b)
        a = jnp.where(keep, a * pltpu.roll(a, d, axis=0), a)
        d *= 2
    if seq_len > SUBLANES:
        a_scr[...] = a
        b_scr[...] = b

        def step(k, carry):
            r0 = pl.multiple_of(k * SUBLANES, SUBLANES)
            hk = b_scr[pl.ds(r0, SUBLANES), :] + a_scr[pl.ds(r0, SUBLANES), :] * carry
            b_scr[pl.ds(r0, SUBLANES), :] = hk
            return hk[SUBLANES - 1:SUBLANES, :]

        lax.fori_loop(0, rows // SUBLANES, step, jnp.zeros((1, LRU_BW), F32), unroll=8)
        h = b_scr[...]
    else:
        h = b
    ob_ref[...] = (h * jax.nn.gelu(y_ref[...])).astype(BF16)
    hl_ref[...] = h[rows - hl_ref.shape[0]:, :]


def _lru(proj, seq_len, layer, lconv_w, lconv_b, wa, wx, ba, bx, lam, pe=None, h0e=None):
    m = proj.shape[0]
    has_state = pe is not None
    rows = seq_len if not has_state else m
    nseq = m // rows
    hl_rows = rows if has_state else SUBLANES
    bx0, by0 = COL_X // LRU_BW, COL_Y // LRU_BW
    col = lambda b, n: (b, n)
    vec = pl.BlockSpec((None, 1, LRU_BW), lambda b, n: (layer, 0, n))
    blk = pl.BlockSpec((None, None, LRU_BW, LRU_BW), lambda b, n: (layer, n, 0, 0))
    in_specs = [pl.BlockSpec((rows, LRU_BW), lambda b, n: (b, bx0 + n)),
                pl.BlockSpec((rows, LRU_BW), lambda b, n: (b, by0 + n))]
    args = [proj, proj]
    if has_state:
        in_specs += [pl.BlockSpec((rows, LRU_BW), col), pl.BlockSpec((rows, LRU_BW), col)]
        args += [pe, h0e]
    in_specs += [pl.BlockSpec((None, CONV_W, LRU_BW), lambda b, n: (layer, 0, n)), vec, blk, blk, vec, vec, vec]
    args += [lconv_w, lconv_b, wa, wx, ba, bx, lam]
    return pl.pallas_call(
        functools.partial(_lru_body, seq_len=seq_len, has_state=has_state),
        grid=(nseq, LRU_BLOCKS),
        in_specs=in_specs,
        out_specs=[pl.BlockSpec((rows, LRU_BW), col), pl.BlockSpec((hl_rows, LRU_BW), col)],
        out_shape=[jax.ShapeDtypeStruct((m, W_B), BF16), jax.ShapeDtypeStruct((nseq * hl_rows, W_B), F32)],
        scratch_shapes=[pltpu.VMEM((rows, LRU_BW), F32), pltpu.VMEM((rows, LRU_BW), F32)],
        compiler_params=_cparams("parallel", "parallel"),
        name="lru",
    )(*args)


def _qkv_prep(q_conv, k_conv, v_conv, heads):
    q, k, v = _silu(q_conv), _silu(k_conv), _silu(v_conv)
    qs, ks = [], []
    for h in range(heads):
        sl = slice(h * DK, (h + 1) * DK)
        qh, kh = q[:, sl], k[:, sl]
        qs.append(qh * (lax.rsqrt(jnp.sum(qh * qh, axis=-1, keepdims=True) + EPS) * DK ** -0.5))
        ks.append(kh * lax.rsqrt(jnp.sum(kh * kh, axis=-1, keepdims=True) + EPS))
    return jnp.concatenate(qs, axis=1), jnp.concatenate(ks, axis=1), v


def _gate_norm(o, onorm, gate_pre):
    var = jnp.mean(o * o, axis=-1, keepdims=True)
    return o * lax.rsqrt(var + EPS) * onorm * _silu(gate_pre)


def _seg_cumsum(g, tg, group):
    d = 1
    while d < group:
        g = jnp.where(tg >= d, g + pltpu.roll(g, d, axis=0), g)
        d *= 2
    return g


def _delta_prompt_body(q_ref, k_ref, v_ref, gt_ref, ab_ref, cwq_ref, cwk_ref, cwv_ref, alog_ref, dtb_ref,
                       on_ref, oa_ref, s_ref, qn_scr, kn_scr, vv_scr, gc_scr, bt_scr, s_scr,
                       wq_scr, uc_scr, qk_scr, kdt_scr, sd_scr):
    rows = q_ref.shape[0]
    hg = q_ref.shape[1] // DK
    qn, kn, vv = _qkv_prep(_conv_zero_past(q_ref, cwq_ref[...]), _conv_zero_past(k_ref, cwk_ref[...]),
                           _conv_zero_past(v_ref, cwv_ref[...]), hg)
    qn_scr[...] = qn
    kn_scr[...] = kn
    vv_scr[...] = vv
    ab = ab_ref[...]
    g = -jnp.exp(alog_ref[...]) * jax.nn.softplus(ab + dtb_ref[...])
    gc_scr[...] = _seg_cumsum(g, _row_iota(ab.shape, CHUNK), CHUNK)
    bt_scr[...] = jax.nn.sigmoid(ab)
    s_scr[...] = jnp.zeros_like(s_scr)
    head0 = pl.program_id(1) * hg
    onorm = on_ref[...]

    def factor(step, carry):
        lane = lax.broadcasted_iota(jnp.int32, (CHUNK, LANES), 1)
        ri = lax.broadcasted_iota(jnp.int32, (CHUNK, CHUNK), 0)
        ci = lax.broadcasted_iota(jnp.int32, (CHUNK, CHUNK), 1)
        causal = ri >= ci
        strict = ri > ci
        blk = jnp.bitwise_xor(ri, ci)
        chains = []
        for cc in range(FACTOR_CHUNKS):
            c = step * FACTOR_CHUNKS + cc
            rs = pl.ds(pl.multiple_of(c * CHUNK, CHUNK), CHUNK)
            gcc = gc_scr[rs, :]
            btc = bt_scr[rs, :]
            for hh in range(hg):
                sl = slice(hh * DK, (hh + 1) * DK)
                gcol = jnp.sum(jnp.where(lane == head0 + hh, gcc, 0.0), axis=1, keepdims=True)
                bcol = jnp.sum(jnp.where(lane == head0 + hh + H_A, btc, 0.0), axis=1, keepdims=True)
                gmat = jnp.broadcast_to(gcol, (CHUNK, CHUNK))
                decay = jnp.where(causal, jnp.exp(jnp.where(causal, gmat - gmat.T, 0.0)), 0.0)
                q = qn_scr[rs, sl]
                k = kn_scr[rs, sl]
                kq = _dot_nt(jnp.concatenate([k, q], axis=0).astype(BF16), k.astype(BF16))
                m = jnp.where(strict, bcol * kq[:CHUNK] * decay, 0.0)
                eg = jnp.exp(gcol)
                glast = jnp.broadcast_to(gcol[CHUNK - 1:CHUNK, :], (CHUNK, 1))
                wq_scr[hh, c, CHUNK:, :] = (q * eg).astype(BF16)
                qk_scr[hh, rs, :] = (kq[CHUNK:] * decay).astype(BF16)
                kdt_scr[hh, rs, :] = (k * jnp.exp(glast - gcol)).T.astype(BF16)
                sd_scr[hh, pl.ds(pl.multiple_of(c * SUBLANES, SUBLANES), SUBLANES), :] = jnp.broadcast_to(
                    jnp.broadcast_to(eg, (CHUNK, DV))[CHUNK - 1:CHUNK, :], (SUBLANES, DV))
                chains.append((hh, c, rs, sl, m, bcol, eg))
        es = [-jnp.where(lax.shift_right_logical(blk, 1) == 0, ch[4], 0.0) for ch in chains]
        lg = 1
        while (1 << lg) < CHUNK:
            cross = lax.shift_right_logical(blk, lg) == 1
            offs = [jnp.where(cross, ch[4], 0.0) for ch in chains]
            ebs = [e.astype(BF16) for e in es]
            xs = [off + _dot(eb, off.astype(BF16)) for off, eb in zip(offs, ebs)]
            es = [e - (x + _dot(x.astype(BF16), eb)) for e, x, eb in zip(es, xs, ebs)]
            lg += 1
        for (hh, c, rs, sl, m, bcol, eg), e in zip(chains, es):
            rhs = jnp.concatenate([kn_scr[rs, sl] * (bcol * eg), vv_scr[rs, sl] * bcol], axis=1)
            sol = rhs + _dot(e.astype(BF16), rhs.astype(BF16))
            wq_scr[hh, c, :CHUNK, :] = sol[:, :DK].astype(BF16)
            uc_scr[hh, rs, :] = sol[:, DK:]
        return carry

    lax.fori_loop(0, rows // (CHUNK * FACTOR_CHUNKS), factor, 0)

    def recur(c, carry):
        rs = pl.ds(pl.multiple_of(c * CHUNK, CHUNK), CHUNK)
        s_old = [s_scr[hh] for hh in range(hg)]
        ps = [_dot(wq_scr[hh, c], s_old[hh].astype(BF16)) for hh in range(hg)]
        us = [(uc_scr[hh, rs, :] - ps[hh][:CHUNK]).astype(BF16) for hh in range(hg)]
        os_ = [ps[hh][CHUNK:] + _dot(qk_scr[hh, rs, :], us[hh]) for hh in range(hg)]
        for hh in range(hg):
            sdec = sd_scr[hh, pl.ds(pl.multiple_of(c * SUBLANES, SUBLANES), SUBLANES), :]
            s_scr[hh] = s_old[hh] * sdec[:1, :] + _dot(kdt_scr[hh, rs, :], us[hh])
        for hh in range(hg):
            sl = slice(hh * DK, (hh + 1) * DK)
            oa_ref[rs, sl] = _gate_norm(os_[hh], onorm, gt_ref[rs, sl]).astype(BF16)
        return carry

    lax.fori_loop(0, rows // CHUNK, recur, 0)
    s_ref[...] = s_scr[...]


def _delta_prompt(proj, batch, seq_len, layer, dconv_w, alog_row, dtb_row, onorm):
    m = proj.shape[0]
    hg = HG_PROMPT
    wcols = hg * DK
    nchunks = seq_len // CHUNK
    assert nchunks % FACTOR_CHUNKS == 0
    nq, nk, nv, ng = (c // wcols for c in (COL_Q, COL_K, COL_V, COL_G))
    colspec = lambda off: pl.BlockSpec((seq_len, wcols), lambda b, j: (b, off + j))
    cwspec = lambda off: pl.BlockSpec((None, CONV_W, wcols), lambda b, j: (layer, 0, off + j))
    row = pl.BlockSpec((None, 1, LANES), lambda b, j: (layer, 0, 0))
    return pl.pallas_call(
        _delta_prompt_body,
        grid=(batch, H_A // hg),
        in_specs=[colspec(nq), colspec(nk), colspec(nv), colspec(ng),
                  pl.BlockSpec((seq_len, LANES), lambda b, j: (b, COL_AB // LANES)),
                  cwspec(nq), cwspec(nk), cwspec(nv), row, row, row],
        out_specs=[pl.BlockSpec((seq_len, wcols), lambda b, j: (b, j)),
                   pl.BlockSpec((None, hg, DK, DV), lambda b, j: (b, j, 0, 0))],
        out_shape=[jax.ShapeDtypeStruct((m, W_A), BF16), jax.ShapeDtypeStruct((batch, H_A, DK, DV), F32)],
        scratch_shapes=[pltpu.VMEM((seq_len, wcols), F32)] * 3
        + [pltpu.VMEM((seq_len, LANES), F32)] * 2 + [pltpu.VMEM((hg, DK, DV), F32)]
        + [pltpu.VMEM((hg, nchunks, 2 * CHUNK, DK), BF16), pltpu.VMEM((hg, seq_len, DV), F32),
           pltpu.VMEM((hg, seq_len, CHUNK), BF16), pltpu.VMEM((hg, seq_len, CHUNK), BF16),
           pltpu.VMEM((hg, nchunks * SUBLANES, DV), F32)],
        compiler_params=_cparams("parallel", "parallel"),
        name="delta_prompt",
    )(proj, proj, proj, proj, proj, dconv_w, dconv_w, dconv_w, alog_row, dtb_row, onorm)


def _delta_sample_body(q_ref, k_ref, v_ref, gt_ref, ab_ref, pe_ref, cw_ref, alog_ref, dtb_ref, on_ref, s0_ref,
                       *rest, seq_len):
    oa_ref, s_ref, qn_scr, kn_scr, vv_scr, gc_scr, bt_scr = rest[-7:]
    rows = q_ref.shape[0]
    t = _row_iota(q_ref.shape, seq_len)
    pe = pe_ref[...]
    cw = cw_ref[...]
    qn, kn, vv = _qkv_prep(_conv_rows(q_ref[...], pe[:, COL_Q:COL_K], cw[:, COL_Q:COL_K], t),
                           _conv_rows(k_ref[...], pe[:, COL_K:COL_V], cw[:, COL_K:COL_V], t),
                           _conv_rows(v_ref[...], pe[:, COL_V:COL_G], cw[:, COL_V:COL_G], t), H_A)
    qn_scr[...] = qn
    kn_scr[...] = kn
    vv_scr[...] = vv
    ab = ab_ref[...]
    g = -jnp.exp(alog_ref[...]) * jax.nn.softplus(ab + dtb_ref[...])
    gc_scr[...] = _seg_cumsum(g, _row_iota(ab.shape, seq_len), seq_len)
    bt_scr[...] = jax.nn.sigmoid(ab)
    onorm = on_ref[...]
    per_tile = SUBLANES // seq_len

    def tile(p, carry):
        tt = _row_iota((SUBLANES, LANES), seq_len)
        tt2 = _row_iota((SUBLANES, DK + DV), seq_len)
        ri = lax.broadcasted_iota(jnp.int32, (SUBLANES, LANES), 0)
        ri2 = _row_iota((2 * SUBLANES, LANES), SUBLANES)
        lane = lax.broadcasted_iota(jnp.int32, (LANES, LANES), 1)
        zpad = jnp.zeros((LANES - SUBLANES, DK), F32)
        rs = pl.ds(pl.multiple_of(p * SUBLANES, SUBLANES), SUBLANES)
        gcc, btc = gc_scr[rs, :], bt_scr[rs, :]
        work = [(p, rs, h, gcc, btc) for h in range(H_A)]
        staged = []
        for p, rs, h, gcc, btc in work:
            sl = slice(h * DK, (h + 1) * DK)
            gc = jnp.broadcast_to(gcc[:, h:h + 1], (SUBLANES, LANES))
            beta = jnp.broadcast_to(btc[:, H_A + h:H_A + h + 1], (SUBLANES, LANES))
            q = qn_scr[rs, sl]
            k = kn_scr[rs, sl]
            v = vv_scr[rs, sl]
            eg = jnp.exp(gc)
            decs, ms = [None], [None]
            for d in range(1, seq_len):
                ok = tt >= d
                dec = jnp.where(ok, jnp.exp(jnp.where(ok, gc - pltpu.roll(gc, d, axis=0), 0.0)), 0.0)
                kk = jnp.sum(k * pltpu.roll(k, d, axis=0), axis=-1, keepdims=True)
                decs.append(dec)
                md = beta * kk * dec
                ms.append(jnp.concatenate([md, md], axis=1))
            rhs = jnp.concatenate([k * (beta * eg), v * beta], axis=1)
            sol = rhs
            for step in range(1, seq_len):
                acc = ms[1] * pltpu.roll(sol, 1, axis=0)
                for d in range(2, seq_len):
                    acc = acc + ms[d] * pltpu.roll(sol, d, axis=0)
                sol = jnp.where(tt2 == step, rhs - acc, sol)
            lhs = jnp.concatenate([sol[:, :DK], q * eg], axis=0).astype(BF16)
            staged.append((sl, gc, q, k, eg, decs, sol[:, DK:], lhs))
        pps = []
        for (p, rs, h, _, _), st in zip(work, staged):
            lhs = st[-1]
            pp = _dot(lhs, s0_ref[p * per_tile, h].astype(BF16))
            for bb in range(1, per_tile):
                pp = jnp.where(ri2 >= bb * seq_len, _dot(lhs, s0_ref[p * per_tile + bb, h].astype(BF16)), pp)
            pps.append(pp)
        updates = []
        for (p, rs, h, _, _), st, pp in zip(work, staged, pps):
            sl, gc, q, k, eg, decs, uc, _ = st
            u = uc - pp[:SUBLANES]
            o = pp[SUBLANES:] + jnp.sum(q * k, axis=-1, keepdims=True) * u
            for d in range(1, seq_len):
                qk = jnp.sum(q * pltpu.roll(k, d, axis=0), axis=-1, keepdims=True)
                o = o + (qk * decs[d]) * pltpu.roll(u, d, axis=0)
            oa_ref[rs, sl] = _gate_norm(o, onorm, gt_ref[rs, sl]).astype(BF16)
            glast = jnp.broadcast_to(gc[seq_len - 1:seq_len, :], (SUBLANES, LANES))
            for bb in range(1, per_tile):
                last = bb * seq_len + seq_len - 1
                glast = jnp.where(ri >= bb * seq_len, jnp.broadcast_to(gc[last:last + 1, :], (SUBLANES, LANES)), glast)
            kd = k * jnp.exp(glast - gc)
            kdt = jnp.concatenate([kd, zpad], axis=0).T
            updates.append((kdt, jnp.concatenate([u, zpad], axis=0).astype(BF16)))
        for (p, rs, h, _, _), st, (kdt, upad) in zip(work, staged, updates):
            eg = st[4]
            for bb in range(per_tile):
                last = bb * seq_len + seq_len - 1
                cols = (lane >= bb * seq_len) & (lane < (bb + 1) * seq_len)
                kd_b = jnp.where(cols, kdt, 0.0).astype(BF16)
                s_ref[p * per_tile + bb, h] = (s0_ref[p * per_tile + bb, h] * eg[last:last + 1, :]
                                               + _dot(kd_b, upad))
        return carry

    lax.fori_loop(0, rows // SUBLANES, tile, 0)


def _delta_sample(proj, seq_len, layer, pe, dconv_w, alog_row, dtb_row, onorm, s0, s_stack):
    m = proj.shape[0]
    tb = TB_SAMPLE
    rows = tb * seq_len
    colspec = lambda off: pl.BlockSpec((rows, W_A), lambda i: (i, off // W_A))
    row = pl.BlockSpec((None, 1, LANES), lambda i: (layer, 0, 0))
    sspec = pl.BlockSpec((None, tb, H_A, DK, DV), lambda i: (layer, i, 0, 0, 0))
    in_specs = [colspec(COL_Q), colspec(COL_K), colspec(COL_V), colspec(COL_G),
                pl.BlockSpec((rows, LANES), lambda i: (i, COL_AB // LANES)),
                pl.BlockSpec((rows, N_DCONV), lambda i: (i, 0)),
                pl.BlockSpec((None, CONV_W, N_DCONV), lambda i: (layer, 0, 0)),
                row, row, row, sspec]
    args = [proj, proj, proj, proj, proj, pe, dconv_w, alog_row, dtb_row, onorm, s0]
    aliases = {}
    if s_stack is not None:
        in_specs.append(pl.BlockSpec(memory_space=pl.ANY))
        aliases = {len(args): 1}
        args.append(s_stack)
    return pl.pallas_call(
        functools.partial(_delta_sample_body, seq_len=seq_len),
        grid=(m // rows,),
        in_specs=in_specs,
        out_specs=[pl.BlockSpec((rows, W_A), lambda i: (i, 0)), sspec],
        out_shape=[jax.ShapeDtypeStruct((m, W_A), BF16), jax.ShapeDtypeStruct(s0.shape, F32)],
        input_output_aliases=aliases,
        scratch_shapes=[pltpu.VMEM((rows, W_A), F32)] * 3 + [pltpu.VMEM((rows, LANES), F32)] * 2,
        compiler_params=_cparams("parallel"),
        name="delta_sample",
    )(*args)


def _pad_rows(state, seq_len):
    b, r, c = state.shape
    return jnp.pad(state, ((0, 0), (0, seq_len - r), (0, 0))).reshape(b * seq_len, c)


def _trunk(x, mod, batch, seq_len, wts, states, ffn_cast):
    (w_in, w_out, norm_g3, ffn_f32, dconv_w, alog_row, dtb_row, onorm, lconv_w, lconv_b,
     wa, wx, ba, bx, lam, final_g) = wts
    new_d, new_dc, new_l, new_lc = [], [], [], []
    s_stack = None

    def ffn(x, layer, s, fg=None):
        if (layer, s) in ffn_cast:
            return _ffn(x, mod, norm_g3, *ffn_cast[(layer, s)], layer, s, fg)
        y, w13, w2 = _ffn(x, mod, norm_g3, ffn_f32[:2], ffn_f32[2], layer, s, fg, emit=True)
        ffn_cast[(layer, s)] = (w13, w2)
        return y

    for layer in range(DEPTH):
        x = ffn(x, layer, 0)
        proj = _inproj(x, mod, norm_g3, w_in, layer)
        if states is None:
            oa, sd = _delta_prompt(proj, batch, seq_len, layer, dconv_w, alog_row, dtb_row, onorm)
            new_d.append(sd)
            ob, hl = _lru(proj, seq_len, layer, lconv_w, lconv_b, wa, wx, ba, bx, lam)
            sl = hl.reshape(batch, SUBLANES, W_B)[:, SUBLANES - 1]
        else:
            s_delta, s_dconv, s_lru, s_lconv = states
            oa, s_stack = _delta_sample(proj, seq_len, layer, _pad_rows(s_dconv[layer], seq_len),
                                        dconv_w, alog_row, dtb_row, onorm, s_delta, s_stack)
            ob, hl = _lru(proj, seq_len, layer, lconv_w, lconv_b, wa, wx, ba, bx, lam,
                          pe=_pad_rows(s_lconv[layer], seq_len),
                          h0e=_pad_rows(s_lru[layer][:, None, :], seq_len))
            sl = hl.reshape(batch, seq_len, W_B)[:, seq_len - 1]
        x = _outproj(x, oa, ob, mod, w_out, layer)
        x = ffn(x, layer, 1, final_g if layer == DEPTH - 1 else None)
        first_row = seq_len - (CONV_W - 1)
        if seq_len % SUBLANES == 0:
            ends = proj.reshape(batch, seq_len, N_PROJ)[:, first_row:]
            last_rows = lambda lo, hi: ends[:, :, lo:hi]
        else:
            last_rows = lambda lo, hi: proj[:, lo:hi].reshape(batch, seq_len, hi - lo)[:, first_row:]
        new_dc.append(last_rows(COL_Q, COL_G))
        new_l.append(sl)
        new_lc.append(last_rows(COL_X, COL_Y))
    new_delta = jnp.stack(new_d) if states is None else s_stack
    return x, new_delta, jnp.stack(new_dc), jnp.stack(new_l), jnp.stack(new_lc)


def kernel(x_prompt, x_sample, c_prompt, c_sample, state_delta, state_delta_conv, state_lru, state_lru_conv, w_in, w_out, norm_g, w_ada, b_ada, ffn_w1, ffn_w3, ffn_w2, dconv_w, d_alog, d_dtbias, d_onorm, lconv_w, lconv_b, lru_wa, lru_ba, lru_wx, lru_bx, lru_lam, final_g):
    bp, lp, _ = x_prompt.shape
    bs, ls, _ = x_sample.shape
    assert lp % CHUNK == 0 and SUBLANES % ls == 0 and bp <= SUBLANES

    w_in_r = jnp.zeros((DEPTH, D_MODEL, N_PROJ), BF16)
    for dst, lo, hi in ((COL_Q, 0, OFF_A), (COL_X, OFF_X, N_IN), (COL_AB, OFF_A, OFF_X)):
        w_in_r = lax.dynamic_update_slice(w_in_r, w_in[..., lo:hi].astype(BF16), (0, 0, dst))
    w_out_b = w_out.astype(BF16)
    wa = lru_wa.astype(BF16)
    wx = lru_wx.astype(BF16)
    norm_g3 = norm_g.reshape(DEPTH * 3, 1, D_MODEL)
    lane_row = lambda v: jnp.pad(v, ((0, 0), (0, LANES - v.shape[1]))).reshape(DEPTH, 1, LANES)
    vec = lambda v: v.reshape(DEPTH, 1, W_B)
    wts = (w_in_r, w_out_b, norm_g3, (ffn_w1, ffn_w3, ffn_w2), dconv_w, lane_row(d_alog), lane_row(d_dtbias),
           d_onorm.reshape(DEPTH, 1, DV), lconv_w, vec(lconv_b), wa, wx, vec(lru_ba), vec(lru_bx),
           vec(lru_lam), final_g)

    ms = bs * ls
    c_all = jnp.concatenate([c_sample, c_prompt, jnp.zeros((SUBLANES - bp, D_MODEL), F32)], axis=0)
    mod_all = _ada(c_all, w_ada, b_ada, bs, ls)
    mod_p = _Mod(mod_all[:, ms:ms + bp].reshape(DEPTH, bp, 1, 9 * D_MODEL), False, lp)
    mod_s = _Mod(mod_all, True, ls)

    ffn_cast = {}
    ys, sd, sdc, sl_, slc = _trunk(x_sample.reshape(bs * ls, D_MODEL), mod_s, bs, ls, wts,
                                   (state_delta, state_delta_conv, state_lru, state_lru_conv), ffn_cast)
    yp, pd, pdc, pl_, plc = _trunk(x_prompt.reshape(bp * lp, D_MODEL), mod_p, bp, lp, wts, None, ffn_cast)
    return (yp.reshape(bp, lp, D_MODEL), ys.reshape(bs, ls, D_MODEL), pd, pdc, pl_, plc, sd, sdc, sl_, slc)
```

```python
import functools

import jax
import jax.numpy as jnp
from jax import lax
from jax.experimental import pallas as pl
from jax.experimental.pallas import tpu as pltpu

F32 = jnp.float32
BF16 = jnp.bfloat16

D_MODEL = 2048
DEPTH = 4
H_A = 8
DK = 128
DV = 128
W_QK = H_A * DK
W_A = H_A * DV
W_B = D_MODEL - W_A
LRU_BLOCKS = 8
LRU_BW = W_B // LRU_BLOCKS
LRU_C = 8.0
CONV_W = 4
D_FF = ((8 * D_MODEL // 3 + 127) // 128) * 128
HALF = 0.5
EPS = 1e-6
N_DCONV = 2 * W_QK + W_A
OFF_G = 2 * W_QK + W_A
OFF_A = OFF_G + W_A
OFF_X = OFF_A + 2 * H_A
N_IN = OFF_X + 2 * W_B

LANES = 128
SUBLANES = 8
VMEM_LIMIT = 56 * 1024 * 1024

COL_Q, COL_K, COL_V, COL_G = 0, W_QK, 2 * W_QK, 2 * W_QK + W_A
COL_X = COL_G + W_A
COL_Y = COL_X + W_B
COL_AB = COL_Y + W_B
N_PROJ = COL_AB + 2 * LANES

TM_ROWS = 512
TM_PROJ = 1024
TF_FFN = 512
TF_EMIT = 256
TN_PROJ = 1280
TN_ADA = 1024
CHUNK = 128
HG_PROMPT = 2
FACTOR_CHUNKS = 16
TB_SAMPLE = 16

assert N_PROJ % TN_PROJ == 0 and (9 * D_MODEL) % TN_ADA == 0 and D_FF % LANES == 0


def _cparams(*sem):
    return pltpu.CompilerParams(dimension_semantics=sem, vmem_limit_bytes=VMEM_LIMIT)


def _dot(a, b):
    return jnp.dot(a, b, preferred_element_type=F32)


def _dot_nt(a, b):
    return lax.dot_general(a, b, (((1,), (1,)), ((), ())), preferred_element_type=F32)


def _silu(x):
    return x * jax.nn.sigmoid(x)


def _expm1(x):
    u = jnp.exp(x)
    um1 = u - 1.0
    safe = um1 * x / jnp.log(jnp.where(um1 == 0.0, 2.0, u))
    return jnp.where(um1 == 0.0, x, jnp.where(x < -30.0, um1, safe))


def _norm_mod(x, g, shift, scale):
    var = jnp.mean(x * x, axis=-1, keepdims=True)
    return x * lax.rsqrt(var + EPS) * (g * (1.0 + scale)) + shift


def _row_iota(shape, mod):
    return jnp.bitwise_and(lax.broadcasted_iota(jnp.int32, shape, 0), mod - 1)


def _conv_zero_past(x_ref, w):
    rows = x_ref.shape[0]
    head = x_ref[:SUBLANES, :]
    head_y = _conv_rows(head, None, w, lax.broadcasted_iota(jnp.int32, head.shape, 0))
    y = x_ref[SUBLANES:, :] * w[CONV_W - 1:CONV_W, :]
    for j in range(1, CONV_W):
        y = y + x_ref[SUBLANES - j:rows - j, :] * w[CONV_W - 1 - j:CONV_W - j, :]
    return jnp.concatenate([head_y, y], axis=0)


def _conv_rows(x, pe, w, t):
    rows = x.shape[0]
    y = x * w[CONV_W - 1:CONV_W, :]
    for j in range(1, CONV_W):
        term = pltpu.roll(x, j, axis=0)
        if pe is None:
            term = jnp.where(t >= j, term, 0.0)
        else:
            back = CONV_W - 1 - j
            prev = pe if back == 0 else pltpu.roll(pe, rows - back, axis=0)
            term = jnp.where(t >= j, term, prev)
        y = y + term * w[CONV_W - 1 - j:CONV_W - j, :]
    return y


def _ada_body(c_ref, w_ref, b_ref, o_ref, *, nseq, reps):
    cs = _silu(c_ref[...]).astype(BF16)
    m = _dot(cs, w_ref[...].astype(BF16)) + b_ref[...]
    hi = m[:nseq].astype(BF16)
    r1 = m[:nseq] - hi.astype(F32)
    mid = r1.astype(BF16)
    lo = (r1 - mid.astype(F32)).astype(BF16)
    rows = nseq * reps
    src = lax.broadcasted_iota(jnp.int32, (rows, 3 * nseq), 1)
    dst = lax.div(lax.broadcasted_iota(jnp.int32, (rows, 3 * nseq), 0), reps)
    hit = (src == dst) | (src == dst + nseq) | (src == dst + 2 * nseq)
    pick = jnp.where(hit, 1.0, 0.0).astype(BF16)
    o_ref[:rows, :] = _dot(pick, jnp.concatenate([hi, mid, lo], axis=0))
    o_ref[rows:, :] = m[nseq:]


def _ada(c_all, w_ada, b_ada, nseq, reps):
    rows = c_all.shape[0]
    out_rows = nseq * reps + rows - nseq
    n9 = 9 * D_MODEL
    return pl.pallas_call(
        functools.partial(_ada_body, nseq=nseq, reps=reps),
        grid=(DEPTH, n9 // TN_ADA),
        in_specs=[
            pl.BlockSpec((rows, D_MODEL), lambda l, n: (0, 0)),
            pl.BlockSpec((None, D_MODEL, TN_ADA), lambda l, n: (l, 0, n)),
            pl.BlockSpec((None, 1, TN_ADA), lambda l, n: (l, 0, n)),
        ],
        out_specs=pl.BlockSpec((None, out_rows, TN_ADA), lambda l, n: (l, 0, n)),
        out_shape=jax.ShapeDtypeStruct((DEPTH, out_rows, n9), F32),
        compiler_params=_cparams("parallel", "parallel"),
        name="ada",
    )(c_all, w_ada, b_ada.reshape(DEPTH, 1, n9))


class _Mod:
    def __init__(self, arr, per_token, seq_len):
        self.arr = arr
        self.per_token = per_token
        self.seq_len = seq_len

    def spec(self, layer, tm, k, **kw):
        if self.per_token:
            return pl.BlockSpec((None, tm, D_MODEL), lambda i, *_: (layer, i, k), **kw)
        seq_len = self.seq_len
        return pl.BlockSpec((None, None, 1, D_MODEL), lambda i, *_: (layer, (i * tm) // seq_len, 0, k), **kw)


def _ffn_body(x_ref, g_ref, sh_ref, sc_ref, gt_ref, w1_ref, w3_ref, w2_ref, *rest, final, emit, tf):
    rest = list(rest)
    fg_ref = rest.pop(0) if final else None
    o_ref = rest.pop(0)
    if emit:
        w1b_ref, w3b_ref, w2b_ref = rest[:3]
        rest = rest[3:]
        w1b_ref[...] = w1_ref[...].astype(BF16)
        w3b_ref[...] = w3_ref[...].astype(BF16)
        w2b_ref[...] = w2_ref[...].astype(BF16)
        w1_ref, w3_ref, w2_ref = w1b_ref, w3b_ref, w2b_ref
    f = pl.program_id(1)
    last = pl.num_programs(1) - 1
    steps = -(-D_FF // tf)
    tail = D_FF - (steps - 1) * tf

    def epilogue(acc):
        y = x_ref[...] + HALF * gt_ref[...] * acc
        if final:
            var = jnp.mean(y * y, axis=-1, keepdims=True)
            y = y * lax.rsqrt(var + EPS) * fg_ref[...]
        o_ref[...] = y

    h_scr, acc_scr = rest

    def swiglu(width):
        h = h_scr[...]
        a = _dot(h, w1_ref[:, :width])
        b = _dot(h, w3_ref[:, :width])
        return _dot((_silu(a) * b).astype(BF16), w2_ref[:width, :])

    @pl.when(f == 0)
    def _():
        h_scr[...] = _norm_mod(x_ref[...], g_ref[...], sh_ref[...], sc_ref[...]).astype(BF16)
        acc_scr[...] = swiglu(tf)

    @pl.when((f > 0) & (f < last))
    def _():
        acc_scr[...] += swiglu(tf)

    @pl.when(f == last)
    def _():
        epilogue(acc_scr[...] + swiglu(tail))


def _ffn(x, mod, norm_g3, w13, w2, layer, s, final_g=None, emit=False):
    m = x.shape[0]
    tm = min(TM_ROWS, m)
    tf = TF_EMIT if emit else TF_FFN
    steps = -(-D_FF // tf)
    assert steps >= 2
    sub = 2 * s
    final = final_g is not None
    w1, w3 = w13
    if emit:
        assert m == tm
        once = dict(pipeline_mode=pl.Buffered(1))
        up = pl.BlockSpec((None, None, D_MODEL, tf), lambda i, f: (layer, s, 0, f))
        down = pl.BlockSpec((None, None, tf, D_MODEL), lambda i, f: (layer, s, f, 0))
    else:
        once = {}
        up = pl.BlockSpec((D_MODEL, tf), lambda i, f: (0, f))
        down = pl.BlockSpec((tf, D_MODEL), lambda i, f: (f, 0))
    in_specs = [
        pl.BlockSpec((tm, D_MODEL), lambda i, f: (i, 0), **once),
        pl.BlockSpec((None, 1, D_MODEL), lambda i, f: (layer * 3 + sub, 0, 0)),
        mod.spec(layer, tm, 3 * sub, **once), mod.spec(layer, tm, 3 * sub + 1, **once),
        mod.spec(layer, tm, 3 * sub + 2, **once),
        up, up, down,
    ]
    args = [x, norm_g3, mod.arr, mod.arr, mod.arr, w1, w3, w2]
    if final:
        in_specs.append(pl.BlockSpec((1, D_MODEL), lambda i, f: (0, 0)))
        args.append(final_g.reshape(1, D_MODEL))
    out_specs = [pl.BlockSpec((tm, D_MODEL), lambda i, f: (i, 0))]
    out_shape = [jax.ShapeDtypeStruct((m, D_MODEL), F32)]
    if emit:
        out_specs += [pl.BlockSpec((D_MODEL, tf), lambda i, f: (0, f))] * 2 + [pl.BlockSpec((tf, D_MODEL), lambda i, f: (f, 0))]
        out_shape += [jax.ShapeDtypeStruct((D_MODEL, D_FF), BF16)] * 2 + [jax.ShapeDtypeStruct((D_FF, D_MODEL), BF16)]
    outs = pl.pallas_call(
        functools.partial(_ffn_body, final=final, emit=emit, tf=tf),
        grid=(m // tm, steps),
        in_specs=in_specs,
        out_specs=out_specs,
        out_shape=out_shape,
        scratch_shapes=[pltpu.VMEM((tm, D_MODEL), BF16), pltpu.VMEM((tm, D_MODEL), F32)],
        compiler_params=_cparams("parallel", "arbitrary"),
        name="ffn_cast" if emit else "ffn",
    )(*args)
    return (outs[0], (outs[1], outs[2]), outs[3]) if emit else outs[0]


def _inproj_body(x_ref, g_ref, sh_ref, sc_ref, w_ref, o_ref, h_scr):
    n = pl.program_id(1)

    @pl.when(n == 0)
    def _():
        h_scr[...] = _norm_mod(x_ref[...], g_ref[...], sh_ref[...], sc_ref[...]).astype(BF16)
        o_ref[...] = _dot(h_scr[...], w_ref[...])

    @pl.when(n > 0)
    def _():
        o_ref[...] = _dot(h_scr[...], w_ref[...])


def _inproj(x, mod, norm_g3, w_in, layer):
    m = x.shape[0]
    tm = min(TM_PROJ, m)
    return pl.pallas_call(
        _inproj_body,
        grid=(m // tm, N_PROJ // TN_PROJ),
        in_specs=[
            pl.BlockSpec((tm, D_MODEL), lambda i, n: (i, 0)),
            pl.BlockSpec((None, 1, D_MODEL), lambda i, n: (layer * 3 + 1, 0, 0)),
            mod.spec(layer, tm, 3), mod.spec(layer, tm, 4),
            pl.BlockSpec((None, D_MODEL, TN_PROJ), lambda i, n: (layer, 0, n)),
        ],
        out_specs=pl.BlockSpec((tm, TN_PROJ), lambda i, n: (i, n)),
        out_shape=jax.ShapeDtypeStruct((m, N_PROJ), F32),
        scratch_shapes=[pltpu.VMEM((tm, D_MODEL), BF16)],
        compiler_params=_cparams("parallel", "arbitrary"),
        name="inproj",
    )(x, norm_g3, mod.arr, mod.arr, w_in)


def _outproj_body(x_ref, oa_ref, ob_ref, gt_ref, wa_ref, wb_ref, o_ref):
    acc = _dot(oa_ref[...], wa_ref[...]) + _dot(ob_ref[...], wb_ref[...])
    o_ref[...] = x_ref[...] + gt_ref[...] * acc


def _outproj(x, oa, ob, mod, w_out, layer):
    m = x.shape[0]
    tm = min(TM_ROWS, m)
    return pl.pallas_call(
        _outproj_body,
        grid=(m // tm,),
        in_specs=[
            pl.BlockSpec((tm, D_MODEL), lambda i: (i, 0)),
            pl.BlockSpec((tm, W_A), lambda i: (i, 0)),
            pl.BlockSpec((tm, W_B), lambda i: (i, 0)),
            mod.spec(layer, tm, 5),
            pl.BlockSpec((None, W_A, D_MODEL), lambda i: (layer, 0, 0)),
            pl.BlockSpec((None, W_B, D_MODEL), lambda i: (layer, 1, 0)),
        ],
        out_specs=pl.BlockSpec((tm, D_MODEL), lambda i: (i, 0)),
        out_shape=jax.ShapeDtypeStruct((m, D_MODEL), F32),
        compiler_params=_cparams("parallel"),
        name="outproj",
    )(x, oa, ob, mod.arr, w_out, w_out)


def _lru_body(*refs, seq_len, has_state):
    if has_state:
        (x_ref, y_ref, pe_ref, h0_ref, cw_ref, cb_ref, wa_ref, wx_ref, ba_ref, bx_ref, lam_ref,
         ob_ref, hl_ref, a_scr, b_scr) = refs
    else:
        (x_ref, y_ref, cw_ref, cb_ref, wa_ref, wx_ref, ba_ref, bx_ref, lam_ref,
         ob_ref, hl_ref, a_scr, b_scr) = refs
    rows = x_ref.shape[0]
    if has_state:
        xc = _conv_rows(x_ref[...], pe_ref[...], cw_ref[...], _row_iota(x_ref.shape, seq_len))
    else:
        xc = _conv_zero_past(x_ref, cw_ref[...])
    xc = xc + cb_ref[...]
    xcb = xc.astype(BF16)
    r = jax.nn.sigmoid(_dot(xcb, wa_ref[...]) + ba_ref[...])
    i = jax.nn.sigmoid(_dot(xcb, wx_ref[...]) + bx_ref[...])
    log_a = -LRU_C * r * jax.nn.softplus(-lam_ref[...])
    a = jnp.exp(log_a)
    b = jnp.sqrt(-_expm1(2.0 * log_a)) * (i * xc)
    if has_state:
        b = b + a * h0_ref[...]
    group = min(seq_len, SUBLANES)
    tg = _row_iota(x_ref.shape, group)
    d = 1
    while d < group:
        keep = tg >= d
        b = jnp.where(keep, a * pltpu.roll(b, d, axis=0) + b, b)
        a = jnp.where(keep, a * pltpu.roll(a, d, axis=0), a)
        d *= 2
    if seq_len > SUBLANES:
        a_scr[...] = a
        b_scr[...] = b

        def step(k, carry):
            r0 = pl.multiple_of(k * SUBLANES, SUBLANES)
            hk = b_scr[pl.ds(r0, SUBLANES), :] + a_scr[pl.ds(r0, SUBLANES), :] * carry
            b_scr[pl.ds(r0, SUBLANES), :] = hk
            return hk[SUBLANES - 1:SUBLANES, :]

        lax.fori_loop(0, rows // SUBLANES, step, jnp.zeros((1, LRU_BW), F32), unroll=8)
        h = b_scr[...]
    else:
        h = b
    ob_ref[...] = (h * jax.nn.gelu(y_ref[...])).astype(BF16)
    hl_ref[...] = h[rows - hl_ref.shape[0]:, :]


def _lru(proj, seq_len, layer, lconv_w, lconv_b, wa, wx, ba, bx, lam, pe=None, h0e=None):
    m = proj.shape[0]
    has_state = pe is not None
    rows = seq_len if not has_state else m
    nseq = m // rows
    hl_rows = rows if has_state else SUBLANES
    bx0, by0 = COL_X // LRU_BW, COL_Y // LRU_BW
    col = lambda b, n: (b, n)
    vec = pl.BlockSpec((None, 1, LRU_BW), lambda b, n: (layer, 0, n))
    blk = pl.BlockSpec((None, None, LRU_BW, LRU_BW), lambda b, n: (layer, n, 0, 0))
    in_specs = [pl.BlockSpec((rows, LRU_BW), lambda b, n: (b, bx0 + n)),
                pl.BlockSpec((rows, LRU_BW), lambda b, n: (b, by0 + n))]
    args = [proj, proj]
    if has_state:
        in_specs += [pl.BlockSpec((rows, LRU_BW), col), pl.BlockSpec((rows, LRU_BW), col)]
        args += [pe, h0e]
    in_specs += [pl.BlockSpec((None, CONV_W, LRU_BW), lambda b, n: (layer, 0, n)), vec, blk, blk, vec, vec, vec]
    args += [lconv_w, lconv_b, wa, wx, ba, bx, lam]
    return pl.pallas_call(
        functools.partial(_lru_body, seq_len=seq_len, has_state=has_state),
        grid=(nseq, LRU_BLOCKS),
        in_specs=in_specs,
        out_specs=[pl.BlockSpec((rows, LRU_BW), col), pl.BlockSpec((hl_rows, LRU_BW), col)],
        out_shape=[jax.ShapeDtypeStruct((m, W_B), BF16), jax.ShapeDtypeStruct((nseq * hl_rows, W_B), F32)],
        scratch_shapes=[pltpu.VMEM((rows, LRU_BW), F32), pltpu.VMEM((rows, LRU_BW), F32)],
        compiler_params=_cparams("parallel", "parallel"),
        name="lru",
    )(*args)


def _qkv_prep(q_conv, k_conv, v_conv, heads):
    q, k, v = _silu(q_conv), _silu(k_conv), _silu(v_conv)
    qs, ks = [], []
    for h in range(heads):
        sl = slice(h * DK, (h + 1) * DK)
        qh, kh = q[:, sl], k[:, sl]
        qs.append(qh * (lax.rsqrt(jnp.sum(qh * qh, axis=-1, keepdims=True) + EPS) * DK ** -0.5))
        ks.append(kh * lax.rsqrt(jnp.sum(kh * kh, axis=-1, keepdims=True) + EPS))
    return jnp.concatenate(qs, axis=1), jnp.concatenate(ks, axis=1), v


def _gate_norm(o, onorm, gate_pre):
    var = jnp.mean(o * o, axis=-1, keepdims=True)
    return o * lax.rsqrt(var + EPS) * onorm * _silu(gate_pre)


def _seg_cumsum(g, tg, group):
    d = 1
    while d < group:
        g = jnp.where(tg >= d, g + pltpu.roll(g, d, axis=0), g)
        d *= 2
    return g


def _delta_prompt_body(q_ref, k_ref, v_ref, gt_ref, ab_ref, cwq_ref, cwk_ref, cwv_ref, alog_ref, dtb_ref,
                       on_ref, oa_ref, s_ref, qn_scr, kn_scr, vv_scr, gc_scr, bt_scr, s_scr,
                       wq_scr, uc_scr, qk_scr, kdt_scr, sd_scr):
    rows = q_ref.shape[0]
    hg = q_ref.shape[1] // DK
    qn, kn, vv = _qkv_prep(_conv_zero_past(q_ref, cwq_ref[...]), _conv_zero_past(k_ref, cwk_ref[...]),
                           _conv_zero_past(v_ref, cwv_ref[...]), hg)
    qn_scr[...] = qn
    kn_scr[...] = kn
    vv_scr[...] = vv
    ab = ab_ref[...]
    g = -jnp.exp(alog_ref[...]) * jax.nn.softplus(ab + dtb_ref[...])
    gc_scr[...] = _seg_cumsum(g, _row_iota(ab.shape, CHUNK), CHUNK)
    bt_scr[...] = jax.nn.sigmoid(ab)
    s_scr[...] = jnp.zeros_like(s_scr)
    head0 = pl.program_id(1) * hg
    onorm = on_ref[...]

    def factor(step, carry):
        lane = lax.broadcasted_iota(jnp.int32, (CHUNK, LANES), 1)
        ri = lax.broadcasted_iota(jnp.int32, (CHUNK, CHUNK), 0)
        ci = lax.broadcasted_iota(jnp.int32, (CHUNK, CHUNK), 1)
        causal = ri >= ci
        strict = ri > ci
        blk = jnp.bitwise_xor(ri, ci)
        chains = []
        for cc in range(FACTOR_CHUNKS):
            c = step * FACTOR_CHUNKS + cc
            rs = pl.ds(pl.multiple_of(c * CHUNK, CHUNK), CHUNK)
            gcc = gc_scr[rs, :]
            btc = bt_scr[rs, :]
            for hh in range(hg):
                sl = slice(hh * DK, (hh + 1) * DK)
                gcol = jnp.sum(jnp.where(lane == head0 + hh, gcc, 0.0), axis=1, keepdims=True)
                bcol = jnp.sum(jnp.where(lane == head0 + hh + H_A, btc, 0.0), axis=1, keepdims=True)
                gmat = jnp.broadcast_to(gcol, (CHUNK, CHUNK))
                decay = jnp.where(causal, jnp.exp(jnp.where(causal, gmat - gmat.T, 0.0)), 0.0)
                q = qn_scr[rs, sl]
                k = kn_scr[rs, sl]
                kq = _dot_nt(jnp.concatenate([k, q], axis=0).astype(BF16), k.astype(BF16))
                m = jnp.where(strict, bcol * kq[:CHUNK] * decay, 0.0)
                eg = jnp.exp(gcol)
                glast = jnp.broadcast_to(gcol[CHUNK - 1:CHUNK, :], (CHUNK, 1))
                wq_scr[hh, c, CHUNK:, :] = (q * eg).astype(BF16)
                qk_scr[hh, rs, :] = (kq[CHUNK:] * decay).astype(BF16)
                kdt_scr[hh, rs, :] = (k * jnp.exp(glast - gcol)).T.astype(BF16)
                sd_scr[hh, pl.ds(pl.multiple_of(c * SUBLANES, SUBLANES), SUBLANES), :] = jnp.broadcast_to(
                    jnp.broadcast_to(eg, (CHUNK, DV))[CHUNK - 1:CHUNK, :], (SUBLANES, DV))
                chains.append((hh, c, rs, sl, m, bcol, eg))
        es = [-jnp.where(lax.shift_right_logical(blk, 1) == 0, ch[4], 0.0) for ch in chains]
        lg = 1
        while (1 << lg) < CHUNK:
            cross = lax.shift_right_logical(blk, lg) == 1
            offs = [jnp.where(cross, ch[4], 0.0) for ch in chains]
            ebs = [e.astype(BF16) for e in es]
            xs = [off + _dot(eb, off.astype(BF16)) for off, eb in zip(offs, ebs)]
            es = [e - (x + _dot(x.astype(BF16), eb)) for e, x, eb in zip(es, xs, ebs)]
            lg += 1
        for (hh, c, rs, sl, m, bcol, eg), e in zip(chains, es):
            rhs = jnp.concatenate([kn_scr[rs, sl] * (bcol * eg), vv_scr[rs, sl] * bcol], axis=1)
            sol = rhs + _dot(e.astype(BF16), rhs.astype(BF16))
            wq_scr[hh, c, :CHUNK, :] = sol[:, :DK].astype(BF16)
            uc_scr[hh, rs, :] = sol[:, DK:]
        return carry

    lax.fori_loop(0, rows // (CHUNK * FACTOR_CHUNKS), factor, 0)

    def recur(c, carry):
        rs = pl.ds(pl.multiple_of(c * CHUNK, CHUNK), CHUNK)
        s_old = [s_scr[hh] for hh in range(hg)]
        ps = [_dot(wq_scr[hh, c], s_old[hh].astype(BF16)) for hh in range(hg)]
        us = [(uc_scr[hh, rs, :] - ps[hh][:CHUNK]).astype(BF16) for hh in range(hg)]
        os_ = [ps[hh][CHUNK:] + _dot(qk_scr[hh, rs, :], us[hh]) for hh in range(hg)]
        for hh in range(hg):
            sdec = sd_scr[hh, pl.ds(pl.multiple_of(c * SUBLANES, SUBLANES), SUBLANES), :]
            s_scr[hh] = s_old[hh] * sdec[:1, :] + _dot(kdt_scr[hh, rs, :], us[hh])
        for hh in range(hg):
            sl = slice(hh * DK, (hh + 1) * DK)
            oa_ref[rs, sl] = _gate_norm(os_[hh], onorm, gt_ref[rs, sl]).astype(BF16)
        return carry

    lax.fori_loop(0, rows // CHUNK, recur, 0, unroll=2)
    s_ref[...] = s_scr[...]


def _delta_prompt(proj, batch, seq_len, layer, dconv_w, alog_row, dtb_row, onorm):
    m = proj.shape[0]
    hg = HG_PROMPT
    wcols = hg * DK
    nchunks = seq_len // CHUNK
    assert nchunks % FACTOR_CHUNKS == 0
    nq, nk, nv, ng = (c // wcols for c in (COL_Q, COL_K, COL_V, COL_G))
    colspec = lambda off: pl.BlockSpec((seq_len, wcols), lambda b, j: (b, off + j))
    cwspec = lambda off: pl.BlockSpec((None, CONV_W, wcols), lambda b, j: (layer, 0, off + j))
    row = pl.BlockSpec((None, 1, LANES), lambda b, j: (layer, 0, 0))
    return pl.pallas_call(
        _delta_prompt_body,
        grid=(batch, H_A // hg),
        in_specs=[colspec(nq), colspec(nk), colspec(nv), colspec(ng),
                  pl.BlockSpec((seq_len, LANES), lambda b, j: (b, COL_AB // LANES)),
                  cwspec(nq), cwspec(nk), cwspec(nv), row, row, row],
        out_specs=[pl.BlockSpec((seq_len, wcols), lambda b, j: (b, j)),
                   pl.BlockSpec((None, hg, DK, DV), lambda b, j: (b, j, 0, 0))],
        out_shape=[jax.ShapeDtypeStruct((m, W_A), BF16), jax.ShapeDtypeStruct((batch, H_A, DK, DV), F32)],
        scratch_shapes=[pltpu.VMEM((seq_len, wcols), F32)] * 3
        + [pltpu.VMEM((seq_len, LANES), F32)] * 2 + [pltpu.VMEM((hg, DK, DV), F32)]
        + [pltpu.VMEM((hg, nchunks, 2 * CHUNK, DK), BF16), pltpu.VMEM((hg, seq_len, DV), F32),
           pltpu.VMEM((hg, seq_len, CHUNK), BF16), pltpu.VMEM((hg, seq_len, CHUNK), BF16),
           pltpu.VMEM((hg, nchunks * SUBLANES, DV), F32)],
        compiler_params=_cparams("parallel", "parallel"),
        name="delta_prompt",
    )(proj, proj, proj, proj, proj, dconv_w, dconv_w, dconv_w, alog_row, dtb_row, onorm)


def _delta_sample_body(q_ref, k_ref, v_ref, gt_ref, ab_ref, pe_ref, cw_ref, alog_ref, dtb_ref, on_ref, s0_ref,
                       *rest, seq_len):
    oa_ref, s_ref, qn_scr, kn_scr, vv_scr, gc_scr, bt_scr = rest[-7:]
    rows = q_ref.shape[0]
    t = _row_iota(q_ref.shape, seq_len)
    pe = pe_ref[...]
    cw = cw_ref[...]
    qn, kn, vv = _qkv_prep(_conv_rows(q_ref[...], pe[:, COL_Q:COL_K], cw[:, COL_Q:COL_K], t),
                           _conv_rows(k_ref[...], pe[:, COL_K:COL_V], cw[:, COL_K:COL_V], t),
                           _conv_rows(v_ref[...], pe[:, COL_V:COL_G], cw[:, COL_V:COL_G], t), H_A)
    qn_scr[...] = qn
    kn_scr[...] = kn
    vv_scr[...] = vv
    ab = ab_ref[...]
    g = -jnp.exp(alog_ref[...]) * jax.nn.softplus(ab + dtb_ref[...])
    gc_scr[...] = _seg_cumsum(g, _row_iota(ab.shape, seq_len), seq_len)
    bt_scr[...] = jax.nn.sigmoid(ab)
    onorm = on_ref[...]
    per_tile = SUBLANES // seq_len

    def tile(p, carry):
        tt = _row_iota((SUBLANES, LANES), seq_len)
        tt2 = _row_iota((SUBLANES, DK + DV), seq_len)
        ri = lax.broadcasted_iota(jnp.int32, (SUBLANES, LANES), 0)
        ri2 = _row_iota((2 * SUBLANES, LANES), SUBLANES)
        lane = lax.broadcasted_iota(jnp.int32, (LANES, LANES), 1)
        zpad = jnp.zeros((LANES - SUBLANES, DK), F32)
        rs = pl.ds(pl.multiple_of(p * SUBLANES, SUBLANES), SUBLANES)
        gcc, btc = gc_scr[rs, :], bt_scr[rs, :]
        work = [(p, rs, h, gcc, btc) for h in range(H_A)]
        staged = []
        for p, rs, h, gcc, btc in work:
            sl = slice(h * DK, (h + 1) * DK)
            gc = jnp.broadcast_to(gcc[:, h:h + 1], (SUBLANES, LANES))
            beta = jnp.broadcast_to(btc[:, H_A + h:H_A + h + 1], (SUBLANES, LANES))
            q = qn_scr[rs, sl]
            k = kn_scr[rs, sl]
            v = vv_scr[rs, sl]
            eg = jnp.exp(gc)
            decs, ms = [None], [None]
            for d in range(1, seq_len):
                ok = tt >= d
                dec = jnp.where(ok, jnp.exp(jnp.where(ok, gc - pltpu.roll(gc, d, axis=0), 0.0)), 0.0)
                kk = jnp.sum(k * pltpu.roll(k, d, axis=0), axis=-1, keepdims=True)
                decs.append(dec)
                md = beta * kk * dec
                ms.append(jnp.concatenate([md, md], axis=1))
            rhs = jnp.concatenate([k * (beta * eg), v * beta], axis=1)
            sol = rhs
            for step in range(1, seq_len):
                acc = ms[1] * pltpu.roll(sol, 1, axis=0)
                for d in range(2, seq_len):
                    acc = acc + ms[d] * pltpu.roll(sol, d, axis=0)
                sol = jnp.where(tt2 == step, rhs - acc, sol)
            lhs = jnp.concatenate([sol[:, :DK], q * eg], axis=0).astype(BF16)
            staged.append((sl, gc, q, k, eg, decs, sol[:, DK:], lhs))
        pps = []
        for (p, rs, h, _, _), st in zip(work, staged):
            lhs = st[-1]
            pp = _dot(lhs, s0_ref[p * per_tile, h].astype(BF16))
            for bb in range(1, per_tile):
                pp = jnp.where(ri2 >= bb * seq_len, _dot(lhs, s0_ref[p * per_tile + bb, h].astype(BF16)), pp)
            pps.append(pp)
        updates = []
        for (p, rs, h, _, _), st, pp in zip(work, staged, pps):
            sl, gc, q, k, eg, decs, uc, _ = st
            u = uc - pp[:SUBLANES]
            o = pp[SUBLANES:] + jnp.sum(q * k, axis=-1, keepdims=True) * u
            for d in range(1, seq_len):
                qk = jnp.sum(q * pltpu.roll(k, d, axis=0), axis=-1, keepdims=True)
                o = o + (qk * decs[d]) * pltpu.roll(u, d, axis=0)
            oa_ref[rs, sl] = _gate_norm(o, onorm, gt_ref[rs, sl]).astype(BF16)
            glast = jnp.broadcast_to(gc[seq_len - 1:seq_len, :], (SUBLANES, LANES))
            for bb in range(1, per_tile):
                last = bb * seq_len + seq_len - 1
                glast = jnp.where(ri >= bb * seq_len, jnp.broadcast_to(gc[last:last + 1, :], (SUBLANES, LANES)), glast)
            kd = k * jnp.exp(glast - gc)
            kdt = jnp.concatenate([kd, zpad], axis=0).T
            updates.append((kdt, jnp.concatenate([u, zpad], axis=0).astype(BF16)))
        for (p, rs, h, _, _), st, (kdt, upad) in zip(work, staged, updates):
            eg = st[4]
            for bb in range(per_tile):
                last = bb * seq_len + seq_len - 1
                cols = (lane >= bb * seq_len) & (lane < (bb + 1) * seq_len)
                kd_b = jnp.where(cols, kdt, 0.0).astype(BF16)
                s_ref[p * per_tile + bb, h] = (s0_ref[p * per_tile + bb, h] * eg[last:last + 1, :]
                                               + _dot(kd_b, upad))
        return carry

    lax.fori_loop(0, rows // SUBLANES, tile, 0)


def _delta_sample(proj, seq_len, layer, pe, dconv_w, alog_row, dtb_row, onorm, s0, s_stack):
    m = proj.shape[0]
    tb = TB_SAMPLE
    rows = tb * seq_len
    colspec = lambda off: pl.BlockSpec((rows, W_A), lambda i: (i, off // W_A))
    row = pl.BlockSpec((None, 1, LANES), lambda i: (layer, 0, 0))
    sspec = pl.BlockSpec((None, tb, H_A, DK, DV), lambda i: (layer, i, 0, 0, 0))
    in_specs = [colspec(COL_Q), colspec(COL_K), colspec(COL_V), colspec(COL_G),
                pl.BlockSpec((rows, LANES), lambda i: (i, COL_AB // LANES)),
                pl.BlockSpec((rows, N_DCONV), lambda i: (i, 0)),
                pl.BlockSpec((None, CONV_W, N_DCONV), lambda i: (layer, 0, 0)),
                row, row, row, sspec]
    args = [proj, proj, proj, proj, proj, pe, dconv_w, alog_row, dtb_row, onorm, s0]
    aliases = {}
    if s_stack is not None:
        in_specs.append(pl.BlockSpec(memory_space=pl.ANY))
        aliases = {len(args): 1}
        args.append(s_stack)
    return pl.pallas_call(
        functools.partial(_delta_sample_body, seq_len=seq_len),
        grid=(m // rows,),
        in_specs=in_specs,
        out_specs=[pl.BlockSpec((rows, W_A), lambda i: (i, 0)), sspec],
        out_shape=[jax.ShapeDtypeStruct((m, W_A), BF16), jax.ShapeDtypeStruct(s0.shape, F32)],
        input_output_aliases=aliases,
        scratch_shapes=[pltpu.VMEM((rows, W_A), F32)] * 3 + [pltpu.VMEM((rows, LANES), F32)] * 2,
        compiler_params=_cparams("parallel"),
        name="delta_sample",
    )(*args)


def _pad_rows(state, seq_len):
    b, r, c = state.shape
    return jnp.pad(state, ((0, 0), (0, seq_len - r), (0, 0))).reshape(b * seq_len, c)


def _trunk(x, mod, batch, seq_len, wts, states, ffn_cast):
    (w_in, w_out, norm_g3, ffn_f32, dconv_w, alog_row, dtb_row, onorm, lconv_w, lconv_b,
     wa, wx, ba, bx, lam, final_g) = wts
    new_d, new_dc, new_l, new_lc = [], [], [], []
    s_stack = None

    def ffn(x, layer, s, fg=None):
        if (layer, s) in ffn_cast:
            return _ffn(x, mod, norm_g3, *ffn_cast[(layer, s)], layer, s, fg)
        y, w13, w2 = _ffn(x, mod, norm_g3, ffn_f32[:2], ffn_f32[2], layer, s, fg, emit=True)
        ffn_cast[(layer, s)] = (w13, w2)
        return y

    for layer in range(DEPTH):
        x = ffn(x, layer, 0)
        proj = _inproj(x, mod, norm_g3, w_in, layer)
        if states is None:
            oa, sd = _delta_prompt(proj, batch, seq_len, layer, dconv_w, alog_row, dtb_row, onorm)
            new_d.append(sd)
            ob, hl = _lru(proj, seq_len, layer, lconv_w, lconv_b, wa, wx, ba, bx, lam)
            sl = hl.reshape(batch, SUBLANES, W_B)[:, SUBLANES - 1]
        else:
            s_delta, s_dconv, s_lru, s_lconv = states
            oa, s_stack = _delta_sample(proj, seq_len, layer, _pad_rows(s_dconv[layer], seq_len),
                                        dconv_w, alog_row, dtb_row, onorm, s_delta, s_stack)
            ob, hl = _lru(proj, seq_len, layer, lconv_w, lconv_b, wa, wx, ba, bx, lam,
                          pe=_pad_rows(s_lconv[layer], seq_len),
                          h0e=_pad_rows(s_lru[layer][:, None, :], seq_len))
            sl = hl.reshape(batch, seq_len, W_B)[:, seq_len - 1]
        x = _outproj(x, oa, ob, mod, w_out, layer)
        x = ffn(x, layer, 1, final_g if layer == DEPTH - 1 else None)
        first_row = seq_len - (CONV_W - 1)
        if seq_len % SUBLANES == 0:
            ends = proj.reshape(batch, seq_len, N_PROJ)[:, first_row:]
            last_rows = lambda lo, hi: ends[:, :, lo:hi]
        else:
            last_rows = lambda lo, hi: proj[:, lo:hi].reshape(batch, seq_len, hi - lo)[:, first_row:]
        new_dc.append(last_rows(COL_Q, COL_G))
        new_l.append(sl)
        new_lc.append(last_rows(COL_X, COL_Y))
    new_delta = jnp.stack(new_d) if states is None else s_stack
    return x, new_delta, jnp.stack(new_dc), jnp.stack(new_l), jnp.stack(new_lc)


def kernel(x_prompt, x_sample, c_prompt, c_sample, state_delta, state_delta_conv, state_lru, state_lru_conv, w_in, w_out, norm_g, w_ada, b_ada, ffn_w1, ffn_w3, ffn_w2, dconv_w, d_alog, d_dtbias, d_onorm, lconv_w, lconv_b, lru_wa, lru_ba, lru_wx, lru_bx, lru_lam, final_g):
    bp, lp, _ = x_prompt.shape
    bs, ls, _ = x_sample.shape
    assert lp % CHUNK == 0 and SUBLANES % ls == 0 and bp <= SUBLANES

    w_in_r = jnp.zeros((DEPTH, D_MODEL, N_PROJ), BF16)
    for dst, lo, hi in ((COL_Q, 0, OFF_A), (COL_X, OFF_X, N_IN), (COL_AB, OFF_A, OFF_X)):
        w_in_r = lax.dynamic_update_slice(w_in_r, w_in[..., lo:hi].astype(BF16), (0, 0, dst))
    w_out_b = w_out.astype(BF16)
    wa = lru_wa.astype(BF16)
    wx = lru_wx.astype(BF16)
    norm_g3 = norm_g.reshape(DEPTH * 3, 1, D_MODEL)
    lane_row = lambda v: jnp.pad(v, ((0, 0), (0, LANES - v.shape[1]))).reshape(DEPTH, 1, LANES)
    vec = lambda v: v.reshape(DEPTH, 1, W_B)
    wts = (w_in_r, w_out_b, norm_g3, (ffn_w1, ffn_w3, ffn_w2), dconv_w, lane_row(d_alog), lane_row(d_dtbias),
           d_onorm.reshape(DEPTH, 1, DV), lconv_w, vec(lconv_b), wa, wx, vec(lru_ba), vec(lru_bx),
           vec(lru_lam), final_g)

    ms = bs * ls
    c_all = jnp.concatenate([c_sample, c_prompt, jnp.zeros((SUBLANES - bp, D_MODEL), F32)], axis=0)
    mod_all = _ada(c_all, w_ada, b_ada, bs, ls)
    mod_p = _Mod(mod_all[:, ms:ms + bp].reshape(DEPTH, bp, 1, 9 * D_MODEL), False, lp)
    mod_s = _Mod(mod_all, True, ls)

    ffn_cast = {}
    ys, sd, sdc, sl_, slc = _trunk(x_sample.reshape(bs * ls, D_MODEL), mod_s, bs, ls, wts,
                                   (state_delta, state_delta_conv, state_lru, state_lru_conv), ffn_cast)
    yp, pd, pdc, pl_, plc = _trunk(x_prompt.reshape(bp * lp, D_MODEL), mod_p, bp, lp, wts, None, ffn_cast)
    return (yp.reshape(bp, lp, D_MODEL), ys.reshape(bs, ls, D_MODEL), pd, pdc, pl_, plc, sd, sdc, sl_, slc)
```

```python
import functools

import jax
import jax.numpy as jnp
from jax import lax
from jax.experimental import pallas as pl
from jax.experimental.pallas import tpu as pltpu

F32 = jnp.float32
BF16 = jnp.bfloat16

D_MODEL = 2048
DEPTH = 4
H_A = 8
DK = 128
DV = 128
W_QK = H_A * DK
W_A = H_A * DV
W_B = D_MODEL - W_A
LRU_BLOCKS = 8
LRU_BW = W_B // LRU_BLOCKS
LRU_C = 8.0
CONV_W = 4
D_FF = ((8 * D_MODEL // 3 + 127) // 128) * 128
HALF = 0.5
EPS = 1e-6
N_DCONV = 2 * W_QK + W_A
OFF_G = 2 * W_QK + W_A
OFF_A = OFF_G + W_A
OFF_X = OFF_A + 2 * H_A
N_IN = OFF_X + 2 * W_B

LANES = 128
SUBLANES = 8
VMEM_LIMIT = 56 * 1024 * 1024

COL_Q, COL_K, COL_V, COL_G = 0, W_QK, 2 * W_QK, 2 * W_QK + W_A
COL_X = COL_G + W_A
COL_Y = COL_X + W_B
COL_AB = COL_Y + W_B
N_PROJ = COL_AB + 2 * LANES

TM_ROWS = 512
TM_PROJ = 1024
TF_FFN = 512
TF_EMIT = 256
TN_PROJ = 1280
TN_ADA = 1024
CHUNK = 128
HG_PROMPT = 2
FACTOR_CHUNKS = 16
TB_SAMPLE = 16

assert N_PROJ % TN_PROJ == 0 and (9 * D_MODEL) % TN_ADA == 0 and D_FF % LANES == 0


def _cparams(*sem):
    return pltpu.CompilerParams(dimension_semantics=sem, vmem_limit_bytes=VMEM_LIMIT)


def _dot(a, b):
    return jnp.dot(a, b, preferred_element_type=F32)


def _dot_nt(a, b):
    return lax.dot_general(a, b, (((1,), (1,)), ((), ())), preferred_element_type=F32)


def _silu(x):
    return x * jax.nn.sigmoid(x)


def _one_minus_exp2x(x):
    t = jnp.tanh(x)
    return -2.0 * t / (1.0 - t)


def _norm_mod(x, g, shift, scale):
    var = jnp.mean(x * x, axis=-1, keepdims=True)
    return x * lax.rsqrt(var + EPS) * (g * (1.0 + scale)) + shift


def _row_iota(shape, mod):
    return jnp.bitwise_and(lax.broadcasted_iota(jnp.int32, shape, 0), mod - 1)


def _conv_zero_past(x_ref, w):
    rows = x_ref.shape[0]
    head = x_ref[:SUBLANES, :]
    head_y = _conv_rows(head, None, w, lax.broadcasted_iota(jnp.int32, head.shape, 0))
    y = x_ref[SUBLANES:, :] * w[CONV_W - 1:CONV_W, :]
    for j in range(1, CONV_W):
        y = y + x_ref[SUBLANES - j:rows - j, :] * w[CONV_W - 1 - j:CONV_W - j, :]
    return jnp.concatenate([head_y, y], axis=0)


def _conv_rows(x, pe, w, t):
    rows = x.shape[0]
    y = x * w[CONV_W - 1:CONV_W, :]
    for j in range(1, CONV_W):
        term = pltpu.roll(x, j, axis=0)
        if pe is None:
            term = jnp.where(t >= j, term, 0.0)
        else:
            back = CONV_W - 1 - j
            prev = pe if back == 0 else pltpu.roll(pe, rows - back, axis=0)
            term = jnp.where(t >= j, term, prev)
        y = y + term * w[CONV_W - 1 - j:CONV_W - j, :]
    return y


def _ada_body(c_ref, w_ref, b_ref, o_ref, *, nseq, reps):
    cs = _silu(c_ref[...]).astype(BF16)
    m = _dot(cs, w_ref[...].astype(BF16)) + b_ref[...]
    hi = m[:nseq].astype(BF16)
    r1 = m[:nseq] - hi.astype(F32)
    mid = r1.astype(BF16)
    lo = (r1 - mid.astype(F32)).astype(BF16)
    rows = nseq * reps
    src = lax.broadcasted_iota(jnp.int32, (rows, 3 * nseq), 1)
    dst = lax.div(lax.broadcasted_iota(jnp.int32, (rows, 3 * nseq), 0), reps)
    hit = (src == dst) | (src == dst + nseq) | (src == dst + 2 * nseq)
    pick = jnp.where(hit, 1.0, 0.0).astype(BF16)
    o_ref[:rows, :] = _dot(pick, jnp.concatenate([hi, mid, lo], axis=0))
    o_ref[rows:, :] = m[nseq:]


def _ada(c_all, w_ada, b_ada, nseq, reps):
    rows = c_all.shape[0]
    out_rows = nseq * reps + rows - nseq
    n9 = 9 * D_MODEL
    return pl.pallas_call(
        functools.partial(_ada_body, nseq=nseq, reps=reps),
        grid=(DEPTH, n9 // TN_ADA),
        in_specs=[
            pl.BlockSpec((rows, D_MODEL), lambda l, n: (0, 0)),
            pl.BlockSpec((None, D_MODEL, TN_ADA), lambda l, n: (l, 0, n)),
            pl.BlockSpec((None, 1, TN_ADA), lambda l, n: (l, 0, n)),
        ],
        out_specs=pl.BlockSpec((None, out_rows, TN_ADA), lambda l, n: (l, 0, n)),
        out_shape=jax.ShapeDtypeStruct((DEPTH, out_rows, n9), F32),
        compiler_params=_cparams("parallel", "parallel"),
        name="ada",
    )(c_all, w_ada, b_ada.reshape(DEPTH, 1, n9))


class _Mod:
    def __init__(self, arr, per_token, seq_len):
        self.arr = arr
        self.per_token = per_token
        self.seq_len = seq_len

    def spec(self, layer, tm, k, **kw):
        if self.per_token:
            return pl.BlockSpec((None, tm, D_MODEL), lambda i, *_: (layer, i, k), **kw)
        seq_len = self.seq_len
        return pl.BlockSpec((None, None, 1, D_MODEL), lambda i, *_: (layer, (i * tm) // seq_len, 0, k), **kw)


def _ffn_body(x_ref, g_ref, sh_ref, sc_ref, gt_ref, w1_ref, w3_ref, w2_ref, *rest, final, emit, tf):
    rest = list(rest)
    fg_ref = rest.pop(0) if final else None
    o_ref = rest.pop(0)
    if emit:
        w1b_ref, w3b_ref, w2b_ref = rest[:3]
        rest = rest[3:]
        w1b_ref[...] = w1_ref[...].astype(BF16)
        w3b_ref[...] = w3_ref[...].astype(BF16)
        w2b_ref[...] = w2_ref[...].astype(BF16)
        w1_ref, w3_ref, w2_ref = w1b_ref, w3b_ref, w2b_ref
    f = pl.program_id(1)
    last = pl.num_programs(1) - 1
    steps = -(-D_FF // tf)
    tail = D_FF - (steps - 1) * tf

    def epilogue(acc):
        y = x_ref[...] + HALF * gt_ref[...] * acc
        if final:
            var = jnp.mean(y * y, axis=-1, keepdims=True)
            y = y * lax.rsqrt(var + EPS) * fg_ref[...]
        o_ref[...] = y

    h_scr, acc_scr = rest

    def swiglu(width):
        h = h_scr[...]
        a = _dot(h, w1_ref[:, :width])
        b = _dot(h, w3_ref[:, :width])
        return _dot((_silu(a) * b).astype(BF16), w2_ref[:width, :])

    @pl.when(f == 0)
    def _():
        h_scr[...] = _norm_mod(x_ref[...], g_ref[...], sh_ref[...], sc_ref[...]).astype(BF16)
        acc_scr[...] = swiglu(tf)

    @pl.when((f > 0) & (f < last))
    def _():
        acc_scr[...] += swiglu(tf)

    @pl.when(f == last)
    def _():
        epilogue(acc_scr[...] + swiglu(tail))


def _ffn(x, mod, norm_g3, w13, w2, layer, s, final_g=None, emit=False):
    m = x.shape[0]
    tm = min(TM_ROWS, m)
    tf = TF_EMIT if emit else TF_FFN
    steps = -(-D_FF // tf)
    assert steps >= 2
    sub = 2 * s
    final = final_g is not None
    w1, w3 = w13
    if emit:
        assert m == tm
        once = dict(pipeline_mode=pl.Buffered(1))
        up = pl.BlockSpec((None, None, D_MODEL, tf), lambda i, f: (layer, s, 0, f))
        down = pl.BlockSpec((None, None, tf, D_MODEL), lambda i, f: (layer, s, f, 0))
    else:
        once = {}
        up = pl.BlockSpec((D_MODEL, tf), lambda i, f: (0, f))
        down = pl.BlockSpec((tf, D_MODEL), lambda i, f: (f, 0))
    in_specs = [
        pl.BlockSpec((tm, D_MODEL), lambda i, f: (i, 0), **once),
        pl.BlockSpec((None, 1, D_MODEL), lambda i, f: (layer * 3 + sub, 0, 0)),
        mod.spec(layer, tm, 3 * sub, **once), mod.spec(layer, tm, 3 * sub + 1, **once),
        mod.spec(layer, tm, 3 * sub + 2, **once),
        up, up, down,
    ]
    args = [x, norm_g3, mod.arr, mod.arr, mod.arr, w1, w3, w2]
    if final:
        in_specs.append(pl.BlockSpec((1, D_MODEL), lambda i, f: (0, 0)))
        args.append(final_g.reshape(1, D_MODEL))
    out_specs = [pl.BlockSpec((tm, D_MODEL), lambda i, f: (i, 0))]
    out_shape = [jax.ShapeDtypeStruct((m, D_MODEL), F32)]
    if emit:
        out_specs += [pl.BlockSpec((D_MODEL, tf), lambda i, f: (0, f))] * 2 + [pl.BlockSpec((tf, D_MODEL), lambda i, f: (f, 0))]
        out_shape += [jax.ShapeDtypeStruct((D_MODEL, D_FF), BF16)] * 2 + [jax.ShapeDtypeStruct((D_FF, D_MODEL), BF16)]
    outs = pl.pallas_call(
        functools.partial(_ffn_body, final=final, emit=emit, tf=tf),
        grid=(m // tm, steps),
        in_specs=in_specs,
        out_specs=out_specs,
        out_shape=out_shape,
        scratch_shapes=[pltpu.VMEM((tm, D_MODEL), BF16), pltpu.VMEM((tm, D_MODEL), F32)],
        compiler_params=_cparams("parallel", "arbitrary"),
        name="ffn_cast" if emit else "ffn",
    )(*args)
    return (outs[0], (outs[1], outs[2]), outs[3]) if emit else outs[0]


def _inproj_body(x_ref, g_ref, sh_ref, sc_ref, w_ref, o_ref, h_scr):
    n = pl.program_id(1)

    @pl.when(n == 0)
    def _():
        h_scr[...] = _norm_mod(x_ref[...], g_ref[...], sh_ref[...], sc_ref[...]).astype(BF16)
        o_ref[...] = _dot(h_scr[...], w_ref[...])

    @pl.when(n > 0)
    def _():
        o_ref[...] = _dot(h_scr[...], w_ref[...])


def _inproj(x, mod, norm_g3, w_in, layer):
    m = x.shape[0]
    tm = min(TM_PROJ, m)
    return pl.pallas_call(
        _inproj_body,
        grid=(m // tm, N_PROJ // TN_PROJ),
        in_specs=[
            pl.BlockSpec((tm, D_MODEL), lambda i, n: (i, 0)),
            pl.BlockSpec((None, 1, D_MODEL), lambda i, n: (layer * 3 + 1, 0, 0)),
            mod.spec(layer, tm, 3), mod.spec(layer, tm, 4),
            pl.BlockSpec((None, D_MODEL, TN_PROJ), lambda i, n: (layer, 0, n)),
        ],
        out_specs=pl.BlockSpec((tm, TN_PROJ), lambda i, n: (i, n)),
        out_shape=jax.ShapeDtypeStruct((m, N_PROJ), F32),
        scratch_shapes=[pltpu.VMEM((tm, D_MODEL), BF16)],
        compiler_params=_cparams("parallel", "arbitrary"),
        name="inproj",
    )(x, norm_g3, mod.arr, mod.arr, w_in)


def _outproj_body(x_ref, oa_ref, ob_ref, gt_ref, wa_ref, wb_ref, o_ref):
    acc = _dot(oa_ref[...], wa_ref[...]) + _dot(ob_ref[...], wb_ref[...])
    o_ref[...] = x_ref[...] + gt_ref[...] * acc


def _outproj(x, oa, ob, mod, w_out, layer):
    m = x.shape[0]
    tm = min(TM_ROWS, m)
    return pl.pallas_call(
        _outproj_body,
        grid=(m // tm,),
        in_specs=[
            pl.BlockSpec((tm, D_MODEL), lambda i: (i, 0)),
            pl.BlockSpec((tm, W_A), lambda i: (i, 0)),
            pl.BlockSpec((tm, W_B), lambda i: (i, 0)),
            mod.spec(layer, tm, 5),
            pl.BlockSpec((None, W_A, D_MODEL), lambda i: (layer, 0, 0)),
            pl.BlockSpec((None, W_B, D_MODEL), lambda i: (layer, 1, 0)),
        ],
        out_specs=pl.BlockSpec((tm, D_MODEL), lambda i: (i, 0)),
        out_shape=jax.ShapeDtypeStruct((m, D_MODEL), F32),
        compiler_params=_cparams("parallel"),
        name="outproj",
    )(x, oa, ob, mod.arr, w_out, w_out)


def _lru_body(*refs, seq_len, has_state):
    if has_state:
        (x_ref, y_ref, pe_ref, h0_ref, cw_ref, cb_ref, wa_ref, wx_ref, ba_ref, bx_ref, lam_ref,
         ob_ref, hl_ref, a_scr, b_scr) = refs
    else:
        (x_ref, y_ref, cw_ref, cb_ref, wa_ref, wx_ref, ba_ref, bx_ref, lam_ref,
         ob_ref, hl_ref, a_scr, b_scr) = refs
    rows = x_ref.shape[0]
    if has_state:
        xc = _conv_rows(x_ref[...], pe_ref[...], cw_ref[...], _row_iota(x_ref.shape, seq_len))
    else:
        xc = _conv_zero_past(x_ref, cw_ref[...])
    xc = xc + cb_ref[...]
    xcb = xc.astype(BF16)
    r = jax.nn.sigmoid(_dot(xcb, wa_ref[...]) + ba_ref[...])
    i = jax.nn.sigmoid(_dot(xcb, wx_ref[...]) + bx_ref[...])
    log_a = -LRU_C * r * jax.nn.softplus(-lam_ref[...])
    a = jnp.exp(log_a)
    b = jnp.sqrt(_one_minus_exp2x(log_a)) * (i * xc)
    if has_state:
        b = b + a * h0_ref[...]
    group = min(seq_len, SUBLANES)
    tg = _row_iota(x_ref.shape, group)
    d = 1
    while d < group:
        keep = tg >= d
        b = jnp.where(keep, a * pltpu.roll(b, d, axis=0) + b, b)
        a = jnp.where(keep, a * pltpu.roll(a, d, axis=0), a)
        d *= 2
    if seq_len > SUBLANES:
        a_scr[...] = a
        b_scr[...] = b

        def step(k, carry):
            r0 = pl.multiple_of(k * SUBLANES, SUBLANES)
            hk = b_scr[pl.ds(r0, SUBLANES), :] + a_scr[pl.ds(r0, SUBLANES), :] * carry
            b_scr[pl.ds(r0, SUBLANES), :] = hk
            return hk[SUBLANES - 1:SUBLANES, :]

        lax.fori_loop(0, rows // SUBLANES, step, jnp.zeros((1, LRU_BW), F32), unroll=8)
        h = b_scr[...]
    else:
        h = b
    ob_ref[...] = (h * jax.nn.gelu(y_ref[...])).astype(BF16)
    hl_ref[...] = h[rows - hl_ref.shape[0]:, :]


def _lru(proj, seq_len, layer, lconv_w, lconv_b, wa, wx, ba, bx, lam, pe=None, h0e=None):
    m = proj.shape[0]
    has_state = pe is not None
    rows = seq_len if not has_state else m
    nseq = m // rows
    hl_rows = rows if has_state else SUBLANES
    bx0, by0 = COL_X // LRU_BW, COL_Y // LRU_BW
    col = lambda b, n: (b, n)
    vec = pl.BlockSpec((None, 1, LRU_BW), lambda b, n: (layer, 0, n))
    blk = pl.BlockSpec((None, None, LRU_BW, LRU_BW), lambda b, n: (layer, n, 0, 0))
    in_specs = [pl.BlockSpec((rows, LRU_BW), lambda b, n: (b, bx0 + n)),
                pl.BlockSpec((rows, LRU_BW), lambda b, n: (b, by0 + n))]
    args = [proj, proj]
    if has_state:
        in_specs += [pl.BlockSpec((rows, LRU_BW), col), pl.BlockSpec((rows, LRU_BW), col)]
        args += [pe, h0e]
    in_specs += [pl.BlockSpec((None, CONV_W, LRU_BW), lambda b, n: (layer, 0, n)), vec, blk, blk, vec, vec, vec]
    args += [lconv_w, lconv_b, wa, wx, ba, bx, lam]
    return pl.pallas_call(
        functools.partial(_lru_body, seq_len=seq_len, has_state=has_state),
        grid=(nseq, LRU_BLOCKS),
        in_specs=in_specs,
        out_specs=[pl.BlockSpec((rows, LRU_BW), col), pl.BlockSpec((hl_rows, LRU_BW), col)],
        out_shape=[jax.ShapeDtypeStruct((m, W_B), BF16), jax.ShapeDtypeStruct((nseq * hl_rows, W_B), F32)],
        scratch_shapes=[pltpu.VMEM((rows, LRU_BW), F32), pltpu.VMEM((rows, LRU_BW), F32)],
        compiler_params=_cparams("parallel", "parallel"),
        name="lru",
    )(*args)


def _qkv_prep(q_conv, k_conv, v_conv, heads):
    q, k, v = _silu(q_conv), _silu(k_conv), _silu(v_conv)
    qs, ks = [], []
    for h in range(heads):
        sl = slice(h * DK, (h + 1) * DK)
        qh, kh = q[:, sl], k[:, sl]
        qs.append(qh * (lax.rsqrt(jnp.sum(qh * qh, axis=-1, keepdims=True) + EPS) * DK ** -0.5))
        ks.append(kh * lax.rsqrt(jnp.sum(kh * kh, axis=-1, keepdims=True) + EPS))
    return jnp.concatenate(qs, axis=1), jnp.concatenate(ks, axis=1), v


def _gate_norm(o, onorm, gate_pre):
    var = jnp.mean(o * o, axis=-1, keepdims=True)
    return o * lax.rsqrt(var + EPS) * onorm * _silu(gate_pre)


def _seg_cumsum(g, tg, group):
    d = 1
    while d < group:
        g = jnp.where(tg >= d, g + pltpu.roll(g, d, axis=0), g)
        d *= 2
    return g


def _delta_prompt_body(q_ref, k_ref, v_ref, gt_ref, ab_ref, cwq_ref, cwk_ref, cwv_ref, alog_ref, dtb_ref,
                       on_ref, oa_ref, s_ref, qn_scr, kn_scr, vv_scr, gc_scr, bt_scr, s_scr,
                       wq_scr, uc_scr, qk_scr, kdt_scr, sd_scr):
    rows = q_ref.shape[0]
    hg = q_ref.shape[1] // DK
    qn, kn, vv = _qkv_prep(_conv_zero_past(q_ref, cwq_ref[...]), _conv_zero_past(k_ref, cwk_ref[...]),
                           _conv_zero_past(v_ref, cwv_ref[...]), hg)
    qn_scr[...] = qn
    kn_scr[...] = kn
    vv_scr[...] = vv
    ab = ab_ref[...]
    g = -jnp.exp(alog_ref[...]) * jax.nn.softplus(ab + dtb_ref[...])
    gc_scr[...] = _seg_cumsum(g, _row_iota(ab.shape, CHUNK), CHUNK)
    bt_scr[...] = jax.nn.sigmoid(ab)
    s_scr[...] = jnp.zeros_like(s_scr)
    head0 = pl.program_id(1) * hg
    onorm = on_ref[...]

    def factor(step, carry):
        lane = lax.broadcasted_iota(jnp.int32, (CHUNK, LANES), 1)
        ri = lax.broadcasted_iota(jnp.int32, (CHUNK, CHUNK), 0)
        ci = lax.broadcasted_iota(jnp.int32, (CHUNK, CHUNK), 1)
        causal = ri >= ci
        strict = ri > ci
        blk = jnp.bitwise_xor(ri, ci)
        chains = []
        for cc in range(FACTOR_CHUNKS):
            c = step * FACTOR_CHUNKS + cc
            rs = pl.ds(pl.multiple_of(c * CHUNK, CHUNK), CHUNK)
            gcc = gc_scr[rs, :]
            btc = bt_scr[rs, :]
            for hh in range(hg):
                sl = slice(hh * DK, (hh + 1) * DK)
                gcol = jnp.sum(jnp.where(lane == head0 + hh, gcc, 0.0), axis=1, keepdims=True)
                bcol = jnp.sum(jnp.where(lane == head0 + hh + H_A, btc, 0.0), axis=1, keepdims=True)
                gmat = jnp.broadcast_to(gcol, (CHUNK, CHUNK))
                decay = jnp.where(causal, jnp.exp(jnp.where(causal, gmat - gmat.T, 0.0)), 0.0)
                q = qn_scr[rs, sl]
                k = kn_scr[rs, sl]
                kq = _dot_nt(jnp.concatenate([k, q], axis=0).astype(BF16), k.astype(BF16))
                m = jnp.where(strict, bcol * kq[:CHUNK] * decay, 0.0)
                eg = jnp.exp(gcol)
                glast = jnp.broadcast_to(gcol[CHUNK - 1:CHUNK, :], (CHUNK, 1))
                wq_scr[hh, c, CHUNK:, :] = (q * eg).astype(BF16)
                qk_scr[hh, rs, :] = (kq[CHUNK:] * decay).astype(BF16)
                kdt_scr[hh, rs, :] = (k * jnp.exp(glast - gcol)).T.astype(BF16)
                sd_scr[hh, pl.ds(pl.multiple_of(c * SUBLANES, SUBLANES), SUBLANES), :] = jnp.broadcast_to(
                    jnp.broadcast_to(eg, (CHUNK, DV))[CHUNK - 1:CHUNK, :], (SUBLANES, DV))
                chains.append((hh, c, rs, sl, m, bcol, eg))
        es = [-jnp.where(lax.shift_right_logical(blk, 1) == 0, ch[4], 0.0) for ch in chains]
        lg = 1
        while (1 << lg) < CHUNK:
            cross = lax.shift_right_logical(blk, lg) == 1
            offs = [jnp.where(cross, ch[4], 0.0) for ch in chains]
            ebs = [e.astype(BF16) for e in es]
            xs = [off + _dot(eb, off.astype(BF16)) for off, eb in zip(offs, ebs)]
            es = [e - (x + _dot(x.astype(BF16), eb)) for e, x, eb in zip(es, xs, ebs)]
            lg += 1
        for (hh, c, rs, sl, m, bcol, eg), e in zip(chains, es):
            rhs = jnp.concatenate([kn_scr[rs, sl] * (bcol * eg), vv_scr[rs, sl] * bcol], axis=1)
            sol = rhs + _dot(e.astype(BF16), rhs.astype(BF16))
            wq_scr[hh, c, :CHUNK, :] = sol[:, :DK].astype(BF16)
            uc_scr[hh, rs, :] = sol[:, DK:]
        return carry

    lax.fori_loop(0, rows // (CHUNK * FACTOR_CHUNKS), factor, 0)

    def recur(c, carry):
        rs = pl.ds(pl.multiple_of(c * CHUNK, CHUNK), CHUNK)
        s_old = [s_scr[hh] for hh in range(hg)]
        ps = [_dot(wq_scr[hh, c], s_old[hh].astype(BF16)) for hh in range(hg)]
        us = [(uc_scr[hh, rs, :] - ps[hh][:CHUNK]).astype(BF16) for hh in range(hg)]
        os_ = [ps[hh][CHUNK:] + _dot(qk_scr[hh, rs, :], us[hh]) for hh in range(hg)]
        for hh in range(hg):
            sdec = sd_scr[hh, pl.ds(pl.multiple_of(c * SUBLANES, SUBLANES), SUBLANES), :]
            s_scr[hh] = s_old[hh] * sdec[:1, :] + _dot(kdt_scr[hh, rs, :], us[hh])
        for hh in range(hg):
            sl = slice(hh * DK, (hh + 1) * DK)
            oa_ref[rs, sl] = _gate_norm(os_[hh], onorm, gt_ref[rs, sl]).astype(BF16)
        return carry

    lax.fori_loop(0, rows // CHUNK, recur, 0, unroll=4)
    s_ref[...] = s_scr[...]


def _delta_prompt(proj, batch, seq_len, layer, dconv_w, alog_row, dtb_row, onorm):
    m = proj.shape[0]
    hg = HG_PROMPT
    wcols = hg * DK
    nchunks = seq_len // CHUNK
    assert nchunks % FACTOR_CHUNKS == 0
    nq, nk, nv, ng = (c // wcols for c in (COL_Q, COL_K, COL_V, COL_G))
    colspec = lambda off: pl.BlockSpec((seq_len, wcols), lambda b, j: (b, off + j))
    cwspec = lambda off: pl.BlockSpec((None, CONV_W, wcols), lambda b, j: (layer, 0, off + j))
    row = pl.BlockSpec((None, 1, LANES), lambda b, j: (layer, 0, 0))
    return pl.pallas_call(
        _delta_prompt_body,
        grid=(batch, H_A // hg),
        in_specs=[colspec(nq), colspec(nk), colspec(nv), colspec(ng),
                  pl.BlockSpec((seq_len, LANES), lambda b, j: (b, COL_AB // LANES)),
                  cwspec(nq), cwspec(nk), cwspec(nv), row, row, row],
        out_specs=[pl.BlockSpec((seq_len, wcols), lambda b, j: (b, j)),
                   pl.BlockSpec((None, hg, DK, DV), lambda b, j: (b, j, 0, 0))],
        out_shape=[jax.ShapeDtypeStruct((m, W_A), BF16), jax.ShapeDtypeStruct((batch, H_A, DK, DV), F32)],
        scratch_shapes=[pltpu.VMEM((seq_len, wcols), F32)] * 3
        + [pltpu.VMEM((seq_len, LANES), F32)] * 2 + [pltpu.VMEM((hg, DK, DV), F32)]
        + [pltpu.VMEM((hg, nchunks, 2 * CHUNK, DK), BF16), pltpu.VMEM((hg, seq_len, DV), F32),
           pltpu.VMEM((hg, seq_len, CHUNK), BF16), pltpu.VMEM((hg, seq_len, CHUNK), BF16),
           pltpu.VMEM((hg, nchunks * SUBLANES, DV), F32)],
        compiler_params=_cparams("parallel", "parallel"),
        name="delta_prompt",
    )(proj, proj, proj, proj, proj, dconv_w, dconv_w, dconv_w, alog_row, dtb_row, onorm)


def _delta_sample_body(q_ref, k_ref, v_ref, gt_ref, ab_ref, pe_ref, cw_ref, alog_ref, dtb_ref, on_ref, s0_ref,
                       *rest, seq_len):
    oa_ref, s_ref, qn_scr, kn_scr, vv_scr, gc_scr, bt_scr = rest[-7:]
    rows = q_ref.shape[0]
    t = _row_iota(q_ref.shape, seq_len)
    pe = pe_ref[...]
    cw = cw_ref[...]
    qn, kn, vv = _qkv_prep(_conv_rows(q_ref[...], pe[:, COL_Q:COL_K], cw[:, COL_Q:COL_K], t),
                           _conv_rows(k_ref[...], pe[:, COL_K:COL_V], cw[:, COL_K:COL_V], t),
                           _conv_rows(v_ref[...], pe[:, COL_V:COL_G], cw[:, COL_V:COL_G], t), H_A)
    qn_scr[...] = qn
    kn_scr[...] = kn
    vv_scr[...] = vv
    ab = ab_ref[...]
    g = -jnp.exp(alog_ref[...]) * jax.nn.softplus(ab + dtb_ref[...])
    gc_scr[...] = _seg_cumsum(g, _row_iota(ab.shape, seq_len), seq_len)
    bt_scr[...] = jax.nn.sigmoid(ab)
    onorm = on_ref[...]
    per_tile = SUBLANES // seq_len

    def tile(p, carry):
        tt = _row_iota((SUBLANES, LANES), seq_len)
        tt2 = _row_iota((SUBLANES, DK + DV), seq_len)
        ri = lax.broadcasted_iota(jnp.int32, (SUBLANES, LANES), 0)
        ri2 = _row_iota((2 * SUBLANES, LANES), SUBLANES)
        lane = lax.broadcasted_iota(jnp.int32, (LANES, LANES), 1)
        zpad = jnp.zeros((LANES - SUBLANES, DK), F32)
        rs = pl.ds(pl.multiple_of(p * SUBLANES, SUBLANES), SUBLANES)
        gcc, btc = gc_scr[rs, :], bt_scr[rs, :]
        work = [(p, rs, h, gcc, btc) for h in range(H_A)]
        staged = []
        for p, rs, h, gcc, btc in work:
            sl = slice(h * DK, (h + 1) * DK)
            gc = jnp.broadcast_to(gcc[:, h:h + 1], (SUBLANES, LANES))
            beta = jnp.broadcast_to(btc[:, H_A + h:H_A + h + 1], (SUBLANES, LANES))
            q = qn_scr[rs, sl]
            k = kn_scr[rs, sl]
            v = vv_scr[rs, sl]
            eg = jnp.exp(gc)
            decs, ms = [None], [None]
            for d in range(1, seq_len):
                ok = tt >= d
                dec = jnp.where(ok, jnp.exp(jnp.where(ok, gc - pltpu.roll(gc, d, axis=0), 0.0)), 0.0)
                kk = jnp.sum(k * pltpu.roll(k, d, axis=0), axis=-1, keepdims=True)
                decs.append(dec)
                md = beta * kk * dec
                ms.append(jnp.concatenate([md, md], axis=1))
            rhs = jnp.concatenate([k * (beta * eg), v * beta], axis=1)
            sol = rhs
            for step in range(1, seq_len):
                acc = ms[1] * pltpu.roll(sol, 1, axis=0)
                for d in range(2, seq_len):
                    acc = acc + ms[d] * pltpu.roll(sol, d, axis=0)
                sol = jnp.where(tt2 == step, rhs - acc, sol)
            lhs = jnp.concatenate([sol[:, :DK], q * eg], axis=0).astype(BF16)
            staged.append((sl, gc, q, k, eg, decs, sol[:, DK:], lhs))
        pps = []
        for (p, rs, h, _, _), st in zip(work, staged):
            lhs = st[-1]
            pp = _dot(lhs, s0_ref[p * per_tile, h].astype(BF16))
            for bb in range(1, per_tile):
                pp = jnp.where(ri2 >= bb * seq_len, _dot(lhs, s0_ref[p * per_tile + bb, h].astype(BF16)), pp)
            pps.append(pp)
        updates = []
        for (p, rs, h, _, _), st, pp in zip(work, staged, pps):
            sl, gc, q, k, eg, decs, uc, _ = st
            u = uc - pp[:SUBLANES]
            o = pp[SUBLANES:] + jnp.sum(q * k, axis=-1, keepdims=True) * u
            for d in range(1, seq_len):
                qk = jnp.sum(q * pltpu.roll(k, d, axis=0), axis=-1, keepdims=True)
                o = o + (qk * decs[d]) * pltpu.roll(u, d, axis=0)
            oa_ref[rs, sl] = _gate_norm(o, onorm, gt_ref[rs, sl]).astype(BF16)
            glast = jnp.broadcast_to(gc[seq_len - 1:seq_len, :], (SUBLANES, LANES))
            for bb in range(1, per_tile):
                last = bb * seq_len + seq_len - 1
                glast = jnp.where(ri >= bb * seq_len, jnp.broadcast_to(gc[last:last + 1, :], (SUBLANES, LANES)), glast)
            kd = k * jnp.exp(glast - gc)
            kdt = jnp.concatenate([kd, zpad], axis=0).T
            updates.append((kdt, jnp.concatenate([u, zpad], axis=0).astype(BF16)))
        for (p, rs, h, _, _), st, (kdt, upad) in zip(work, staged, updates):
            eg = st[4]
            for bb in range(per_tile):
                last = bb * seq_len + seq_len - 1
                cols = (lane >= bb * seq_len) & (lane < (bb + 1) * seq_len)
                kd_b = jnp.where(cols, kdt, 0.0).astype(BF16)
                s_ref[p * per_tile + bb, h] = (s0_ref[p * per_tile + bb, h] * eg[last:last + 1, :]
                                               + _dot(kd_b, upad))
        return carry

    lax.fori_loop(0, rows // SUBLANES, tile, 0)


def _delta_sample(proj, seq_len, layer, pe, dconv_w, alog_row, dtb_row, onorm, s0, s_stack):
    m = proj.shape[0]
    tb = TB_SAMPLE
    rows = tb * seq_len
    colspec = lambda off: pl.BlockSpec((rows, W_A), lambda i: (i, off // W_A))
    row = pl.BlockSpec((None, 1, LANES), lambda i: (layer, 0, 0))
    sspec = pl.BlockSpec((None, tb, H_A, DK, DV), lambda i: (layer, i, 0, 0, 0))
    in_specs = [colspec(COL_Q), colspec(COL_K), colspec(COL_V), colspec(COL_G),
                pl.BlockSpec((rows, LANES), lambda i: (i, COL_AB // LANES)),
                pl.BlockSpec((rows, N_DCONV), lambda i: (i, 0)),
                pl.BlockSpec((None, CONV_W, N_DCONV), lambda i: (layer, 0, 0)),
                row, row, row, sspec]
    args = [proj, proj, proj, proj, proj, pe, dconv_w, alog_row, dtb_row, onorm, s0]
    aliases = {}
    if s_stack is not None:
        in_specs.append(pl.BlockSpec(memory_space=pl.ANY))
        aliases = {len(args): 1}
        args.append(s_stack)
    return pl.pallas_call(
        functools.partial(_delta_sample_body, seq_len=seq_len),
        grid=(m // rows,),
        in_specs=in_specs,
        out_specs=[pl.BlockSpec((rows, W_A), lambda i: (i, 0)), sspec],
        out_shape=[jax.ShapeDtypeStruct((m, W_A), BF16), jax.ShapeDtypeStruct(s0.shape, F32)],
        input_output_aliases=aliases,
        scratch_shapes=[pltpu.VMEM((rows, W_A), F32)] * 3 + [pltpu.VMEM((rows, LANES), F32)] * 2,
        compiler_params=_cparams("parallel"),
        name="delta_sample",
    )(*args)


def _pad_rows(state, seq_len):
    b, r, c = state.shape
    return jnp.pad(state, ((0, 0), (0, seq_len - r), (0, 0))).reshape(b * seq_len, c)


def _trunk(x, mod, batch, seq_len, wts, states, ffn_cast):
    (w_in, w_out, norm_g3, ffn_f32, dconv_w, alog_row, dtb_row, onorm, lconv_w, lconv_b,
     wa, wx, ba, bx, lam, final_g) = wts
    new_d, new_dc, new_l, new_lc = [], [], [], []
    s_stack = None

    def ffn(x, layer, s, fg=None):
        if (layer, s) in ffn_cast:
            return _ffn(x, mod, norm_g3, *ffn_cast[(layer, s)], layer, s, fg)
        y, w13, w2 = _ffn(x, mod, norm_g3, ffn_f32[:2], ffn_f32[2], layer, s, fg, emit=True)
        ffn_cast[(layer, s)] = (w13, w2)
        return y

    for layer in range(DEPTH):
        x = ffn(x, layer, 0)
        proj = _inproj(x, mod, norm_g3, w_in, layer)
        if states is None:
            oa, sd = _delta_prompt(proj, batch, seq_len, layer, dconv_w, alog_row, dtb_row, onorm)
            new_d.append(sd)
            ob, hl = _lru(proj, seq_len, layer, lconv_w, lconv_b, wa, wx, ba, bx, lam)
            sl = hl.reshape(batch, SUBLANES, W_B)[:, SUBLANES - 1]
        else:
            s_delta, s_dconv, s_lru, s_lconv = states
            oa, s_stack = _delta_sample(proj, seq_len, layer, _pad_rows(s_dconv[layer], seq_len),
                                        dconv_w, alog_row, dtb_row, onorm, s_delta, s_stack)
            ob, hl = _lru(proj, seq_len, layer, lconv_w, lconv_b, wa, wx, ba, bx, lam,
                          pe=_pad_rows(s_lconv[layer], seq_len),
                          h0e=_pad_rows(s_lru[layer][:, None, :], seq_len))
            sl = hl.reshape(batch, seq_len, W_B)[:, seq_len - 1]
        x = _outproj(x, oa, ob, mod, w_out, layer)
        x = ffn(x, layer, 1, final_g if layer == DEPTH - 1 else None)
        first_row = seq_len - (CONV_W - 1)
        if seq_len % SUBLANES == 0:
            ends = proj.reshape(batch, seq_len, N_PROJ)[:, first_row:]
            last_rows = lambda lo, hi: ends[:, :, lo:hi]
        else:
            last_rows = lambda lo, hi: proj[:, lo:hi].reshape(batch, seq_len, hi - lo)[:, first_row:]
        new_dc.append(last_rows(COL_Q, COL_G))
        new_l.append(sl)
        new_lc.append(last_rows(COL_X, COL_Y))
    new_delta = jnp.stack(new_d) if states is None else s_stack
    return x, new_delta, jnp.stack(new_dc), jnp.stack(new_l), jnp.stack(new_lc)


def kernel(x_prompt, x_sample, c_prompt, c_sample, state_delta, state_delta_conv, state_lru, state_lru_conv, w_in, w_out, norm_g, w_ada, b_ada, ffn_w1, ffn_w3, ffn_w2, dconv_w, d_alog, d_dtbias, d_onorm, lconv_w, lconv_b, lru_wa, lru_ba, lru_wx, lru_bx, lru_lam, final_g):
    bp, lp, _ = x_prompt.shape
    bs, ls, _ = x_sample.shape
    assert lp % CHUNK == 0 and SUBLANES % ls == 0 and bp <= SUBLANES

    w_in_r = jnp.zeros((DEPTH, D_MODEL, N_PROJ), BF16)
    for dst, lo, hi in ((COL_Q, 0, OFF_A), (COL_X, OFF_X, N_IN), (COL_AB, OFF_A, OFF_X)):
        w_in_r = lax.dynamic_update_slice(w_in_r, w_in[..., lo:hi].astype(BF16), (0, 0, dst))
    w_out_b = w_out.astype(BF16)
    wa = lru_wa.astype(BF16)
    wx = lru_wx.astype(BF16)
    norm_g3 = norm_g.reshape(DEPTH * 3, 1, D_MODEL)
    lane_row = lambda v: jnp.pad(v, ((0, 0), (0, LANES - v.shape[1]))).reshape(DEPTH, 1, LANES)
    vec = lambda v: v.reshape(DEPTH, 1, W_B)
    wts = (w_in_r, w_out_b, norm_g3, (ffn_w1, ffn_w3, ffn_w2), dconv_w, lane_row(d_alog), lane_row(d_dtbias),
           d_onorm.reshape(DEPTH, 1, DV), lconv_w, vec(lconv_b), wa, wx, vec(lru_ba), vec(lru_bx),
           vec(lru_lam), final_g)

    ms = bs * ls
    c_all = jnp.concatenate([c_sample, c_prompt, jnp.zeros((SUBLANES - bp, D_MODEL), F32)], axis=0)
    mod_all = _ada(c_all, w_ada, b_ada, bs, ls)
    mod_p = _Mod(mod_all[:, ms:ms + bp].reshape(DEPTH, bp, 1, 9 * D_MODEL), False, lp)
    mod_s = _Mod(mod_all, True, ls)

    ffn_cast = {}
    ys, sd, sdc, sl_, slc = _trunk(x_sample.reshape(bs * ls, D_MODEL), mod_s, bs, ls, wts,
                                   (state_delta, state_delta_conv, state_lru, state_lru_conv), ffn_cast)
    yp, pd, pdc, pl_, plc = _trunk(x_prompt.reshape(bp * lp, D_MODEL), mod_p, bp, lp, wts, None, ffn_cast)
    return (yp.reshape(bp, lp, D_MODEL), ys.reshape(bs, ls, D_MODEL), pd, pdc, pl_, plc, sd, sdc, sl_, slc)
```

```python
import functools

import jax
import jax.numpy as jnp
from jax import lax
from jax.experimental import pallas as pl
from jax.experimental.pallas import tpu as pltpu

F32 = jnp.float32
BF16 = jnp.bfloat16

D_MODEL = 2048
DEPTH = 4
H_A = 8
DK = 128
DV = 128
W_QK = H_A * DK
W_A = H_A * DV
W_B = D_MODEL - W_A
LRU_BLOCKS = 8
LRU_BW = W_B // LRU_BLOCKS
LRU_C = 8.0
CONV_W = 4
D_FF = ((8 * D_MODEL // 3 + 127) // 128) * 128
HALF = 0.5
EPS = 1e-6
N_DCONV = 2 * W_QK + W_A
OFF_G = 2 * W_QK + W_A
OFF_A = OFF_G + W_A
OFF_X = OFF_A + 2 * H_A
N_IN = OFF_X + 2 * W_B

LANES = 128
SUBLANES = 8
VMEM_LIMIT = 56 * 1024 * 1024

COL_Q, COL_K, COL_V, COL_G = 0, W_QK, 2 * W_QK, 2 * W_QK + W_A
COL_X = COL_G + W_A
COL_Y = COL_X + W_B
COL_AB = COL_Y + W_B
N_PROJ = COL_AB + 2 * LANES

TM_ROWS = 512
TM_PROJ = 1024
TF_FFN = 512
TF_EMIT = 256
TN_PROJ = 1280
TN_ADA = 1024
CHUNK = 128
HG_PROMPT = 2
FACTOR_CHUNKS = 16
TB_SAMPLE = 16

assert N_PROJ % TN_PROJ == 0 and (9 * D_MODEL) % TN_ADA == 0 and D_FF % LANES == 0


def _cparams(*sem):
    return pltpu.CompilerParams(dimension_semantics=sem, vmem_limit_bytes=VMEM_LIMIT)


def _dot(a, b):
    return jnp.dot(a, b, preferred_element_type=F32)


def _dot_nt(a, b):
    return lax.dot_general(a, b, (((1,), (1,)), ((), ())), preferred_element_type=F32)


def _silu(x):
    return x * jax.nn.sigmoid(x)


def _one_minus_exp2x(x):
    t = jnp.tanh(x)
    return -2.0 * t / (1.0 - t)


def _norm_mod(x, g, shift, scale):
    var = jnp.mean(x * x, axis=-1, keepdims=True)
    return x * lax.rsqrt(var + EPS) * (g * (1.0 + scale)) + shift


def _row_iota(shape, mod):
    return jnp.bitwise_and(lax.broadcasted_iota(jnp.int32, shape, 0), mod - 1)


def _conv_zero_past(x_ref, w):
    rows = x_ref.shape[0]
    head = x_ref[:SUBLANES, :]
    head_y = _conv_rows(head, None, w, lax.broadcasted_iota(jnp.int32, head.shape, 0))
    y = x_ref[SUBLANES:, :] * w[CONV_W - 1:CONV_W, :]
    for j in range(1, CONV_W):
        y = y + x_ref[SUBLANES - j:rows - j, :] * w[CONV_W - 1 - j:CONV_W - j, :]
    return jnp.concatenate([head_y, y], axis=0)


def _conv_rows(x, pe, w, t):
    rows = x.shape[0]
    y = x * w[CONV_W - 1:CONV_W, :]
    for j in range(1, CONV_W):
        term = pltpu.roll(x, j, axis=0)
        if pe is None:
            term = jnp.where(t >= j, term, 0.0)
        else:
            back = CONV_W - 1 - j
            prev = pe if back == 0 else pltpu.roll(pe, rows - back, axis=0)
            term = jnp.where(t >= j, term, prev)
        y = y + term * w[CONV_W - 1 - j:CONV_W - j, :]
    return y


def _ada_body(c_ref, w_ref, b_ref, o_ref, *, nseq, reps):
    cs = _silu(c_ref[...]).astype(BF16)
    m = _dot(cs, w_ref[...].astype(BF16)) + b_ref[...]
    hi = m[:nseq].astype(BF16)
    r1 = m[:nseq] - hi.astype(F32)
    mid = r1.astype(BF16)
    lo = (r1 - mid.astype(F32)).astype(BF16)
    rows = nseq * reps
    src = lax.broadcasted_iota(jnp.int32, (rows, 3 * nseq), 1)
    dst = lax.div(lax.broadcasted_iota(jnp.int32, (rows, 3 * nseq), 0), reps)
    hit = (src == dst) | (src == dst + nseq) | (src == dst + 2 * nseq)
    pick = jnp.where(hit, 1.0, 0.0).astype(BF16)
    o_ref[:rows, :] = _dot(pick, jnp.concatenate([hi, mid, lo], axis=0))
    o_ref[rows:, :] = m[nseq:]


def _ada(c_all, w_ada, b_ada, nseq, reps):
    rows = c_all.shape[0]
    out_rows = nseq * reps + rows - nseq
    n9 = 9 * D_MODEL
    return pl.pallas_call(
        functools.partial(_ada_body, nseq=nseq, reps=reps),
        grid=(DEPTH, n9 // TN_ADA),
        in_specs=[
            pl.BlockSpec((rows, D_MODEL), lambda l, n: (0, 0)),
            pl.BlockSpec((None, D_MODEL, TN_ADA), lambda l, n: (l, 0, n)),
            pl.BlockSpec((None, 1, TN_ADA), lambda l, n: (l, 0, n)),
        ],
        out_specs=pl.BlockSpec((None, out_rows, TN_ADA), lambda l, n: (l, 0, n)),
        out_shape=jax.ShapeDtypeStruct((DEPTH, out_rows, n9), F32),
        compiler_params=_cparams("parallel", "parallel"),
        name="ada",
    )(c_all, w_ada, b_ada.reshape(DEPTH, 1, n9))


class _Mod:
    def __init__(self, arr, per_token, seq_len):
        self.arr = arr
        self.per_token = per_token
        self.seq_len = seq_len

    def spec(self, layer, tm, k, **kw):
        if self.per_token:
            return pl.BlockSpec((None, tm, D_MODEL), lambda i, *_: (layer, i, k), **kw)
        seq_len = self.seq_len
        return pl.BlockSpec((None, None, 1, D_MODEL), lambda i, *_: (layer, (i * tm) // seq_len, 0, k), **kw)


def _ffn_body(x_ref, g_ref, sh_ref, sc_ref, gt_ref, w1_ref, w3_ref, w2_ref, *rest, final, emit, tf):
    rest = list(rest)
    fg_ref = rest.pop(0) if final else None
    o_ref = rest.pop(0)
    if emit:
        w1b_ref, w3b_ref, w2b_ref = rest[:3]
        rest = rest[3:]
        w1b_ref[...] = w1_ref[...].astype(BF16)
        w3b_ref[...] = w3_ref[...].astype(BF16)
        w2b_ref[...] = w2_ref[...].astype(BF16)
        w1_ref, w3_ref, w2_ref = w1b_ref, w3b_ref, w2b_ref
    f = pl.program_id(1)
    last = pl.num_programs(1) - 1
    steps = -(-D_FF // tf)
    tail = D_FF - (steps - 1) * tf

    def epilogue(acc):
        y = x_ref[...] + HALF * gt_ref[...] * acc
        if final:
            var = jnp.mean(y * y, axis=-1, keepdims=True)
            y = y * lax.rsqrt(var + EPS) * fg_ref[...]
        o_ref[...] = y

    h_scr, acc_scr = rest

    def swiglu(width):
        h = h_scr[...]
        a = _dot(h, w1_ref[:, :width])
        b = _dot(h, w3_ref[:, :width])
        return _dot((_silu(a) * b).astype(BF16), w2_ref[:width, :])

    @pl.when(f == 0)
    def _():
        h_scr[...] = _norm_mod(x_ref[...], g_ref[...], sh_ref[...], sc_ref[...]).astype(BF16)
        acc_scr[...] = swiglu(tf)

    @pl.when((f > 0) & (f < last))
    def _():
        acc_scr[...] += swiglu(tf)

    @pl.when(f == last)
    def _():
        epilogue(acc_scr[...] + swiglu(tail))


def _ffn(x, mod, norm_g3, w13, w2, layer, s, final_g=None, emit=False):
    m = x.shape[0]
    tm = min(TM_ROWS, m)
    tf = TF_EMIT if emit else TF_FFN
    steps = -(-D_FF // tf)
    assert steps >= 2
    sub = 2 * s
    final = final_g is not None
    w1, w3 = w13
    if emit:
        assert m == tm
        once = dict(pipeline_mode=pl.Buffered(1))
        up = pl.BlockSpec((None, None, D_MODEL, tf), lambda i, f: (layer, s, 0, f))
        down = pl.BlockSpec((None, None, tf, D_MODEL), lambda i, f: (layer, s, f, 0))
    else:
        once = {}
        up = pl.BlockSpec((D_MODEL, tf), lambda i, f: (0, f))
        down = pl.BlockSpec((tf, D_MODEL), lambda i, f: (f, 0))
    in_specs = [
        pl.BlockSpec((tm, D_MODEL), lambda i, f: (i, 0), **once),
        pl.BlockSpec((None, 1, D_MODEL), lambda i, f: (layer * 3 + sub, 0, 0)),
        mod.spec(layer, tm, 3 * sub, **once), mod.spec(layer, tm, 3 * sub + 1, **once),
        mod.spec(layer, tm, 3 * sub + 2, **once),
        up, up, down,
    ]
    args = [x, norm_g3, mod.arr, mod.arr, mod.arr, w1, w3, w2]
    if final:
        in_specs.append(pl.BlockSpec((1, D_MODEL), lambda i, f: (0, 0)))
        args.append(final_g.reshape(1, D_MODEL))
    out_specs = [pl.BlockSpec((tm, D_MODEL), lambda i, f: (i, 0))]
    out_shape = [jax.ShapeDtypeStruct((m, D_MODEL), F32)]
    if emit:
        out_specs += [pl.BlockSpec((D_MODEL, tf), lambda i, f: (0, f))] * 2 + [pl.BlockSpec((tf, D_MODEL), lambda i, f: (f, 0))]
        out_shape += [jax.ShapeDtypeStruct((D_MODEL, D_FF), BF16)] * 2 + [jax.ShapeDtypeStruct((D_FF, D_MODEL), BF16)]
    outs = pl.pallas_call(
        functools.partial(_ffn_body, final=final, emit=emit, tf=tf),
        grid=(m // tm, steps),
        in_specs=in_specs,
        out_specs=out_specs,
        out_shape=out_shape,
        scratch_shapes=[pltpu.VMEM((tm, D_MODEL), BF16), pltpu.VMEM((tm, D_MODEL), F32)],
        compiler_params=_cparams("parallel", "arbitrary"),
        name="ffn_cast" if emit else "ffn",
    )(*args)
    return (outs[0], (outs[1], outs[2]), outs[3]) if emit else outs[0]


def _inproj_body(x_ref, g_ref, sh_ref, sc_ref, w_ref, o_ref, h_scr):
    n = pl.program_id(1)

    @pl.when(n == 0)
    def _():
        h_scr[...] = _norm_mod(x_ref[...], g_ref[...], sh_ref[...], sc_ref[...]).astype(BF16)
        o_ref[...] = _dot(h_scr[...], w_ref[...])

    @pl.when(n > 0)
    def _():
        o_ref[...] = _dot(h_scr[...], w_ref[...])


def _inproj(x, mod, norm_g3, w_in, layer):
    m = x.shape[0]
    tm = min(TM_PROJ, m)
    return pl.pallas_call(
        _inproj_body,
        grid=(m // tm, N_PROJ // TN_PROJ),
        in_specs=[
            pl.BlockSpec((tm, D_MODEL), lambda i, n: (i, 0)),
            pl.BlockSpec((None, 1, D_MODEL), lambda i, n: (layer * 3 + 1, 0, 0)),
            mod.spec(layer, tm, 3), mod.spec(layer, tm, 4),
            pl.BlockSpec((None, D_MODEL, TN_PROJ), lambda i, n: (layer, 0, n)),
        ],
        out_specs=pl.BlockSpec((tm, TN_PROJ), lambda i, n: (i, n)),
        out_shape=jax.ShapeDtypeStruct((m, N_PROJ), F32),
        scratch_shapes=[pltpu.VMEM((tm, D_MODEL), BF16)],
        compiler_params=_cparams("parallel", "arbitrary"),
        name="inproj",
    )(x, norm_g3, mod.arr, mod.arr, w_in)


def _outproj_body(x_ref, oa_ref, ob_ref, gt_ref, wa_ref, wb_ref, o_ref):
    acc = _dot(oa_ref[...], wa_ref[...]) + _dot(ob_ref[...], wb_ref[...])
    o_ref[...] = x_ref[...] + gt_ref[...] * acc


def _outproj(x, oa, ob, mod, w_out, layer):
    m = x.shape[0]
    tm = min(TM_ROWS, m)
    return pl.pallas_call(
        _outproj_body,
        grid=(m // tm,),
        in_specs=[
            pl.BlockSpec((tm, D_MODEL), lambda i: (i, 0)),
            pl.BlockSpec((tm, W_A), lambda i: (i, 0)),
            pl.BlockSpec((tm, W_B), lambda i: (i, 0)),
            mod.spec(layer, tm, 5),
            pl.BlockSpec((None, W_A, D_MODEL), lambda i: (layer, 0, 0)),
            pl.BlockSpec((None, W_B, D_MODEL), lambda i: (layer, 1, 0)),
        ],
        out_specs=pl.BlockSpec((tm, D_MODEL), lambda i: (i, 0)),
        out_shape=jax.ShapeDtypeStruct((m, D_MODEL), F32),
        compiler_params=_cparams("parallel"),
        name="outproj",
    )(x, oa, ob, mod.arr, w_out, w_out)


def _lru_body(*refs, seq_len, has_state):
    if has_state:
        (x_ref, y_ref, pe_ref, h0_ref, cw_ref, cb_ref, wa_ref, wx_ref, ba_ref, bx_ref, lam_ref,
         ob_ref, hl_ref, a_scr, b_scr) = refs
    else:
        (x_ref, y_ref, cw_ref, cb_ref, wa_ref, wx_ref, ba_ref, bx_ref, lam_ref,
         ob_ref, hl_ref, a_scr, b_scr) = refs
    rows = x_ref.shape[0]
    if has_state:
        xc = _conv_rows(x_ref[...], pe_ref[...], cw_ref[...], _row_iota(x_ref.shape, seq_len))
    else:
        xc = _conv_zero_past(x_ref, cw_ref[...])
    xc = xc + cb_ref[...]
    xcb = xc.astype(BF16)
    r = jax.nn.sigmoid(_dot(xcb, wa_ref[...]) + ba_ref[...])
    i = jax.nn.sigmoid(_dot(xcb, wx_ref[...]) + bx_ref[...])
    log_a = -LRU_C * r * jax.nn.softplus(-lam_ref[...])
    a = jnp.exp(log_a)
    b = jnp.sqrt(_one_minus_exp2x(log_a)) * (i * xc)
    if has_state:
        b = b + a * h0_ref[...]
    group = min(seq_len, SUBLANES)
    tg = _row_iota(x_ref.shape, group)
    d = 1
    while d < group:
        keep = tg >= d
        b = jnp.where(keep, a * pltpu.roll(b, d, axis=0) + b, b)
        a = jnp.where(keep, a * pltpu.roll(a, d, axis=0), a)
        d *= 2
    if seq_len > SUBLANES:
        a_scr[...] = a
        b_scr[...] = b

        def step(k, carry):
            r0 = pl.multiple_of(k * SUBLANES, SUBLANES)
            hk = b_scr[pl.ds(r0, SUBLANES), :] + a_scr[pl.ds(r0, SUBLANES), :] * carry
            b_scr[pl.ds(r0, SUBLANES), :] = hk
            return hk[SUBLANES - 1:SUBLANES, :]

        lax.fori_loop(0, rows // SUBLANES, step, jnp.zeros((1, LRU_BW), F32), unroll=8)
        h = b_scr[...]
    else:
        h = b
    ob_ref[...] = (h * jax.nn.gelu(y_ref[...])).astype(BF16)
    hl_ref[...] = h[rows - hl_ref.shape[0]:, :]


def _lru(proj, seq_len, layer, lconv_w, lconv_b, wa, wx, ba, bx, lam, pe=None, h0e=None):
    m = proj.shape[0]
    has_state = pe is not None
    rows = seq_len if not has_state else m
    nseq = m // rows
    hl_rows = rows if has_state else SUBLANES
    bx0, by0 = COL_X // LRU_BW, COL_Y // LRU_BW
    col = lambda b, n: (b, n)
    vec = pl.BlockSpec((None, 1, LRU_BW), lambda b, n: (layer, 0, n))
    blk = pl.BlockSpec((None, None, LRU_BW, LRU_BW), lambda b, n: (layer, n, 0, 0))
    in_specs = [pl.BlockSpec((rows, LRU_BW), lambda b, n: (b, bx0 + n)),
                pl.BlockSpec((rows, LRU_BW), lambda b, n: (b, by0 + n))]
    args = [proj, proj]
    if has_state:
        in_specs += [pl.BlockSpec((rows, LRU_BW), col), pl.BlockSpec((rows, LRU_BW), col)]
        args += [pe, h0e]
    in_specs += [pl.BlockSpec((None, CONV_W, LRU_BW), lambda b, n: (layer, 0, n)), vec, blk, blk, vec, vec, vec]
    args += [lconv_w, lconv_b, wa, wx, ba, bx, lam]
    return pl.pallas_call(
        functools.partial(_lru_body, seq_len=seq_len, has_state=has_state),
        grid=(nseq, LRU_BLOCKS),
        in_specs=in_specs,
        out_specs=[pl.BlockSpec((rows, LRU_BW), col), pl.BlockSpec((hl_rows, LRU_BW), col)],
        out_shape=[jax.ShapeDtypeStruct((m, W_B), BF16), jax.ShapeDtypeStruct((nseq * hl_rows, W_B), F32)],
        scratch_shapes=[pltpu.VMEM((rows, LRU_BW), F32), pltpu.VMEM((rows, LRU_BW), F32)],
        compiler_params=_cparams("parallel", "parallel"),
        name="lru",
    )(*args)


def _qkv_prep(q_conv, k_conv, v_conv, heads):
    q, k, v = _silu(q_conv), _silu(k_conv), _silu(v_conv)
    qs, ks = [], []
    for h in range(heads):
        sl = slice(h * DK, (h + 1) * DK)
        qh, kh = q[:, sl], k[:, sl]
        qs.append(qh * (lax.rsqrt(jnp.sum(qh * qh, axis=-1, keepdims=True) + EPS) * DK ** -0.5))
        ks.append(kh * lax.rsqrt(jnp.sum(kh * kh, axis=-1, keepdims=True) + EPS))
    return jnp.concatenate(qs, axis=1), jnp.concatenate(ks, axis=1), v


def _gate_norm(o, onorm, gate_pre):
    var = jnp.mean(o * o, axis=-1, keepdims=True)
    return o * lax.rsqrt(var + EPS) * onorm * _silu(gate_pre)


def _seg_cumsum(g, tg, group):
    d = 1
    while d < group:
        g = jnp.where(tg >= d, g + pltpu.roll(g, d, axis=0), g)
        d *= 2
    return g


def _delta_prompt_body(q_ref, k_ref, v_ref, gt_ref, ab_ref, cwq_ref, cwk_ref, cwv_ref, alog_ref, dtb_ref,
                       on_ref, oa_ref, s_ref, qn_scr, kn_scr, vv_scr, gc_scr, bt_scr, s_scr,
                       wq_scr, uc_scr, qk_scr, kdt_scr, sd_scr):
    rows = q_ref.shape[0]
    hg = q_ref.shape[1] // DK
    qn, kn, vv = _qkv_prep(_conv_zero_past(q_ref, cwq_ref[...]), _conv_zero_past(k_ref, cwk_ref[...]),
                           _conv_zero_past(v_ref, cwv_ref[...]), hg)
    qn_scr[...] = qn
    kn_scr[...] = kn
    vv_scr[...] = vv
    ab = ab_ref[...]
    g = -jnp.exp(alog_ref[...]) * jax.nn.softplus(ab + dtb_ref[...])
    gc_scr[...] = _seg_cumsum(g, _row_iota(ab.shape, CHUNK), CHUNK)
    bt_scr[...] = jax.nn.sigmoid(ab)
    s_scr[...] = jnp.zeros_like(s_scr)
    head0 = pl.program_id(1) * hg
    onorm = on_ref[...]

    def factor(step, carry):
        lane = lax.broadcasted_iota(jnp.int32, (CHUNK, LANES), 1)
        ri = lax.broadcasted_iota(jnp.int32, (CHUNK, CHUNK), 0)
        ci = lax.broadcasted_iota(jnp.int32, (CHUNK, CHUNK), 1)
        causal = ri >= ci
        strict = ri > ci
        blk = jnp.bitwise_xor(ri, ci)
        chains = []
        for cc in range(FACTOR_CHUNKS):
            c = step * FACTOR_CHUNKS + cc
            rs = pl.ds(pl.multiple_of(c * CHUNK, CHUNK), CHUNK)
            gcc = gc_scr[rs, :]
            btc = bt_scr[rs, :]
            for hh in range(hg):
                sl = slice(hh * DK, (hh + 1) * DK)
                gcol = jnp.sum(jnp.where(lane == head0 + hh, gcc, 0.0), axis=1, keepdims=True)
                bcol = jnp.sum(jnp.where(lane == head0 + hh + H_A, btc, 0.0), axis=1, keepdims=True)
                gmat = jnp.broadcast_to(gcol, (CHUNK, CHUNK))
                decay = jnp.where(causal, jnp.exp(jnp.where(causal, gmat - gmat.T, 0.0)), 0.0)
                q = qn_scr[rs, sl]
                k = kn_scr[rs, sl]
                kq = _dot_nt(jnp.concatenate([k, q], axis=0).astype(BF16), k.astype(BF16))
                m = jnp.where(strict, bcol * kq[:CHUNK] * decay, 0.0)
                eg = jnp.exp(gcol)
                glast = jnp.broadcast_to(gcol[CHUNK - 1:CHUNK, :], (CHUNK, 1))
                wq_scr[hh, c, CHUNK:, :] = (q * eg).astype(BF16)
                qk_scr[hh, rs, :] = (kq[CHUNK:] * decay).astype(BF16)
                kdt_scr[hh, rs, :] = (k * jnp.exp(glast - gcol)).T.astype(BF16)
                sd_scr[hh, pl.ds(pl.multiple_of(c * SUBLANES, SUBLANES), SUBLANES), :] = jnp.broadcast_to(
                    jnp.broadcast_to(eg, (CHUNK, DV))[CHUNK - 1:CHUNK, :], (SUBLANES, DV))
                chains.append((hh, c, rs, sl, m, bcol, eg))
        es = [-jnp.where(lax.shift_right_logical(blk, 1) == 0, ch[4], 0.0) for ch in chains]
        lg = 1
        while (1 << lg) < CHUNK:
            cross = lax.shift_right_logical(blk, lg) == 1
            offs = [jnp.where(cross, ch[4], 0.0) for ch in chains]
            ebs = [e.astype(BF16) for e in es]
            xs = [off + _dot(eb, off.astype(BF16)) for off, eb in zip(offs, ebs)]
            es = [e - (x + _dot(x.astype(BF16), eb)) for e, x, eb in zip(es, xs, ebs)]
            lg += 1
        for (hh, c, rs, sl, m, bcol, eg), e in zip(chains, es):
            rhs = jnp.concatenate([kn_scr[rs, sl] * (bcol * eg), vv_scr[rs, sl] * bcol], axis=1)
            sol = rhs + _dot(e.astype(BF16), rhs.astype(BF16))
            wq_scr[hh, c, :CHUNK, :] = sol[:, :DK].astype(BF16)
            uc_scr[hh, rs, :] = sol[:, DK:]
        return carry

    lax.fori_loop(0, rows // (CHUNK * FACTOR_CHUNKS), factor, 0)

    def recur(c, carry):
        rs = pl.ds(pl.multiple_of(c * CHUNK, CHUNK), CHUNK)
        s_old = [s_scr[hh] for hh in range(hg)]
        ps = [_dot(wq_scr[hh, c], s_old[hh].astype(BF16)) for hh in range(hg)]
        us = [(uc_scr[hh, rs, :] - ps[hh][:CHUNK]).astype(BF16) for hh in range(hg)]
        os_ = [ps[hh][CHUNK:] + _dot(qk_scr[hh, rs, :], us[hh]) for hh in range(hg)]
        for hh in range(hg):
            sdec = sd_scr[hh, pl.ds(pl.multiple_of(c * SUBLANES, SUBLANES), SUBLANES), :]
            s_scr[hh] = s_old[hh] * sdec[:1, :] + _dot(kdt_scr[hh, rs, :], us[hh])
        for hh in range(hg):
            sl = slice(hh * DK, (hh + 1) * DK)
            oa_ref[rs, sl] = _gate_norm(os_[hh], onorm, gt_ref[rs, sl]).astype(BF16)
        return carry

    lax.fori_loop(0, rows // CHUNK, recur, 0, unroll=4)
    s_ref[...] = s_scr[...]


def _delta_prompt(proj, batch, seq_len, layer, dconv_w, alog_row, dtb_row, onorm):
    m = proj.shape[0]
    hg = HG_PROMPT
    wcols = hg * DK
    nchunks = seq_len // CHUNK
    assert nchunks % FACTOR_CHUNKS == 0
    nq, nk, nv, ng = (c // wcols for c in (COL_Q, COL_K, COL_V, COL_G))
    colspec = lambda off: pl.BlockSpec((seq_len, wcols), lambda b, j: (b, off + j))
    cwspec = lambda off: pl.BlockSpec((None, CONV_W, wcols), lambda b, j: (layer, 0, off + j))
    row = pl.BlockSpec((None, 1, LANES), lambda b, j: (layer, 0, 0))
    return pl.pallas_call(
        _delta_prompt_body,
        grid=(batch, H_A // hg),
        in_specs=[colspec(nq), colspec(nk), colspec(nv), colspec(ng),
                  pl.BlockSpec((seq_len, LANES), lambda b, j: (b, COL_AB // LANES)),
                  cwspec(nq), cwspec(nk), cwspec(nv), row, row, row],
        out_specs=[pl.BlockSpec((seq_len, wcols), lambda b, j: (b, j)),
                   pl.BlockSpec((None, hg, DK, DV), lambda b, j: (b, j, 0, 0))],
        out_shape=[jax.ShapeDtypeStruct((m, W_A), BF16), jax.ShapeDtypeStruct((batch, H_A, DK, DV), F32)],
        scratch_shapes=[pltpu.VMEM((seq_len, wcols), F32)] * 3
        + [pltpu.VMEM((seq_len, LANES), F32)] * 2 + [pltpu.VMEM((hg, DK, DV), F32)]
        + [pltpu.VMEM((hg, nchunks, 2 * CHUNK, DK), BF16), pltpu.VMEM((hg, seq_len, DV), F32),
           pltpu.VMEM((hg, seq_len, CHUNK), BF16), pltpu.VMEM((hg, seq_len, CHUNK), BF16),
           pltpu.VMEM((hg, nchunks * SUBLANES, DV), F32)],
        compiler_params=_cparams("parallel", "parallel"),
        name="delta_prompt",
    )(proj, proj, proj, proj, proj, dconv_w, dconv_w, dconv_w, alog_row, dtb_row, onorm)


def _delta_sample_body(q_ref, k_ref, v_ref, gt_ref, ab_ref, pe_ref, cw_ref, alog_ref, dtb_ref, on_ref, s0_ref,
                       *rest, seq_len):
    oa_ref, s_ref, qn_scr, kn_scr, vv_scr, gc_scr, bt_scr = rest[-7:]
    rows = q_ref.shape[0]
    t = _row_iota(q_ref.shape, seq_len)
    pe = pe_ref[...]
    cw = cw_ref[...]
    qn, kn, vv = _qkv_prep(_conv_rows(q_ref[...], pe[:, COL_Q:COL_K], cw[:, COL_Q:COL_K], t),
                           _conv_rows(k_ref[...], pe[:, COL_K:COL_V], cw[:, COL_K:COL_V], t),
                           _conv_rows(v_ref[...], pe[:, COL_V:COL_G], cw[:, COL_V:COL_G], t), H_A)
    qn_scr[...] = qn
    kn_scr[...] = kn
    vv_scr[...] = vv
    ab = ab_ref[...]
    g = -jnp.exp(alog_ref[...]) * jax.nn.softplus(ab + dtb_ref[...])
    gc_scr[...] = _seg_cumsum(g, _row_iota(ab.shape, seq_len), seq_len)
    bt_scr[...] = jax.nn.sigmoid(ab)
    onorm = on_ref[...]
    per_tile = SUBLANES // seq_len

    def tile(p, carry):
        tt = _row_iota((SUBLANES, LANES), seq_len)
        tt2 = _row_iota((SUBLANES, DK + DV), seq_len)
        ri = lax.broadcasted_iota(jnp.int32, (SUBLANES, LANES), 0)
        ri2 = _row_iota((2 * SUBLANES, LANES), SUBLANES)
        lane = lax.broadcasted_iota(jnp.int32, (LANES, LANES), 1)
        zpad = jnp.zeros((LANES - SUBLANES, DK), F32)
        rs = pl.ds(pl.multiple_of(p * SUBLANES, SUBLANES), SUBLANES)
        gcc, btc = gc_scr[rs, :], bt_scr[rs, :]
        work = [(p, rs, h, gcc, btc) for h in range(H_A)]
        staged = []
        for p, rs, h, gcc, btc in work:
            sl = slice(h * DK, (h + 1) * DK)
            gc = jnp.broadcast_to(gcc[:, h:h + 1], (SUBLANES, LANES))
            beta = jnp.broadcast_to(btc[:, H_A + h:H_A + h + 1], (SUBLANES, LANES))
            q = qn_scr[rs, sl]
            k = kn_scr[rs, sl]
            v = vv_scr[rs, sl]
            eg = jnp.exp(gc)
            decs, ms = [None], [None]
            for d in range(1, seq_len):
                ok = tt >= d
                dec = jnp.where(ok, jnp.exp(jnp.where(ok, gc - pltpu.roll(gc, d, axis=0), 0.0)), 0.0)
                kk = jnp.sum(k * pltpu.roll(k, d, axis=0), axis=-1, keepdims=True)
                decs.append(dec)
                md = beta * kk * dec
                ms.append(jnp.concatenate([md, md], axis=1))
            rhs = jnp.concatenate([k * (beta * eg), v * beta], axis=1)
            sol = rhs
            for step in range(1, seq_len):
                acc = ms[1] * pltpu.roll(sol, 1, axis=0)
                for d in range(2, seq_len):
                    acc = acc + ms[d] * pltpu.roll(sol, d, axis=0)
                sol = jnp.where(tt2 == step, rhs - acc, sol)
            lhs = jnp.concatenate([sol[:, :DK], q * eg], axis=0).astype(BF16)
            staged.append((sl, gc, q, k, eg, decs, sol[:, DK:], lhs))
        pps = []
        for (p, rs, h, _, _), st in zip(work, staged):
            lhs = st[-1]
            pp = _dot(lhs, s0_ref[p * per_tile, h].astype(BF16))
            for bb in range(1, per_tile):
                pp = jnp.where(ri2 >= bb * seq_len, _dot(lhs, s0_ref[p * per_tile + bb, h].astype(BF16)), pp)
            pps.append(pp)
        updates = []
        for (p, rs, h, _, _), st, pp in zip(work, staged, pps):
            sl, gc, q, k, eg, decs, uc, _ = st
            u = uc - pp[:SUBLANES]
            o = pp[SUBLANES:] + jnp.sum(q * k, axis=-1, keepdims=True) * u
            for d in range(1, seq_len):
                qk = jnp.sum(q * pltpu.roll(k, d, axis=0), axis=-1, keepdims=True)
                o = o + (qk * decs[d]) * pltpu.roll(u, d, axis=0)
            oa_ref[rs, sl] = _gate_norm(o, onorm, gt_ref[rs, sl]).astype(BF16)
            glast = jnp.broadcast_to(gc[seq_len - 1:seq_len, :], (SUBLANES, LANES))
            for bb in range(1, per_tile):
                last = bb * seq_len + seq_len - 1
                glast = jnp.where(ri >= bb * seq_len, jnp.broadcast_to(gc[last:last + 1, :], (SUBLANES, LANES)), glast)
            kd = k * jnp.exp(glast - gc)
            kdt = jnp.concatenate([kd, zpad], axis=0).T
            updates.append((kdt, jnp.concatenate([u, zpad], axis=0).astype(BF16)))
        for (p, rs, h, _, _), st, (kdt, upad) in zip(work, staged, updates):
            eg = st[4]
            for bb in range(per_tile):
                last = bb * seq_len + seq_len - 1
                cols = (lane >= bb * seq_len) & (lane < (bb + 1) * seq_len)
                kd_b = jnp.where(cols, kdt, 0.0).astype(BF16)
                s_ref[p * per_tile + bb, h] = (s0_ref[p * per_tile + bb, h] * eg[last:last + 1, :]
                                               + _dot(kd_b, upad))
        return carry

    lax.fori_loop(0, rows // SUBLANES, tile, 0)


def _delta_sample(proj, seq_len, layer, pe, dconv_w, alog_row, dtb_row, onorm, s0, s_stack):
    m = proj.shape[0]
    tb = TB_SAMPLE
    rows = tb * seq_len
    colspec = lambda off: pl.BlockSpec((rows, W_A), lambda i: (i, off // W_A))
    row = pl.BlockSpec((None, 1, LANES), lambda i: (layer, 0, 0))
    sspec = pl.BlockSpec((None, tb, H_A, DK, DV), lambda i: (layer, i, 0, 0, 0))
    in_specs = [colspec(COL_Q), colspec(COL_K), colspec(COL_V), colspec(COL_G),
                pl.BlockSpec((rows, LANES), lambda i: (i, COL_AB // LANES)),
                pl.BlockSpec((rows, N_DCONV), lambda i: (i, 0)),
                pl.BlockSpec((None, CONV_W, N_DCONV), lambda i: (layer, 0, 0)),
                row, row, row, sspec]
    args = [proj, proj, proj, proj, proj, pe, dconv_w, alog_row, dtb_row, onorm, s0, s_stack]
    in_specs.append(pl.BlockSpec(memory_space=pl.ANY))
    aliases = {len(args) - 1: 1}
    return pl.pallas_call(
        functools.partial(_delta_sample_body, seq_len=seq_len),
        grid=(m // rows,),
        in_specs=in_specs,
        out_specs=[pl.BlockSpec((rows, W_A), lambda i: (i, 0)), sspec],
        out_shape=[jax.ShapeDtypeStruct((m, W_A), BF16), jax.ShapeDtypeStruct(s0.shape, F32)],
        input_output_aliases=aliases,
        scratch_shapes=[pltpu.VMEM((rows, W_A), F32)] * 3 + [pltpu.VMEM((rows, LANES), F32)] * 2,
        compiler_params=_cparams("parallel"),
        name="delta_sample",
    )(*args)


def _pad_rows(state, seq_len):
    b, r, c = state.shape
    return jnp.pad(state, ((0, 0), (0, seq_len - r), (0, 0))).reshape(b * seq_len, c)


def _trunk(x, mod, batch, seq_len, wts, states, ffn_cast):
    (w_in, w_out, norm_g3, ffn_f32, dconv_w, alog_row, dtb_row, onorm, lconv_w, lconv_b,
     wa, wx, ba, bx, lam, final_g) = wts
    new_d, new_dc, new_l, new_lc = [], [], [], []
    s_stack = None if states is None else jnp.zeros_like(states[0])

    def ffn(x, layer, s, fg=None):
        if (layer, s) in ffn_cast:
            return _ffn(x, mod, norm_g3, *ffn_cast[(layer, s)], layer, s, fg)
        y, w13, w2 = _ffn(x, mod, norm_g3, ffn_f32[:2], ffn_f32[2], layer, s, fg, emit=True)
        ffn_cast[(layer, s)] = (w13, w2)
        return y

    for layer in range(DEPTH):
        x = ffn(x, layer, 0)
        proj = _inproj(x, mod, norm_g3, w_in, layer)
        if states is None:
            oa, sd = _delta_prompt(proj, batch, seq_len, layer, dconv_w, alog_row, dtb_row, onorm)
            new_d.append(sd)
            ob, hl = _lru(proj, seq_len, layer, lconv_w, lconv_b, wa, wx, ba, bx, lam)
            sl = hl.reshape(batch, SUBLANES, W_B)[:, SUBLANES - 1]
        else:
            s_delta, s_dconv, s_lru, s_lconv = states
            oa, s_stack = _delta_sample(proj, seq_len, layer, _pad_rows(s_dconv[layer], seq_len),
                                        dconv_w, alog_row, dtb_row, onorm, s_delta, s_stack)
            ob, hl = _lru(proj, seq_len, layer, lconv_w, lconv_b, wa, wx, ba, bx, lam,
                          pe=_pad_rows(s_lconv[layer], seq_len),
                          h0e=_pad_rows(s_lru[layer][:, None, :], seq_len))
            sl = hl.reshape(batch, seq_len, W_B)[:, seq_len - 1]
        x = _outproj(x, oa, ob, mod, w_out, layer)
        x = ffn(x, layer, 1, final_g if layer == DEPTH - 1 else None)
        first_row = seq_len - (CONV_W - 1)
        if seq_len % SUBLANES == 0:
            ends = proj.reshape(batch, seq_len, N_PROJ)[:, first_row:]
            last_rows = lambda lo, hi: ends[:, :, lo:hi]
        else:
            last_rows = lambda lo, hi: proj[:, lo:hi].reshape(batch, seq_len, hi - lo)[:, first_row:]
        new_dc.append(last_rows(COL_Q, COL_G))
        new_l.append(sl)
        new_lc.append(last_rows(COL_X, COL_Y))
    new_delta = jnp.stack(new_d) if states is None else s_stack
    return x, new_delta, jnp.stack(new_dc), jnp.stack(new_l), jnp.stack(new_lc)


def kernel(x_prompt, x_sample, c_prompt, c_sample, state_delta, state_delta_conv, state_lru, state_lru_conv, w_in, w_out, norm_g, w_ada, b_ada, ffn_w1, ffn_w3, ffn_w2, dconv_w, d_alog, d_dtbias, d_onorm, lconv_w, lconv_b, lru_wa, lru_ba, lru_wx, lru_bx, lru_lam, final_g):
    bp, lp, _ = x_prompt.shape
    bs, ls, _ = x_sample.shape
    assert lp % CHUNK == 0 and SUBLANES % ls == 0 and bp <= SUBLANES

    w_in_r = jnp.zeros((DEPTH, D_MODEL, N_PROJ), BF16)
    for dst, lo, hi in ((COL_Q, 0, OFF_A), (COL_X, OFF_X, N_IN), (COL_AB, OFF_A, OFF_X)):
        w_in_r = lax.dynamic_update_slice(w_in_r, w_in[..., lo:hi].astype(BF16), (0, 0, dst))
    w_out_b = w_out.astype(BF16)
    wa = lru_wa.astype(BF16)
    wx = lru_wx.astype(BF16)
    norm_g3 = norm_g.reshape(DEPTH * 3, 1, D_MODEL)
    lane_row = lambda v: jnp.pad(v, ((0, 0), (0, LANES - v.shape[1]))).reshape(DEPTH, 1, LANES)
    vec = lambda v: v.reshape(DEPTH, 1, W_B)
    wts = (w_in_r, w_out_b, norm_g3, (ffn_w1, ffn_w3, ffn_w2), dconv_w, lane_row(d_alog), lane_row(d_dtbias),
           d_onorm.reshape(DEPTH, 1, DV), lconv_w, vec(lconv_b), wa, wx, vec(lru_ba), vec(lru_bx),
           vec(lru_lam), final_g)

    ms = bs * ls
    c_all = jnp.concatenate([c_sample, c_prompt, jnp.zeros((SUBLANES - bp, D_MODEL), F32)], axis=0)
    mod_all = _ada(c_all, w_ada, b_ada, bs, ls)
    mod_p = _Mod(mod_all[:, ms:ms + bp].reshape(DEPTH, bp, 1, 9 * D_MODEL), False, lp)
    mod_s = _Mod(mod_all, True, ls)

    ffn_cast = {}
    ys, sd, sdc, sl_, slc = _trunk(x_sample.reshape(bs * ls, D_MODEL), mod_s, bs, ls, wts,
                                   (state_delta, state_delta_conv, state_lru, state_lru_conv), ffn_cast)
    yp, pd, pdc, pl_, plc = _trunk(x_prompt.reshape(bp * lp, D_MODEL), mod_p, bp, lp, wts, None, ffn_cast)
    return (yp.reshape(bp, lp, D_MODEL), ys.reshape(bs, ls, D_MODEL), pd, pdc, pl_, plc, sd, sdc, sl_, slc)
```
